```python
import math
import jax
import jax.numpy as jnp
from jax import lax
import numpy as np

D_MODEL = 1024
BATCH = 32
SEQ = 256
DEPTH = 2
DEC_BATCH = 4
DEC_SEQ = 2048
PAST_LEN = 256

GRID_W = 64
ROPE_THETA = 10000.0
EPS = 1e-6
Q_BLOCK = 128
N_MOD = 9
D_FF = 2816

DA_HEADS = 6
DA_QK = 32
DA_V = 2 * DA_QK
DA_WIDTH = DA_HEADS * DA_V
MLA_HEADS = 6
MLA_Q_RANK = 256
MLA_KV_RANK = 128
MLA_NOPE = 64
MLA_ROPE = 32
MLA_V = 64
MLA_WIDTH = MLA_HEADS * MLA_V
S5_WIDTH = D_MODEL - DA_WIDTH - MLA_WIDTH
S5_CH = 16
S5_GROUPS = S5_WIDTH // S5_CH
S5_STATE = 64

MIX_WIDTH = DA_WIDTH + MLA_WIDTH + S5_WIDTH
IN_SIZES = (DA_HEADS * 2 * DA_QK, DA_HEADS * 2 * DA_QK, DA_WIDTH, MLA_Q_RANK, MLA_KV_RANK, MLA_ROPE, S5_WIDTH)
IN_COLS = 3 * DA_WIDTH + MLA_Q_RANK + MLA_KV_RANK + MLA_ROPE + S5_WIDTH

kernel_name = 'hybrid_dit_diffattn_mla_s5_step'

F32 = jnp.float32


def rms_norm(x, w):
    xf = x.astype(F32)
    y = xf * lax.rsqrt(jnp.mean(xf * xf, axis=-1, keepdims=True) + EPS)
    return (y * w.astype(F32)).astype(x.dtype)


def modulate(h, shift, scale):
    return h * (1.0 + scale) + shift


def swiglu(h, w_in, w_out):
    a, g = jnp.split(h @ w_in, 2, axis=-1)
    return (jax.nn.silu(g) * a) @ w_out


def adaln(cvec, l, p):
    m = jax.nn.silu(cvec) @ p['w_ada'][l] + p['b_ada'][l]
    return m.reshape(cvec.shape[0], N_MOD, D_MODEL)


def grid_positions(n):
    rows = n // GRID_W
    row = jnp.repeat(jnp.arange(rows, dtype=jnp.int32), GRID_W)
    col = jnp.tile(jnp.arange(GRID_W, dtype=jnp.int32), rows)
    return row, col


def _rope_axis(x, pos):
    n = x.shape[-1] // 2
    inv = ROPE_THETA ** (-jnp.arange(n, dtype=F32) / n)
    ang = pos.astype(F32)[:, None] * inv[None, :]
    shape = (1, ang.shape[0]) + (1,) * (x.ndim - 3) + (n,)
    cos = jnp.cos(ang).reshape(shape).astype(x.dtype)
    sin = jnp.sin(ang).reshape(shape).astype(x.dtype)
    x1, x2 = x[..., :n], x[..., n:]
    return jnp.concatenate([x1 * cos - x2 * sin, x1 * sin + x2 * cos], axis=-1)


def rope_2d(x, pos):
    row, col = pos
    h = x.shape[-1] // 2
    return jnp.concatenate([_rope_axis(x[..., :h], row), _rope_axis(x[..., h:], col)], axis=-1)


def sweep_queries(fn, q):
    b, s = q.shape[:2]
    qb = jnp.moveaxis(q.reshape((b, s // Q_BLOCK, Q_BLOCK) + q.shape[2:]), 1, 0)
    out = jnp.moveaxis(lax.map(fn, qb), 0, 1)
    return out.reshape((b, s) + out.shape[3:])


def diff_attention(q, k, v, lam):
    scale = DA_QK ** -0.5
    def block(qb):
        s = jnp.einsum('bqhcd,bkhcd->bhcqk', qb, k).astype(F32) * scale
        pr = jax.nn.softmax(s, axis=-1)
        w = pr[:, :, 0] - lam * pr[:, :, 1]
        return jnp.einsum('bhqk,bkhd->bqhd', w.astype(v.dtype), v)
    return sweep_queries(block, q)


def softmax_attention(q, k, v, scale):
    def block(qb):
        s = jnp.einsum('bqhd,bkhd->bhqk', qb, k).astype(F32) * scale
        pr = jax.nn.softmax(s, axis=-1)
        return jnp.einsum('bhqk,bkhd->bqhd', pr.astype(v.dtype), v)
    return sweep_queries(block, q)


def diff_mixer(q, k, v, l, p, pos, ctx_k, ctx_v):
    b, s = q.shape[:2]
    q = q.reshape(b, s, DA_HEADS, 2, DA_QK)
    k = k.reshape(b, s, DA_HEADS, 2, DA_QK)
    v = v.reshape(b, s, DA_HEADS, DA_V)
    new = (k.reshape(b, s, DA_HEADS, 2 * DA_QK), v)
    if pos is None:
        k_all, v_all = k, v
    else:
        q = rope_2d(q, pos)
        kc = ctx_k.reshape(ctx_k.shape[0], ctx_k.shape[1], DA_HEADS, 2, DA_QK)
        k_all = jnp.concatenate([rope_2d(k, pos), kc.astype(k.dtype)], axis=1)
        v_all = jnp.concatenate([v, ctx_v.astype(v.dtype)], axis=1)
    lam_init = 0.8 - 0.6 * math.exp(-0.3 * l)
    lp = p['diff_lambda'][l].astype(F32)
    lam = jnp.exp(jnp.sum(lp[0] * lp[1])) - jnp.exp(jnp.sum(lp[2] * lp[3])) + lam_init
    o = diff_attention(q, k_all, v_all, lam)
    o = rms_norm(o, p['diff_subln_w'][l]) * (1.0 - lam_init)
    return o.reshape(b, s, DA_WIDTH), new


def mla_expand(ckv, kpe, l, p):
    b, s = ckv.shape[:2]
    kv = (ckv @ p['mla_w_kv_up'][l]).reshape(b, s, MLA_HEADS, MLA_NOPE + MLA_V)
    k_pe = jnp.broadcast_to(kpe[:, :, None, :].astype(kv.dtype), (b, s, MLA_HEADS, MLA_ROPE))
    return jnp.concatenate([kv[..., :MLA_NOPE], k_pe], axis=-1), kv[..., MLA_NOPE:]


def mla_mixer(cq, ckv, kpe, l, p, pos, ctx_ckv, ctx_kpe):
    b, s = cq.shape[:2]
    q = (rms_norm(cq, p['mla_q_norm_w'][l]) @ p['mla_w_q_up'][l]).reshape(b, s, MLA_HEADS, MLA_NOPE + MLA_ROPE)
    ckv = rms_norm(ckv, p['mla_kv_norm_w'][l])
    new = (ckv, kpe)
    if pos is None:
        k, v = mla_expand(ckv, kpe, l, p)
    else:
        q = jnp.concatenate([q[..., :MLA_NOPE], rope_2d(q[..., MLA_NOPE:], pos)], axis=-1)
        kpe_r = rope_2d(kpe[:, :, None, :], pos)[:, :, 0]
        k_l, v_l = mla_expand(ckv, kpe_r, l, p)
        k_c, v_c = mla_expand(ctx_ckv.astype(ckv.dtype), ctx_kpe, l, p)
        k = jnp.concatenate([k_l, k_c], axis=1)
        v = jnp.concatenate([v_l, v_c], axis=1)
    o = softmax_attention(q, k, v, (MLA_NOPE + MLA_ROPE) ** -0.5)
    return o.reshape(b, s, MLA_WIDTH), new


def _ssm_combine(e1, e2):
    a1, b1 = e1
    a2, b2 = e2
    return a2 * a1, a2 * b1 + b2


def s5_mixer(u, l, p, h0_re, h0_im):
    b, n = u.shape[:2]
    uf = u.astype(F32).reshape(b, n, S5_GROUPS, S5_CH)
    uc = uf.astype(jnp.complex64)
    lam = lax.complex(p['s5_a_re'][l].astype(F32), p['s5_a_im'][l].astype(F32))
    step = jnp.exp(p['s5_log_step'][l].astype(F32))[..., None]
    lam_bar = jnp.exp(lam * step)
    b_mat = lax.complex(p['s5_b_re'][l].astype(F32), p['s5_b_im'][l].astype(F32))
    b_bar = ((lam_bar - 1.0) / lam)[..., None] * b_mat
    c_mat = lax.complex(p['s5_c_re'][l].astype(F32), p['s5_c_im'][l].astype(F32))
    y = p['s5_d'][l].astype(F32) * uf
    finals = []
    for d, rev in ((0, False), (1, True)):
        bu = jnp.einsum('gpc,blgc->blgp', b_bar[d], uc)
        if h0_re is not None:
            h_init = lax.complex(h0_re[:, d].astype(F32), h0_im[:, d].astype(F32))
            edge = n - 1 if rev else 0
            bu = bu.at[:, edge].add(lam_bar[d] * h_init)
        a = jnp.broadcast_to(lam_bar[d], bu.shape)
        _, h = lax.associative_scan(_ssm_combine, (a, bu), axis=1, reverse=rev)
        y = y + jnp.real(jnp.einsum('gcp,blgp->blgc', c_mat[d], h))
        finals.append(h[:, 0] if rev else h[:, -1])
    y = jax.nn.gelu(y.reshape(b, n, S5_WIDTH))
    out = y * jax.nn.sigmoid(y @ p['s5_w_glu'][l].astype(F32) + p['s5_b_glu'][l].astype(F32))
    fin = jnp.stack(finals, axis=1)
    return out.astype(u.dtype), jnp.real(fin), jnp.imag(fin)


def mixer(h, l, p, pos, ctx):
    offs = np.cumsum(np.array(IN_SIZES))[:-1].tolist()
    qa, ka, va, cq, ckv, kpe, u = jnp.split(h @ p['w_in'][l], offs, axis=-1)
    if ctx is None:
        ctx = (None,) * 6
    o_a, new_a = diff_mixer(qa, ka, va, l, p, pos, ctx[0], ctx[1])
    o_b, new_b = mla_mixer(cq, ckv, kpe, l, p, pos, ctx[2], ctx[3])
    o_c, s_re, s_im = s5_mixer(u, l, p, ctx[4], ctx[5])
    out = jnp.concatenate([o_a, o_b, o_c], axis=-1) @ p['w_out'][l]
    return out, new_a + new_b + (s_re, s_im)


def trunk_layer(x, mod, l, p, pos, ctx):
    sh1, sc1, g1, sh2, sc2, g2, sh3, sc3, g3 = [mod[:, i, None, :] for i in range(N_MOD)]
    nw = p['norm_w'][l]
    x = x + 0.5 * g1 * swiglu(modulate(rms_norm(x, nw[0]), sh1, sc1), p['ffn_w_in'][l, 0], p['ffn_w_out'][l, 0])
    m, new_ctx = mixer(modulate(rms_norm(x, nw[1]), sh2, sc2), l, p, pos, ctx)
    x = x + g2 * m
    x = x + 0.5 * g3 * swiglu(modulate(rms_norm(x, nw[2]), sh3, sc3), p['ffn_w_in'][l, 1], p['ffn_w_out'][l, 1])
    return x, new_ctx


def setup_inputs(seed: int = 0) -> dict:
    key = jax.random.key(seed)
    ks = iter(jax.random.split(key, 48))
    def nrm(shape, s):
        return jax.random.normal(next(ks), shape, F32) * s
    def gain(shape):
        return 1.0 + nrm(shape, 0.01)
    a_im0 = jnp.pi * jnp.arange(S5_STATE, dtype=F32)
    return {
        'x_prompt': nrm((BATCH, SEQ, D_MODEL), 1.0),
        'x_sample': nrm((DEC_BATCH, DEC_SEQ, D_MODEL), 1.0),
        'cache_diff_k': nrm((DEC_BATCH, DEPTH, PAST_LEN, DA_HEADS, 2 * DA_QK), 1.0),
        'cache_diff_v': nrm((DEC_BATCH, DEPTH, PAST_LEN, DA_HEADS, DA_V), 1.0),
        'cache_mla_ckv': nrm((DEC_BATCH, DEPTH, PAST_LEN, MLA_KV_RANK), 1.0),
        'cache_mla_kpe': nrm((DEC_BATCH, DEPTH, PAST_LEN, MLA_ROPE), 1.0),
        'state_s5_re': nrm((DEC_BATCH, DEPTH, 2, S5_GROUPS, S5_STATE), 1.0),
        'state_s5_im': nrm((DEC_BATCH, DEPTH, 2, S5_GROUPS, S5_STATE), 1.0),
        'c': nrm((DEC_BATCH, D_MODEL), 1.0),
        'c_ctx': nrm((D_MODEL,), 1.0),
        'w_ada': nrm((DEPTH, D_MODEL, N_MOD * D_MODEL), 0.5 * D_MODEL ** -0.5),
        'b_ada': nrm((DEPTH, N_MOD * D_MODEL), 0.01),
        'norm_w': gain((DEPTH, 3, D_MODEL)),
        'ffn_w_in': nrm((DEPTH, 2, D_MODEL, 2 * D_FF), D_MODEL ** -0.5),
        'ffn_w_out': nrm((DEPTH, 2, D_FF, D_MODEL), D_FF ** -0.5),
        'w_in': nrm((DEPTH, D_MODEL, IN_COLS), D_MODEL ** -0.5),
        'diff_lambda': nrm((DEPTH, 4, DA_QK), 0.1),
        'diff_subln_w': gain((DEPTH, DA_V)),
        'mla_q_norm_w': gain((DEPTH, MLA_Q_RANK)),
        'mla_w_q_up': nrm((DEPTH, MLA_Q_RANK, MLA_HEADS * (MLA_NOPE + MLA_ROPE)), MLA_Q_RANK ** -0.5),
        'mla_kv_norm_w': gain((DEPTH, MLA_KV_RANK)),
        'mla_w_kv_up': nrm((DEPTH, MLA_KV_RANK, MLA_HEADS * (MLA_NOPE + MLA_V)), MLA_KV_RANK ** -0.5),
        's5_a_re': -0.5 + nrm((DEPTH, 2, S5_GROUPS, S5_STATE), 0.01),
        's5_a_im': a_im0 + nrm((DEPTH, 2, S5_GROUPS, S5_STATE), 0.01),
        's5_log_step': jax.random.uniform(next(ks), (DEPTH, 2, S5_GROUPS), F32, math.log(1e-3), math.log(1e-1)),
        's5_b_re': nrm((DEPTH, 2, S5_GROUPS, S5_STATE, S5_CH), (2 * S5_CH) ** -0.5),
        's5_b_im': nrm((DEPTH, 2, S5_GROUPS, S5_STATE, S5_CH), (2 * S5_CH) ** -0.5),
        's5_c_re': nrm((DEPTH, 2, S5_GROUPS, S5_CH, S5_STATE), (2 * S5_STATE) ** -0.5),
        's5_c_im': nrm((DEPTH, 2, S5_GROUPS, S5_CH, S5_STATE), (2 * S5_STATE) ** -0.5),
        's5_d': nrm((DEPTH, S5_GROUPS, S5_CH), 1.0),
        's5_w_glu': nrm((DEPTH, S5_WIDTH, S5_WIDTH), S5_WIDTH ** -0.5),
        's5_b_glu': nrm((DEPTH, S5_WIDTH), 0.01),
        'w_out': nrm((DEPTH, MIX_WIDTH, D_MODEL), MIX_WIDTH ** -0.5),
        'final_norm_w': gain((D_MODEL,)),
    }


def reference(x_prompt, x_sample, cache_diff_k, cache_diff_v, cache_mla_ckv, cache_mla_kpe, state_s5_re, state_s5_im, c, c_ctx, w_ada, b_ada, norm_w, ffn_w_in, ffn_w_out, w_in, diff_lambda, diff_subln_w, mla_q_norm_w, mla_w_q_up, mla_kv_norm_w, mla_w_kv_up, s5_a_re, s5_a_im, s5_log_step, s5_b_re, s5_b_im, s5_c_re, s5_c_im, s5_d, s5_w_glu, s5_b_glu, w_out, final_norm_w):
    p = dict(w_ada=w_ada, b_ada=b_ada, norm_w=norm_w, ffn_w_in=ffn_w_in, ffn_w_out=ffn_w_out, w_in=w_in,
             diff_lambda=diff_lambda, diff_subln_w=diff_subln_w, mla_q_norm_w=mla_q_norm_w,
             mla_w_q_up=mla_w_q_up, mla_kv_norm_w=mla_kv_norm_w, mla_w_kv_up=mla_w_kv_up,
             s5_a_re=s5_a_re, s5_a_im=s5_a_im, s5_log_step=s5_log_step, s5_b_re=s5_b_re, s5_b_im=s5_b_im,
             s5_c_re=s5_c_re, s5_c_im=s5_c_im, s5_d=s5_d, s5_w_glu=s5_w_glu, s5_b_glu=s5_b_glu, w_out=w_out)

    x = x_prompt
    ctx_out = []
    for l in range(DEPTH):
        x, new_ctx = trunk_layer(x, adaln(c_ctx[None, :], l, p), l, p, None, None)
        ctx_out.append(new_ctx)
    y_prompt = rms_norm(x, final_norm_w)
    new_diff_k = jnp.stack([t[0] for t in ctx_out], axis=1)
    new_diff_v = jnp.stack([t[1] for t in ctx_out], axis=1)
    new_mla_ckv = jnp.stack([t[2] for t in ctx_out], axis=1)
    new_mla_kpe = jnp.stack([t[3] for t in ctx_out], axis=1)
    new_s5_re = jnp.stack([t[4] for t in ctx_out], axis=1)
    new_s5_im = jnp.stack([t[5] for t in ctx_out], axis=1)

    pos = grid_positions(x_sample.shape[1])
    x = x_sample
    for l in range(DEPTH):
        ctx = (cache_diff_k[:, l], cache_diff_v[:, l], cache_mla_ckv[:, l], cache_mla_kpe[:, l],
               state_s5_re[:, l], state_s5_im[:, l])
        x, _ = trunk_layer(x, adaln(c, l, p), l, p, pos, ctx)
    y_sample = rms_norm(x, final_norm_w)

    return (y_prompt, y_sample, new_diff_k, new_diff_v, new_mla_ckv, new_mla_kpe, new_s5_re, new_s5_im)
```

```python
import functools
import math

import jax
import jax.numpy as jnp
import numpy as np
from jax import lax
from jax.experimental import pallas as pl
from jax.experimental.pallas import tpu as pltpu

F32 = jnp.float32
BF16 = jnp.bfloat16

D_MODEL = 1024
DEPTH = 2
GRID_W = 64
ROPE_THETA = 10000.0
EPS = 1e-6
N_MOD = 9
D_FF = 2816
DA_HEADS = 6
DA_QK = 32
DA_V = 64
DA_WIDTH = DA_HEADS * DA_V
MLA_HEADS = 6
MLA_Q_RANK = 256
MLA_KV_RANK = 128
MLA_NOPE = 64
MLA_ROPE = 32
MLA_V = 64
MLA_WIDTH = MLA_HEADS * MLA_V
S5_WIDTH = D_MODEL - DA_WIDTH - MLA_WIDTH
S5_CH = 16
S5_GROUPS = S5_WIDTH // S5_CH
S5_STATE = 64

LANES = 128
FF_CHUNK = 256
N_FF_CHUNKS = D_FF // FF_CHUNK
TOKEN_TILE = 512
Q_TILE = 256
S5_CHUNK = 16
N_SLABS = DA_WIDTH // LANES
MOD_ROWS = 8
VMEM_LIMIT = 56 * 1024 * 1024


def _rms(x, w):
    return x * lax.rsqrt(jnp.mean(x * x, axis=-1, keepdims=True) + EPS) * w


def _silu(x):
    return x * jax.nn.sigmoid(x)


def _gelu_tanh(x):
    c = math.sqrt(2.0 / math.pi)
    return x * (0.5 * (1.0 + jnp.tanh(c * (x + 0.044715 * (x * x * x)))))


def _dot(a, b):
    return jnp.dot(a, b, preferred_element_type=F32)


def _resident(shape):
    nd = len(shape)
    return pl.BlockSpec(shape, lambda *_: (0,) * nd, pipeline_mode=pl.Buffered(1))


def _adaln_kernel(c_ref, w_ref, b_ref, o_ref):
    s = _silu(c_ref[...])
    o_ref[...] = jnp.dot(s, w_ref[...], preferred_element_type=F32,
                         precision=lax.Precision.HIGHEST) + b_ref[...]


def _adaln(cvec, w_ada, b_ada):
    n = N_MOD * D_MODEL
    tn = 1152
    out = pl.pallas_call(
        _adaln_kernel,
        grid=(DEPTH, n // tn),
        in_specs=[
            pl.BlockSpec((MOD_ROWS, D_MODEL), lambda l, j: (0, 0)),
            pl.BlockSpec((None, D_MODEL, tn), lambda l, j: (l, 0, j)),
            pl.BlockSpec((None, 1, tn), lambda l, j: (l, 0, j)),
        ],
        out_specs=pl.BlockSpec((None, MOD_ROWS, tn), lambda l, j: (l, 0, j)),
        out_shape=jax.ShapeDtypeStruct((DEPTH, MOD_ROWS, n), F32),
        compiler_params=pltpu.CompilerParams(
            dimension_semantics=("arbitrary", "arbitrary"), vmem_limit_bytes=VMEM_LIMIT),
        name="adaln",
    )(cvec, w_ada, b_ada.reshape(DEPTH, 1, n))
    return out.reshape(DEPTH, MOD_ROWS, N_MOD, D_MODEL)


def _ffn_kernel(mix, final, mod_base, *refs):
    it = iter(refs)
    x_ref, mod_ref, nw_ref, wa_ref, wg_ref, wo_ref = (next(it) for _ in range(6))
    if mix:
        oa_ref, ob_ref, ys_ref, wglu_ref, bglu_ref, woa_ref, wob_ref, woc_ref = (next(it) for _ in range(8))
    if final:
        fnw_ref = next(it)
    out_ref, h_scr, acc_scr = next(it), next(it), next(it)

    x = x_ref[...]
    if mix:
        yc = _gelu_tanh(ys_ref[...])
        oc = yc * jax.nn.sigmoid(_dot(yc.astype(BF16), wglu_ref[...]) + bglu_ref[...])
        m = _dot(oc.astype(BF16), woc_ref[...])
        for s in range(N_SLABS):
            m += _dot(oa_ref[s], woa_ref[s]) + _dot(ob_ref[s], wob_ref[s])
        x = x + mod_ref[0, mod_base - 1:mod_base, :] * m

    shift = mod_ref[0, mod_base:mod_base + 1, :]
    scale = mod_ref[0, mod_base + 1:mod_base + 2, :]
    gate = mod_ref[0, mod_base + 2:mod_base + 3, :]
    h = _rms(x, nw_ref[...]) * (1.0 + scale) + shift
    h_scr[...] = h.astype(BF16)
    acc_scr[...] = jnp.zeros_like(acc_scr)

    def chunk(j, carry):
        hb = h_scr[...]
        a = _dot(hb, wa_ref[j])
        g = _dot(hb, wg_ref[j])
        acc_scr[...] += _dot((_silu(g) * a).astype(BF16), wo_ref[j])
        return carry

    lax.fori_loop(0, N_FF_CHUNKS, chunk, 0)
    y = x + 0.5 * gate * acc_scr[...]
    if final:
        y = _rms(y, fnw_ref[...])
    out_ref[...] = y


def _ffn_call(x, mod_l, seg_len, row_base, nw, ffn_w, mod_base, mix_args=None, mix_w=None, final_w=None):
    t = x.shape[0]
    tm = TOKEN_TILE
    mix = mix_args is not None
    final = final_w is not None
    tok = lambda i: (i, 0)
    in_specs = [
        pl.BlockSpec((tm, D_MODEL), tok),
        pl.BlockSpec((1, N_MOD, D_MODEL), lambda i: (row_base + (i * tm) // seg_len, 0, 0)),
        _resident((1, D_MODEL)),
        _resident((N_FF_CHUNKS, D_MODEL, FF_CHUNK)),
        _resident((N_FF_CHUNKS, D_MODEL, FF_CHUNK)),
        _resident((N_FF_CHUNKS, FF_CHUNK, D_MODEL)),
    ]
    args = [x, mod_l, nw.reshape(1, D_MODEL), *ffn_w]
    if mix:
        in_specs += [
            pl.BlockSpec((N_SLABS, tm, LANES), lambda i: (0, i, 0)),
            pl.BlockSpec((N_SLABS, tm, LANES), lambda i: (0, i, 0)),
            pl.BlockSpec((tm, S5_WIDTH), tok),
            _resident((S5_WIDTH, S5_WIDTH)),
            _resident((1, S5_WIDTH)),
            _resident((N_SLABS, LANES, D_MODEL)),
            _resident((N_SLABS, LANES, D_MODEL)),
            _resident((S5_WIDTH, D_MODEL)),
        ]
        args += [*mix_args, *mix_w]
    if final:
        in_specs.append(_resident((1, D_MODEL)))
        args.append(final_w.reshape(1, D_MODEL))
    return pl.pallas_call(
        functools.partial(_ffn_kernel, mix, final, mod_base),
        grid=(t // tm,),
        in_specs=in_specs,
        out_specs=pl.BlockSpec((tm, D_MODEL), tok),
        out_shape=jax.ShapeDtypeStruct((t, D_MODEL), F32),
        scratch_shapes=[pltpu.VMEM((tm, D_MODEL), BF16), pltpu.VMEM((tm, D_MODEL), F32)],
        compiler_params=pltpu.CompilerParams(
            dimension_semantics=("arbitrary",), vmem_limit_bytes=VMEM_LIMIT),
        name="mix_ffn" if mix else "ffn",
    )(*args)


_C_Q, _C_K, _C_V = 0, DA_WIDTH, 2 * DA_WIDTH
_C_CQ = 3 * DA_WIDTH
_C_CKV = _C_CQ + MLA_Q_RANK
_C_U = _C_CKV + MLA_KV_RANK
_C_KPE = _C_U + S5_WIDTH
_C_QSW = _C_KPE + LANES
_C_KSW = _C_QSW + DA_WIDTH
_C_KPESW = _C_KSW + DA_WIDTH
_NCOL_CTX = _C_QSW
_NCOL_DEC = _C_KPESW + LANES


def _inproj_kernel(rope, *refs):
    it = iter(refs)
    x_ref, mod_ref, nw_ref, w_ref, qnw_ref, kvnw_ref, wqup_ref = (next(it) for _ in range(7))
    if rope:
        wqsw_ref = next(it)
    wk_ref, wv_ref = next(it), next(it)
    if rope:
        cos_ref, sin_ref, cosm_ref, sinm_ref = (next(it) for _ in range(4))
    q_ref, kt_ref, v_ref, qm_ref, ktm_ref, vm_ref, u_ref = (next(it) for _ in range(7))
    if not rope:
        ka_ref, va_ref, ckv_ref, kpe_ref = (next(it) for _ in range(4))

    h = _rms(x_ref[...], nw_ref[...]) * (1.0 + mod_ref[0, 4:5, :]) + mod_ref[0, 3:4, :]
    big = _dot(h.astype(BF16), w_ref[...])

    def cols(start, width):
        return big[:, start:start + width]

    if rope:
        cos, sin, cosm, sinm = cos_ref[...], sin_ref[...], cosm_ref[...], sinm_ref[...]
    for s in range(N_SLABS):
        qs = cols(_C_Q + s * LANES, LANES)
        ks = cols(_C_K + s * LANES, LANES)
        if rope:
            qs = qs * cos + cols(_C_QSW + s * LANES, LANES) * sin
            ks = ks * cos + cols(_C_KSW + s * LANES, LANES) * sin
        q_ref[s] = qs.astype(BF16)
        kt_ref[s] = ks.T.astype(BF16)
        v_ref[s] = cols(_C_V + s * LANES, LANES).astype(BF16)
    u_ref[...] = cols(_C_U, S5_WIDTH)

    cqn = _rms(cols(_C_CQ, MLA_Q_RANK), qnw_ref[...]).astype(BF16)
    qm = _dot(cqn, wqup_ref[...])
    if rope:
        qsw = _dot(cqn, wqsw_ref[...])
    ckvn = _rms(cols(_C_CKV, MLA_KV_RANK), kvnw_ref[...])
    ckvb = ckvn.astype(BF16)
    kn = _dot(ckvb, wk_ref[...])
    vm = _dot(ckvb, wv_ref[...])
    kpe = cols(_C_KPE, LANES)
    if rope:
        kpe = kpe * cosm + cols(_C_KPESW, LANES) * sinm
    for hd in range(MLA_HEADS):
        qh = qm[:, hd * LANES:(hd + 1) * LANES]
        if rope:
            qh = qh * cosm + qsw[:, hd * LANES:(hd + 1) * LANES] * sinm
        qm_ref[hd] = qh.astype(BF16)
        ktm_ref[hd] = (kn[:, hd * LANES:(hd + 1) * LANES] + kpe).T.astype(BF16)
    for s in range(N_SLABS):
        vm_ref[s] = vm[:, s * LANES:(s + 1) * LANES].astype(BF16)

    if not rope:
        ka_ref[...] = cols(_C_K, DA_WIDTH)
        va_ref[...] = cols(_C_V, DA_WIDTH)
        ckv_ref[...] = ckvn
        kpe_ref[...] = cols(_C_KPE, MLA_ROPE)


def _inproj_call(x, mod_l, seg_len, row_base, nw, w, mla_w, rope_tabs):
    t = x.shape[0]
    tm = TOKEN_TILE
    rope = rope_tabs is not None
    ncol = _NCOL_DEC if rope else _NCOL_CTX
    qnw, kvnw, wqup, wqsw, wk, wv = mla_w
    tok = lambda i: (i, 0)
    slab = lambda i: (0, i, 0)
    slab_t = lambda i: (0, 0, i)
    in_specs = [
        pl.BlockSpec((tm, D_MODEL), tok),
        pl.BlockSpec((1, N_MOD, D_MODEL), lambda i: (row_base + (i * tm) // seg_len, 0, 0)),
        _resident((1, D_MODEL)),
        _resident((D_MODEL, ncol)),
        _resident((1, MLA_Q_RANK)),
        _resident((1, MLA_KV_RANK)),
        _resident((MLA_Q_RANK, MLA_HEADS * LANES)),
    ]
    args = [x, mod_l, nw.reshape(1, D_MODEL), w, qnw, kvnw, wqup]
    if rope:
        in_specs.append(_resident((MLA_Q_RANK, MLA_HEADS * LANES)))
        args.append(wqsw)
    in_specs += [_resident((MLA_KV_RANK, MLA_HEADS * LANES)), _resident((MLA_KV_RANK, MLA_WIDTH))]
    args += [wk, wv]
    if rope:
        n_pos = rope_tabs[0].shape[0]
        pos = lambda i: (i % (n_pos // tm), 0)
        in_specs += [pl.BlockSpec((tm, LANES), pos)] * 4
        args += list(rope_tabs)
    out_shape = [
        jax.ShapeDtypeStruct((N_SLABS, t, LANES), BF16),
        jax.ShapeDtypeStruct((N_SLABS, LANES, t), BF16),
        jax.ShapeDtypeStruct((N_SLABS, t, LANES), BF16),
        jax.ShapeDtypeStruct((MLA_HEADS, t, LANES), BF16),
        jax.ShapeDtypeStruct((MLA_HEADS, LANES, t), BF16),
        jax.ShapeDtypeStruct((N_SLABS, t, LANES), BF16),
        jax.ShapeDtypeStruct((t, S5_WIDTH), F32),
    ]
    out_specs = [
        pl.BlockSpec((N_SLABS, tm, LANES), slab),
        pl.BlockSpec((N_SLABS, LANES, tm), slab_t),
        pl.BlockSpec((N_SLABS, tm, LANES), slab),
        pl.BlockSpec((MLA_HEADS, tm, LANES), slab),
        pl.BlockSpec((MLA_HEADS, LANES, tm), slab_t),
        pl.BlockSpec((N_SLABS, tm, LANES), slab),
        pl.BlockSpec((tm, S5_WIDTH), tok),
    ]
    if not rope:
        out_shape += [
            jax.ShapeDtypeStruct((t, DA_WIDTH), F32),
            jax.ShapeDtypeStruct((t, DA_WIDTH), F32),
            jax.ShapeDtypeStruct((t, MLA_KV_RANK), F32),
            jax.ShapeDtypeStruct((t, MLA_ROPE), F32),
        ]
        out_specs += [
            pl.BlockSpec((tm, DA_WIDTH), tok),
            pl.BlockSpec((tm, DA_WIDTH), tok),
            pl.BlockSpec((tm, MLA_KV_RANK), tok),
            pl.BlockSpec((tm, MLA_ROPE), tok),
        ]
    return pl.pallas_call(
        functools.partial(_inproj_kernel, rope),
        grid=(t // tm,),
        in_specs=in_specs,
        out_specs=out_specs,
        out_shape=out_shape,
        compiler_params=pltpu.CompilerParams(
            dimension_semantics=("arbitrary",), vmem_limit_bytes=VMEM_LIMIT),
        name="inproj_dec" if rope else "inproj_ctx",
    )(*args)


def _mla_expand_kernel(ckv_ref, kpe_ref, wk_ref, wv_ref, ktm_ref, vm_ref):
    ckvb = ckv_ref[...].astype(BF16)
    kn = _dot(ckvb, wk_ref[...])
    vm = _dot(ckvb, wv_ref[...])
    kpe = kpe_ref[...]
    for hd in range(MLA_HEADS):
        ktm_ref[hd] = (kn[:, hd * LANES:(hd + 1) * LANES] + kpe).T.astype(BF16)
    for s in range(N_SLABS):
        vm_ref[s] = vm[:, s * LANES:(s + 1) * LANES].astype(BF16)


def _mla_expand_call(ckv, kpe128, wk, wv, rows):
    t = ckv.shape[0]
    return pl.pallas_call(
        _mla_expand_kernel,
        grid=(t // rows,),
        in_specs=[
            pl.BlockSpec((rows, MLA_KV_RANK), lambda i: (i, 0)),
            pl.BlockSpec((rows, LANES), lambda i: (i, 0)),
            _resident((MLA_KV_RANK, MLA_HEADS * LANES)),
            _resident((MLA_KV_RANK, MLA_WIDTH)),
        ],
        out_specs=[
            pl.BlockSpec((MLA_HEADS, LANES, rows), lambda i: (0, 0, i)),
            pl.BlockSpec((N_SLABS, rows, LANES), lambda i: (0, i, 0)),
        ],
        out_shape=[
            jax.ShapeDtypeStruct((MLA_HEADS, LANES, t), BF16),
            jax.ShapeDtypeStruct((N_SLABS, t, LANES), BF16),
        ],
        compiler_params=pltpu.CompilerParams(dimension_semantics=("arbitrary",)),
        name="mla_expand",
    )(ckv, kpe128, wk, wv)


def _softmax_parts(parts):
    m = functools.reduce(jnp.maximum, [jnp.max(s, axis=1, keepdims=True) for s in parts])
    es = [jnp.exp(s - m) for s in parts]
    tot = functools.reduce(jnp.add, [jnp.sum(e, axis=1, keepdims=True) for e in es])
    return es, 1.0 / tot


def _attn_kernel(n_parts, lam_init, *refs):
    q_ref, qm_ref, lamp_ref, subw_ref = refs[:4]
    parts = [refs[4 + 4 * i:8 + 4 * i] for i in range(n_parts)]
    oa_ref, ob_ref = refs[4 + 4 * n_parts:]

    lp = lamp_ref[...]
    lam = (jnp.exp(jnp.sum(lp[0:1] * lp[1:2], keepdims=True))
           - jnp.exp(jnp.sum(lp[2:3] * lp[3:4], keepdims=True)) + lam_init)
    tq = q_ref.shape[1]
    lane = lax.broadcasted_iota(jnp.int32, (tq, LANES), 1)
    low_half = lane < DA_V
    scale_a = DA_QK ** -0.5
    scale_b = (MLA_NOPE + MLA_ROPE) ** -0.5
    subw = subw_ref[...] * (1.0 - lam_init)

    def diff_slab(s, carry):
        qs = q_ref[s]
        heads = []
        for half in range(2):
            w = None
            for c in range(2):
                sub = 2 * half + c
                qsel = jnp.where(lane // DA_QK == sub, qs, jnp.zeros_like(qs))
                sc = [_dot(qsel, kt[s]) * scale_a for kt, _, _, _ in parts]
                es, inv = _softmax_parts(sc)
                if c == 0:
                    w = [e * inv for e in es]
                else:
                    w = [w0 - (lam * inv) * e for w0, e in zip(w, es)]
            o = functools.reduce(jnp.add, [_dot(wp.astype(BF16), v[s]) for wp, (_, v, _, _) in zip(w, parts)])
            heads.append(o)
        o = jnp.where(low_half, heads[0], heads[1])
        o2 = o * o
        ss_lo = jnp.sum(jnp.where(low_half, o2, 0.0), axis=1, keepdims=True)
        ss_hi = jnp.sum(jnp.where(low_half, 0.0, o2), axis=1, keepdims=True)
        ms = jnp.where(low_half, ss_lo, ss_hi) * (1.0 / DA_V)
        oa_ref[s] = (o * lax.rsqrt(ms + EPS) * subw).astype(BF16)
        return carry

    lax.fori_loop(0, N_SLABS, diff_slab, 0)

    def mla_slab(s, carry):
        heads = []
        for half in range(2):
            hd = 2 * s + half
            qh = qm_ref[hd]
            sc = [_dot(qh, ktm[hd]) * scale_b for _, _, ktm, _ in parts]
            es, inv = _softmax_parts(sc)
            o = functools.reduce(
                jnp.add, [_dot((e * inv).astype(BF16), vm[s]) for e, (_, _, _, vm) in zip(es, parts)])
            heads.append(o)
        ob_ref[s] = jnp.where(low_half, heads[0], heads[1]).astype(BF16)
        return carry

    lax.fori_loop(0, N_SLABS, mla_slab, 0)


def _attn_call(q, qm, lamp, subw, parts, n_batch, lam_init):
    t = q.shape[1]
    seq = t // n_batch
    tq = min(Q_TILE, seq)
    nq = seq // tq
    qmap = lambda b, i: (0, b * nq + i, 0)
    in_specs = [
        pl.BlockSpec((N_SLABS, tq, LANES), qmap),
        pl.BlockSpec((MLA_HEADS, tq, LANES), qmap),
        pl.BlockSpec((4, DA_QK), lambda b, i: (0, 0)),
        pl.BlockSpec((1, LANES), lambda b, i: (0, 0)),
    ]
    args = [q, qm, lamp, subw]
    for kt, v, ktm, vm, n_keys in parts:
        in_specs += [
            pl.BlockSpec((N_SLABS, LANES, n_keys), lambda b, i: (0, 0, b)),
            pl.BlockSpec((N_SLABS, n_keys, LANES), lambda b, i: (0, b, 0)),
            pl.BlockSpec((MLA_HEADS, LANES, n_keys), lambda b, i: (0, 0, b)),
            pl.BlockSpec((N_SLABS, n_keys, LANES), lambda b, i: (0, b, 0)),
        ]
        args += [kt, v, ktm, vm]
    return pl.pallas_call(
        functools.partial(_attn_kernel, len(parts), lam_init),
        grid=(n_batch, nq),
        in_specs=in_specs,
        out_specs=[pl.BlockSpec((N_SLABS, tq, LANES), qmap)] * 2,
        out_shape=[jax.ShapeDtypeStruct((N_SLABS, t, LANES), BF16)] * 2,
        compiler_params=pltpu.CompilerParams(
            dimension_semantics=("arbitrary", "arbitrary"), vmem_limit_bytes=VMEM_LIMIT),
        name="attn_dec" if len(parts) > 1 else "attn_ctx",
    )(*args)


def _s5_kernel(n_chunks, bp, x_ref, kmat_ref, bst_ref, cst_ref, lamt_ref, h0_ref, y_ref, fin_ref, s_scr, hin_scr):
    x = x_ref[...]
    y = _dot(x, kmat_ref[...])
    for k in range(4):
        s_scr[k] = _dot(x, bst_ref[k])

    a_fr, a_fi, a_br, a_bi = (lamt_ref[k] for k in range(4))

    def step(j, carry):
        fr, fi, br, bi = carry
        rf = pl.multiple_of(j * bp, bp)
        rb = pl.multiple_of((n_chunks - 1 - j) * bp, bp)
        hin_scr[0, pl.ds(rf, bp), :] = fr
        hin_scr[1, pl.ds(rf, bp), :] = fi
        hin_scr[2, pl.ds(rb, bp), :] = br
        hin_scr[3, pl.ds(rb, bp), :] = bi
        nfr = a_fr * fr - a_fi * fi + s_scr[0, pl.ds(rf, bp), :]
        nfi = a_fr * fi + a_fi * fr + s_scr[1, pl.ds(rf, bp), :]
        nbr = a_br * br - a_bi * bi + s_scr[2, pl.ds(rb, bp), :]
        nbi = a_br * bi + a_bi * br + s_scr[3, pl.ds(rb, bp), :]
        return nfr, nfi, nbr, nbi

    fin = lax.fori_loop(0, n_chunks, step, tuple(h0_ref[k] for k in range(4)))
    for k in range(4):
        fin_ref[k] = fin[k]
        y += _dot(hin_scr[k].astype(BF16), cst_ref[k])
    y_ref[...] = y


def _s5_call(xg, kmat, bst, cst, lamt, h0, n_chunks, bp):
    rows = n_chunks * bp
    wide = S5_CHUNK * S5_CH
    grp = lambda g: (g, 0, 0)
    grp4 = lambda g: (g, 0, 0, 0)
    return pl.pallas_call(
        functools.partial(_s5_kernel, n_chunks, bp),
        grid=(S5_GROUPS,),
        in_specs=[
            pl.BlockSpec((None, rows, wide), grp),
            pl.BlockSpec((None, wide, wide), grp),
            pl.BlockSpec((None, 4, wide, S5_STATE), grp4),
            pl.BlockSpec((None, 4, S5_STATE, wide), grp4),
            pl.BlockSpec((None, 4, 1, S5_STATE), grp4),
            pl.BlockSpec((None, 4, bp, S5_STATE), grp4),
        ],
        out_specs=[
            pl.BlockSpec((None, rows, wide), grp),
            pl.BlockSpec((None, 4, bp, S5_STATE), grp4),
        ],
        out_shape=[
            jax.ShapeDtypeStruct((S5_GROUPS, rows, wide), F32),
            jax.ShapeDtypeStruct((S5_GROUPS, 4, bp, S5_STATE), F32),
        ],
        scratch_shapes=[pltpu.VMEM((4, rows, S5_STATE), F32), pltpu.VMEM((4, rows, S5_STATE), F32)],
        compiler_params=pltpu.CompilerParams(dimension_semantics=("arbitrary",)),
        name="s5",
    )(xg, kmat, bst, cst, lamt, h0)


def _s5_operators(a_re, a_im, log_step, b_re, b_im, c_re, c_im, d):
    hi = lax.Precision.HIGHEST
    n = S5_CHUNK
    lam = lax.complex(a_re.astype(F32), a_im.astype(F32))
    z = lam * jnp.exp(log_step.astype(F32))[..., None]
    lam_bar = jnp.exp(z)
    b_bar = ((lam_bar - 1.0) / lam)[..., None] * lax.complex(b_re.astype(F32), b_im.astype(F32))
    c_mat = lax.complex(c_re.astype(F32), c_im.astype(F32))
    k = jnp.arange(n + 1, dtype=F32)
    pw = jnp.exp(k[:, None, None, None] * z[None])

    resp = jnp.real(jnp.einsum('dgcp,tdgp,dgpk->tdgck', c_mat, pw[:n], b_bar, precision=hi))
    tt = np.arange(n)
    lag = tt[None, :] - tt[:, None]
    fwd = resp[np.clip(lag, 0, n - 1), 0] * jnp.asarray(lag >= 0, F32)[:, :, None, None, None]
    bwd = resp[np.clip(-lag, 0, n - 1), 1] * jnp.asarray(lag <= 0, F32)[:, :, None, None, None]
    skip = (jnp.asarray(lag == 0, F32)[:, :, None, None, None]
            * (d.astype(F32)[:, :, None] * jnp.eye(S5_CH, dtype=F32))[None, None])
    kfull = fwd + bwd + skip
    kmat = kfull.transpose(2, 0, 4, 1, 3).reshape(S5_GROUPS, n * S5_CH, n * S5_CH)

    bf = pw[n - 1 - tt, 0][..., None] * b_bar[0][None]
    bb = pw[tt, 1][..., None] * b_bar[1][None]
    bst = jnp.stack([jnp.real(bf), jnp.imag(bf), jnp.real(bb), jnp.imag(bb)], 0)
    bst = bst.transpose(2, 0, 1, 4, 3).reshape(S5_GROUPS, 4, n * S5_CH, S5_STATE)

    cf = c_mat[0][None] * pw[tt + 1, 0][:, :, None, :]
    cb = c_mat[1][None] * pw[n - tt, 1][:, :, None, :]
    cst = jnp.stack([jnp.real(cf), -jnp.imag(cf), jnp.real(cb), -jnp.imag(cb)], 0)
    cst = cst.transpose(2, 0, 4, 1, 3).reshape(S5_GROUPS, 4, S5_STATE, n * S5_CH)

    lt = pw[n]
    lamt = jnp.stack([jnp.real(lt[0]), jnp.imag(lt[0]), jnp.real(lt[1]), jnp.imag(lt[1])], 1)
    return kmat.astype(BF16), bst.astype(BF16), cst.astype(BF16), lamt[:, :, None, :]


def _s5_mixer(u, n_batch, ops, h0):
    t = u.shape[0]
    seq = t // n_batch
    n_chunks = seq // S5_CHUNK
    bp = -(-n_batch // 8) * 8
    xg = u.astype(BF16).reshape(n_batch, n_chunks, S5_CHUNK, S5_GROUPS, S5_CH).transpose(3, 1, 0, 2, 4)
    if h0 is None:
        h0 = jnp.zeros((S5_GROUPS, 4, n_batch, S5_STATE), F32)
    if bp != n_batch:
        xg = jnp.pad(xg, ((0, 0), (0, 0), (0, bp - n_batch), (0, 0), (0, 0)))
        h0 = jnp.pad(h0, ((0, 0), (0, 0), (0, bp - n_batch), (0, 0)))
    xg = xg.reshape(S5_GROUPS, n_chunks * bp, S5_CHUNK * S5_CH)
    y, fin = _s5_call(xg, *ops, h0, n_chunks, bp)
    y = y.reshape(S5_GROUPS, n_chunks, bp, S5_CHUNK, S5_CH)[:, :, :n_batch]
    y = y.transpose(2, 1, 3, 0, 4).reshape(t, S5_WIDTH)
    return y, fin[:, :, :n_batch]


def _swap_halves(w, unit):
    n = w.shape[-1]
    idx = np.arange(n)
    half = unit // 2
    return w[..., np.where(idx % unit < half, idx + half, idx - half)]


def _rope_tables(n_pos):
    n = DA_QK // 4
    inv = ROPE_THETA ** (-jnp.arange(n, dtype=F32) / n)
    pos = jnp.arange(n_pos, dtype=jnp.int32)
    ang_r = (pos // GRID_W).astype(F32)[:, None] * inv[None, :]
    ang_c = (pos % GRID_W).astype(F32)[:, None] * inv[None, :]
    cos32 = jnp.concatenate([jnp.cos(ang_r)] * 2 + [jnp.cos(ang_c)] * 2, axis=1)
    sin32 = jnp.concatenate([-jnp.sin(ang_r), jnp.sin(ang_r), -jnp.sin(ang_c), jnp.sin(ang_c)], axis=1)
    pad = LANES - MLA_ROPE
    cosm = jnp.concatenate([cos32, jnp.ones((n_pos, pad), F32)], axis=1)
    sinm = jnp.concatenate([sin32, jnp.zeros((n_pos, pad), F32)], axis=1)
    return jnp.tile(cos32, (1, 4)), jnp.tile(sin32, (1, 4)), cosm, sinm


def _prep_ffn(w_in, w_out):
    wa = w_in[:, :D_FF].reshape(D_MODEL, N_FF_CHUNKS, FF_CHUNK).transpose(1, 0, 2).astype(BF16)
    wg = w_in[:, D_FF:].reshape(D_MODEL, N_FF_CHUNKS, FF_CHUNK).transpose(1, 0, 2).astype(BF16)
    wo = w_out.reshape(N_FF_CHUNKS, FF_CHUNK, D_MODEL).astype(BF16)
    return wa, wg, wo


def _prep_inproj(w, rope):
    o = np.cumsum([0, DA_WIDTH, DA_WIDTH, DA_WIDTH, MLA_Q_RANK, MLA_KV_RANK, MLA_ROPE, S5_WIDTH])
    wq, wk, wv, wcq, wckv, wkpe, wu = (w[:, o[i]:o[i + 1]] for i in range(7))
    kpe_pad = lambda m: jnp.pad(m, ((0, 0), (0, LANES - MLA_ROPE)))
    cols = [wq, wk, wv, wcq, wckv, wu, kpe_pad(wkpe)]
    if rope:
        half = DA_QK // 2
        cols += [_swap_halves(wq, half), _swap_halves(wk, half), kpe_pad(_swap_halves(wkpe, half))]
    return jnp.concatenate(cols, axis=1).astype(BF16)


def _prep_mla(q_norm_w, w_q_up, kv_norm_w, w_kv_up):
    pad = LANES - MLA_ROPE - MLA_NOPE
    wq = w_q_up.reshape(MLA_Q_RANK, MLA_HEADS, MLA_NOPE + MLA_ROPE)
    nope, rp = wq[..., :MLA_NOPE], wq[..., MLA_NOPE:]
    zeros = lambda n: jnp.zeros((MLA_Q_RANK, MLA_HEADS, n), F32)
    wqup = jnp.concatenate([rp, nope, zeros(pad)], axis=-1).reshape(MLA_Q_RANK, MLA_HEADS * LANES)
    wqsw = jnp.concatenate([_swap_halves(rp, DA_QK // 2), zeros(MLA_NOPE + pad)], axis=-1)
    wqsw = wqsw.reshape(MLA_Q_RANK, MLA_HEADS * LANES)
    wkv = w_kv_up.reshape(MLA_KV_RANK, MLA_HEADS, MLA_NOPE + MLA_V)
    zk = lambda n: jnp.zeros((MLA_KV_RANK, MLA_HEADS, n), F32)
    wk = jnp.concatenate([zk(MLA_ROPE), wkv[..., :MLA_NOPE], zk(pad)], axis=-1)
    wk = wk.reshape(MLA_KV_RANK, MLA_HEADS * LANES)
    wv = wkv[..., MLA_NOPE:].reshape(MLA_KV_RANK, MLA_WIDTH)
    return (q_norm_w.reshape(1, MLA_Q_RANK), kv_norm_w.reshape(1, MLA_KV_RANK),
            wqup.astype(BF16), wqsw.astype(BF16), wk.astype(BF16), wv.astype(BF16))


def _prep_mix(w_glu, b_glu, w_out):
    return (w_glu.astype(BF16), b_glu.reshape(1, S5_WIDTH),
            w_out[:DA_WIDTH].reshape(N_SLABS, LANES, D_MODEL).astype(BF16),
            w_out[DA_WIDTH:DA_WIDTH + MLA_WIDTH].reshape(N_SLABS, LANES, D_MODEL).astype(BF16),
            w_out[DA_WIDTH + MLA_WIDTH:].astype(BF16))


def _slabs_t(a, n_batch):
    b, n, _ = a.shape
    return a.reshape(b * n, N_SLABS, LANES).transpose(1, 2, 0).astype(BF16)


def _slabs(a):
    b, n, _ = a.shape
    return a.reshape(b * n, N_SLABS, LANES).transpose(1, 0, 2).astype(BF16)


def kernel(x_prompt, x_sample, cache_diff_k, cache_diff_v, cache_mla_ckv, cache_mla_kpe, state_s5_re, state_s5_im, c, c_ctx, w_ada, b_ada, norm_w, ffn_w_in, ffn_w_out, w_in, diff_lambda, diff_subln_w, mla_q_norm_w, mla_w_q_up, mla_kv_norm_w, mla_w_kv_up, s5_a_re, s5_a_im, s5_log_step, s5_b_re, s5_b_im, s5_c_re, s5_c_im, s5_d, s5_w_glu, s5_b_glu, w_out, final_norm_w):
    n_ctx, seq, _ = x_prompt.shape
    n_dec, dec_seq, _ = x_sample.shape
    past = cache_diff_k.shape[2]

    cvec = jnp.concatenate(
        [c_ctx[None, :], c, jnp.zeros((MOD_ROWS - 1 - n_dec, D_MODEL), F32)], axis=0)
    mod = _adaln(cvec, w_ada, b_ada)
    rope_tabs = _rope_tables(dec_seq)

    groups = {
        'ctx': dict(x=x_prompt.reshape(n_ctx * seq, D_MODEL), seg=n_ctx * seq, row=0, nb=n_ctx),
        'dec': dict(x=x_sample.reshape(n_dec * dec_seq, D_MODEL), seg=dec_seq, row=1, nb=n_dec),
    }
    ctx_out = []
    for l in range(DEPTH):
        lam_init = 0.8 - 0.6 * math.exp(-0.3 * l)
        ffn1 = _prep_ffn(ffn_w_in[l, 0], ffn_w_out[l, 0])
        ffn2 = _prep_ffn(ffn_w_in[l, 1], ffn_w_out[l, 1])
        mla_w = _prep_mla(mla_q_norm_w[l], mla_w_q_up[l], mla_kv_norm_w[l], mla_w_kv_up[l])
        mix_w = _prep_mix(s5_w_glu[l], s5_b_glu[l], w_out[l])
        s5_ops = _s5_operators(s5_a_re[l], s5_a_im[l], s5_log_step[l], s5_b_re[l], s5_b_im[l],
                               s5_c_re[l], s5_c_im[l], s5_d[l])
        subw = jnp.tile(diff_subln_w[l], 2).reshape(1, LANES)
        final_w = final_norm_w if l == DEPTH - 1 else None
        for name, g in groups.items():
            dec = name == 'dec'
            x = _ffn_call(g['x'], mod[l], g['seg'], g['row'], norm_w[l, 0], ffn1, 0)
            proj = _inproj_call(x, mod[l], g['seg'], g['row'], norm_w[l, 1], _prep_inproj(w_in[l], dec),
                                mla_w, rope_tabs if dec else None)
            q, kt, v, qm, ktm, vm, u = proj[:7]
            n_keys = g['x'].shape[0] // g['nb']
            parts = [(kt, v, ktm, vm, n_keys)]
            h0 = None
            if dec:
                ck = cache_diff_k[:, l].reshape(n_dec, past, DA_WIDTH)
                cv = cache_diff_v[:, l].reshape(n_dec, past, DA_WIDTH)
                kpe128 = jnp.pad(cache_mla_kpe[:, l].reshape(n_dec * past, MLA_ROPE),
                                 ((0, 0), (0, LANES - MLA_ROPE)))
                cktm, cvm = _mla_expand_call(cache_mla_ckv[:, l].reshape(n_dec * past, MLA_KV_RANK),
                                             kpe128, mla_w[4], mla_w[5], past)
                parts.append((_slabs_t(ck, n_dec), _slabs(cv), cktm, cvm, past))
                sr, si = state_s5_re[:, l], state_s5_im[:, l]
                h0 = jnp.stack([sr[:, 0], si[:, 0], sr[:, 1], si[:, 1]], 0).transpose(2, 0, 1, 3)
            o_a, o_b = _attn_call(q, qm, diff_lambda[l], subw, parts, g['nb'], lam_init)
            y_s5, fin = _s5_mixer(u, g['nb'], s5_ops, h0)
            g['x'] = _ffn_call(x, mod[l], g['seg'], g['row'], norm_w[l, 2], ffn2, 6,
                               mix_args=(o_a, o_b, y_s5), mix_w=mix_w, final_w=final_w)
            if not dec:
                ka, va, ckvn, kpe = proj[7:]
                fin = fin.transpose(1, 2, 0, 3)
                ctx_out.append((
                    ka.reshape(n_ctx, seq, DA_HEADS, 2 * DA_QK),
                    va.reshape(n_ctx, seq, DA_HEADS, DA_V),
                    ckvn.reshape(n_ctx, seq, MLA_KV_RANK),
                    kpe.reshape(n_ctx, seq, MLA_ROPE),
                    jnp.stack([fin[0], fin[2]], axis=1),
                    jnp.stack([fin[1], fin[3]], axis=1),
                ))

    y_prompt = groups['ctx']['x'].reshape(n_ctx, seq, D_MODEL)
    y_sample = groups['dec']['x'].reshape(n_dec, dec_seq, D_MODEL)
    new = [jnp.stack([t[i] for t in ctx_out], axis=1) for i in range(6)]
    return (y_prompt, y_sample, *new)
```

```python
import functools
import math

import jax
import jax.numpy as jnp
import numpy as np
from jax import lax
from jax.experimental import pallas as pl
from jax.experimental.pallas import tpu as pltpu

F32 = jnp.float32
BF16 = jnp.bfloat16

D_MODEL = 1024
DEPTH = 2
GRID_W = 64
ROPE_THETA = 10000.0
EPS = 1e-6
N_MOD = 9
D_FF = 2816
DA_HEADS = 6
DA_QK = 32
DA_V = 64
DA_WIDTH = DA_HEADS * DA_V
MLA_HEADS = 6
MLA_Q_RANK = 256
MLA_KV_RANK = 128
MLA_NOPE = 64
MLA_ROPE = 32
MLA_V = 64
MLA_WIDTH = MLA_HEADS * MLA_V
S5_WIDTH = D_MODEL - DA_WIDTH - MLA_WIDTH
S5_CH = 16
S5_GROUPS = S5_WIDTH // S5_CH
S5_STATE = 64

LANES = 128
FF_CHUNK = 256
N_FF_CHUNKS = D_FF // FF_CHUNK
TOKEN_TILE = 512
Q_TILE = 256
S5_CHUNK = 16
N_SLABS = DA_WIDTH // LANES
MOD_ROWS = 8
VMEM_LIMIT = 56 * 1024 * 1024


def _rms(x, w):
    return x * lax.rsqrt(jnp.mean(x * x, axis=-1, keepdims=True) + EPS) * w


def _silu(x):
    return x * jax.nn.sigmoid(x)


def _gelu_tanh(x):
    c = math.sqrt(2.0 / math.pi)
    return x * (0.5 * (1.0 + jnp.tanh(c * (x + 0.044715 * (x * x * x)))))


def _dot(a, b):
    return jnp.dot(a, b, preferred_element_type=F32)


def _resident(shape):
    nd = len(shape)
    return pl.BlockSpec(shape, lambda *_: (0,) * nd, pipeline_mode=pl.Buffered(1))


def _adaln_kernel(c_ref, w_ref, b_ref, o_ref):
    s = _silu(c_ref[...])
    o_ref[...] = jnp.dot(s, w_ref[...], preferred_element_type=F32,
                         precision=lax.Precision.HIGHEST) + b_ref[...]


def _adaln(cvec, w_ada, b_ada):
    n = N_MOD * D_MODEL
    tn = 1152
    out = pl.pallas_call(
        _adaln_kernel,
        grid=(DEPTH, n // tn),
        in_specs=[
            pl.BlockSpec((MOD_ROWS, D_MODEL), lambda l, j: (0, 0)),
            pl.BlockSpec((None, D_MODEL, tn), lambda l, j: (l, 0, j)),
            pl.BlockSpec((None, 1, tn), lambda l, j: (l, 0, j)),
        ],
        out_specs=pl.BlockSpec((None, MOD_ROWS, tn), lambda l, j: (l, 0, j)),
        out_shape=jax.ShapeDtypeStruct((DEPTH, MOD_ROWS, n), F32),
        compiler_params=pltpu.CompilerParams(
            dimension_semantics=("arbitrary", "arbitrary"), vmem_limit_bytes=VMEM_LIMIT),
        name="adaln",
    )(cvec, w_ada, b_ada.reshape(DEPTH, 1, n))
    return out.reshape(DEPTH, MOD_ROWS, N_MOD, D_MODEL)


_ROWS_PER_TILE = TOKEN_TILE // S5_CHUNK
_SLOTS = LANES // S5_CH


def _to_group_major(u_scr, xg_ref):
    slot = lax.broadcasted_iota(jnp.int32, (_ROWS_PER_TILE, LANES), 1) // S5_CH
    for g in range(S5_GROUPS):
        for tt in range(S5_CHUNK // _SLOTS):
            acc = None
            for j in range(_SLOTS):
                t = tt * _SLOTS + j
                src = u_scr[g // _SLOTS, pl.ds(t, _ROWS_PER_TILE, stride=S5_CHUNK), :]
                shift = ((j - g % _SLOTS) % _SLOTS) * S5_CH
                piece = pltpu.roll(src, shift, 1) if shift else src
                acc = piece if acc is None else jnp.where(slot == j, piece, acc)
            xg_ref[g, :, tt * LANES:(tt + 1) * LANES] = acc.astype(BF16)


def _from_group_major(yg_ref, y_scr):
    slot = lax.broadcasted_iota(jnp.int32, (_ROWS_PER_TILE, LANES), 1) // S5_CH
    for t in range(S5_CHUNK):
        for gc in range(S5_GROUPS // _SLOTS):
            acc = None
            for m in range(_SLOTS):
                src = yg_ref[gc * _SLOTS + m, :, (t // _SLOTS) * LANES:(t // _SLOTS + 1) * LANES]
                shift = ((m - t % _SLOTS) % _SLOTS) * S5_CH
                piece = pltpu.roll(src, shift, 1) if shift else src
                acc = piece if acc is None else jnp.where(slot == m, piece, acc)
            y_scr[gc, pl.ds(t, _ROWS_PER_TILE, stride=S5_CHUNK), :] = acc


def _ffn_kernel(mix, final, mod_base, *refs):
    it = iter(refs)
    x_ref, mod_ref, nw_ref, wa_ref, wg_ref, wo_ref = (next(it) for _ in range(6))
    if mix:
        oa_ref, ob_ref, ys_ref, wglu_ref, bglu_ref, woa_ref, wob_ref, woc_ref = (next(it) for _ in range(8))
    if final:
        fnw_ref = next(it)
    out_ref, h_scr, acc_scr = next(it), next(it), next(it)

    x = x_ref[...]
    if mix:
        y_scr = next(it)
        _from_group_major(ys_ref, y_scr)
        yc = _gelu_tanh(jnp.concatenate([y_scr[gc] for gc in range(S5_WIDTH // LANES)], axis=1))
        oc = yc * jax.nn.sigmoid(_dot(yc.astype(BF16), wglu_ref[...]) + bglu_ref[...])
        m = _dot(oc.astype(BF16), woc_ref[...])
        for s in range(N_SLABS):
            m += _dot(oa_ref[s], woa_ref[s]) + _dot(ob_ref[s], wob_ref[s])
        x = x + mod_ref[0, mod_base - 1:mod_base, :] * m

    shift = mod_ref[0, mod_base:mod_base + 1, :]
    scale = mod_ref[0, mod_base + 1:mod_base + 2, :]
    gate = mod_ref[0, mod_base + 2:mod_base + 3, :]
    h = _rms(x, nw_ref[...]) * (1.0 + scale) + shift
    h_scr[...] = h.astype(BF16)
    acc_scr[...] = jnp.zeros_like(acc_scr)

    def chunk(j, carry):
        hb = h_scr[...]
        a = _dot(hb, wa_ref[j])
        g = _dot(hb, wg_ref[j])
        acc_scr[...] += _dot((_silu(g) * a).astype(BF16), wo_ref[j])
        return carry

    lax.fori_loop(0, N_FF_CHUNKS, chunk, 0)
    y = x + 0.5 * gate * acc_scr[...]
    if final:
        y = _rms(y, fnw_ref[...])
    out_ref[...] = y


def _ffn_call(x, mod_l, seg_len, row_base, nw, ffn_w, mod_base, mix_args=None, mix_w=None, final_w=None):
    t = x.shape[0]
    tm = TOKEN_TILE
    mix = mix_args is not None
    final = final_w is not None
    tok = lambda i: (i, 0)
    in_specs = [
        pl.BlockSpec((tm, D_MODEL), tok),
        pl.BlockSpec((1, N_MOD, D_MODEL), lambda i: (row_base + (i * tm) // seg_len, 0, 0)),
        _resident((1, D_MODEL)),
        _resident((N_FF_CHUNKS, D_MODEL, FF_CHUNK)),
        _resident((N_FF_CHUNKS, D_MODEL, FF_CHUNK)),
        _resident((N_FF_CHUNKS, FF_CHUNK, D_MODEL)),
    ]
    args = [x, mod_l, nw.reshape(1, D_MODEL), *ffn_w]
    if mix:
        in_specs += [
            pl.BlockSpec((N_SLABS, tm, LANES), lambda i: (0, i, 0)),
            pl.BlockSpec((N_SLABS, tm, LANES), lambda i: (0, i, 0)),
            pl.BlockSpec((S5_GROUPS, _ROWS_PER_TILE, S5_CHUNK * S5_CH), lambda i: (0, i, 0)),
            _resident((S5_WIDTH, S5_WIDTH)),
            _resident((1, S5_WIDTH)),
            _resident((N_SLABS, LANES, D_MODEL)),
            _resident((N_SLABS, LANES, D_MODEL)),
            _resident((S5_WIDTH, D_MODEL)),
        ]
        args += [*mix_args, *mix_w]
    if final:
        in_specs.append(_resident((1, D_MODEL)))
        args.append(final_w.reshape(1, D_MODEL))
    return pl.pallas_call(
        functools.partial(_ffn_kernel, mix, final, mod_base),
        grid=(t // tm,),
        in_specs=in_specs,
        out_specs=pl.BlockSpec((tm, D_MODEL), tok),
        out_shape=jax.ShapeDtypeStruct((t, D_MODEL), F32),
        scratch_shapes=[pltpu.VMEM((tm, D_MODEL), BF16), pltpu.VMEM((tm, D_MODEL), F32)]
        + ([pltpu.VMEM((S5_WIDTH // LANES, tm, LANES), F32)] if mix else []),
        compiler_params=pltpu.CompilerParams(
            dimension_semantics=("arbitrary",), vmem_limit_bytes=VMEM_LIMIT),
        name="mix_ffn" if mix else "ffn",
    )(*args)


_C_Q, _C_K, _C_V = 0, DA_WIDTH, 2 * DA_WIDTH
_C_CQ = 3 * DA_WIDTH
_C_CKV = _C_CQ + MLA_Q_RANK
_C_U = _C_CKV + MLA_KV_RANK
_C_KPE = _C_U + S5_WIDTH
_C_QSW = _C_KPE + LANES
_C_KSW = _C_QSW + DA_WIDTH
_C_KPESW = _C_KSW + DA_WIDTH
_NCOL_CTX = _C_QSW
_NCOL_DEC = _C_KPESW + LANES


def _inproj_kernel(rope, *refs):
    it = iter(refs)
    x_ref, mod_ref, nw_ref, w_ref, qnw_ref, kvnw_ref, wqup_ref = (next(it) for _ in range(7))
    if rope:
        wqsw_ref = next(it)
    wk_ref, wv_ref = next(it), next(it)
    if rope:
        cos_ref, sin_ref, cosm_ref, sinm_ref = (next(it) for _ in range(4))
    q_ref, kt_ref, v_ref, qm_ref, ktm_ref, vm_ref, xg_ref = (next(it) for _ in range(7))
    if not rope:
        ka_ref, va_ref, ckv_ref, kpe_ref = (next(it) for _ in range(4))
    u_scr = next(it)

    h = _rms(x_ref[...], nw_ref[...]) * (1.0 + mod_ref[0, 4:5, :]) + mod_ref[0, 3:4, :]
    big = _dot(h.astype(BF16), w_ref[...])

    def cols(start, width):
        return big[:, start:start + width]

    if rope:
        cos, sin, cosm, sinm = cos_ref[...], sin_ref[...], cosm_ref[...], sinm_ref[...]
    for s in range(N_SLABS):
        qs = cols(_C_Q + s * LANES, LANES)
        ks = cols(_C_K + s * LANES, LANES)
        if rope:
            qs = qs * cos + cols(_C_QSW + s * LANES, LANES) * sin
            ks = ks * cos + cols(_C_KSW + s * LANES, LANES) * sin
        q_ref[s] = qs.astype(BF16)
        kt_ref[s] = ks.T.astype(BF16)
        v_ref[s] = cols(_C_V + s * LANES, LANES).astype(BF16)
    for gc in range(S5_WIDTH // LANES):
        u_scr[gc] = cols(_C_U + gc * LANES, LANES)
    _to_group_major(u_scr, xg_ref)

    cqn = _rms(cols(_C_CQ, MLA_Q_RANK), qnw_ref[...]).astype(BF16)
    qm = _dot(cqn, wqup_ref[...])
    if rope:
        qsw = _dot(cqn, wqsw_ref[...])
    ckvn = _rms(cols(_C_CKV, MLA_KV_RANK), kvnw_ref[...])
    ckvb = ckvn.astype(BF16)
    kn = _dot(ckvb, wk_ref[...])
    vm = _dot(ckvb, wv_ref[...])
    kpe = cols(_C_KPE, LANES)
    if rope:
        kpe = kpe * cosm + cols(_C_KPESW, LANES) * sinm
    for hd in range(MLA_HEADS):
        qh = qm[:, hd * LANES:(hd + 1) * LANES]
        if rope:
            qh = qh * cosm + qsw[:, hd * LANES:(hd + 1) * LANES] * sinm
        qm_ref[hd] = qh.astype(BF16)
        ktm_ref[hd] = (kn[:, hd * LANES:(hd + 1) * LANES] + kpe).T.astype(BF16)
    for s in range(N_SLABS):
        vm_ref[s] = vm[:, s * LANES:(s + 1) * LANES].astype(BF16)

    if not rope:
        ka_ref[...] = cols(_C_K, DA_WIDTH)
        va_ref[...] = cols(_C_V, DA_WIDTH)
        ckv_ref[...] = ckvn
        kpe_ref[...] = cols(_C_KPE, MLA_ROPE)


def _inproj_call(x, mod_l, seg_len, row_base, nw, w, mla_w, rope_tabs):
    t = x.shape[0]
    tm = TOKEN_TILE
    rope = rope_tabs is not None
    ncol = _NCOL_DEC if rope else _NCOL_CTX
    qnw, kvnw, wqup, wqsw, wk, wv = mla_w
    tok = lambda i: (i, 0)
    slab = lambda i: (0, i, 0)
    slab_t = lambda i: (0, 0, i)
    in_specs = [
        pl.BlockSpec((tm, D_MODEL), tok),
        pl.BlockSpec((1, N_MOD, D_MODEL), lambda i: (row_base + (i * tm) // seg_len, 0, 0)),
        _resident((1, D_MODEL)),
        _resident((D_MODEL, ncol)),
        _resident((1, MLA_Q_RANK)),
        _resident((1, MLA_KV_RANK)),
        _resident((MLA_Q_RANK, MLA_HEADS * LANES)),
    ]
    args = [x, mod_l, nw.reshape(1, D_MODEL), w, qnw, kvnw, wqup]
    if rope:
        in_specs.append(_resident((MLA_Q_RANK, MLA_HEADS * LANES)))
        args.append(wqsw)
    in_specs += [_resident((MLA_KV_RANK, MLA_HEADS * LANES)), _resident((MLA_KV_RANK, MLA_WIDTH))]
    args += [wk, wv]
    if rope:
        n_pos = rope_tabs[0].shape[0]
        pos = lambda i: (i % (n_pos // tm), 0)
        in_specs += [pl.BlockSpec((tm, LANES), pos)] * 4
        args += list(rope_tabs)
    out_shape = [
        jax.ShapeDtypeStruct((N_SLABS, t, LANES), BF16),
        jax.ShapeDtypeStruct((N_SLABS, LANES, t), BF16),
        jax.ShapeDtypeStruct((N_SLABS, t, LANES), BF16),
        jax.ShapeDtypeStruct((MLA_HEADS, t, LANES), BF16),
        jax.ShapeDtypeStruct((MLA_HEADS, LANES, t), BF16),
        jax.ShapeDtypeStruct((N_SLABS, t, LANES), BF16),
        jax.ShapeDtypeStruct((S5_GROUPS, t // S5_CHUNK, S5_CHUNK * S5_CH), BF16),
    ]
    out_specs = [
        pl.BlockSpec((N_SLABS, tm, LANES), slab),
        pl.BlockSpec((N_SLABS, LANES, tm), slab_t),
        pl.BlockSpec((N_SLABS, tm, LANES), slab),
        pl.BlockSpec((MLA_HEADS, tm, LANES), slab),
        pl.BlockSpec((MLA_HEADS, LANES, tm), slab_t),
        pl.BlockSpec((N_SLABS, tm, LANES), slab),
        pl.BlockSpec((S5_GROUPS, _ROWS_PER_TILE, S5_CHUNK * S5_CH), slab),
    ]
    if not rope:
        out_shape += [
            jax.ShapeDtypeStruct((t, DA_WIDTH), F32),
            jax.ShapeDtypeStruct((t, DA_WIDTH), F32),
            jax.ShapeDtypeStruct((t, MLA_KV_RANK), F32),
            jax.ShapeDtypeStruct((t, MLA_ROPE), F32),
        ]
        out_specs += [
            pl.BlockSpec((tm, DA_WIDTH), tok),
            pl.BlockSpec((tm, DA_WIDTH), tok),
            pl.BlockSpec((tm, MLA_KV_RANK), tok),
            pl.BlockSpec((tm, MLA_ROPE), tok),
        ]
    return pl.pallas_call(
        functools.partial(_inproj_kernel, rope),
        grid=(t // tm,),
        in_specs=in_specs,
        out_specs=out_specs,
        out_shape=out_shape,
        scratch_shapes=[pltpu.VMEM((S5_WIDTH // LANES, tm, LANES), F32)],
        compiler_params=pltpu.CompilerParams(
            dimension_semantics=("arbitrary",), vmem_limit_bytes=VMEM_LIMIT),
        name="inproj_dec" if rope else "inproj_ctx",
    )(*args)


def _mla_expand_kernel(ckv_ref, kpe_ref, wk_ref, wv_ref, ktm_ref, vm_ref):
    ckvb = ckv_ref[...].astype(BF16)
    kn = _dot(ckvb, wk_ref[...])
    vm = _dot(ckvb, wv_ref[...])
    kpe = kpe_ref[...]
    for hd in range(MLA_HEADS):
        ktm_ref[hd] = (kn[:, hd * LANES:(hd + 1) * LANES] + kpe).T.astype(BF16)
    for s in range(N_SLABS):
        vm_ref[s] = vm[:, s * LANES:(s + 1) * LANES].astype(BF16)


def _mla_expand_call(ckv, kpe128, wk, wv, rows):
    t = ckv.shape[0]
    return pl.pallas_call(
        _mla_expand_kernel,
        grid=(t // rows,),
        in_specs=[
            pl.BlockSpec((rows, MLA_KV_RANK), lambda i: (i, 0)),
            pl.BlockSpec((rows, LANES), lambda i: (i, 0)),
            _resident((MLA_KV_RANK, MLA_HEADS * LANES)),
            _resident((MLA_KV_RANK, MLA_WIDTH)),
        ],
        out_specs=[
            pl.BlockSpec((MLA_HEADS, LANES, rows), lambda i: (0, 0, i)),
            pl.BlockSpec((N_SLABS, rows, LANES), lambda i: (0, i, 0)),
        ],
        out_shape=[
            jax.ShapeDtypeStruct((MLA_HEADS, LANES, t), BF16),
            jax.ShapeDtypeStruct((N_SLABS, t, LANES), BF16),
        ],
        compiler_params=pltpu.CompilerParams(dimension_semantics=("arbitrary",)),
        name="mla_expand",
    )(ckv, kpe128, wk, wv)


def _softmax_parts(parts):
    m = functools.reduce(jnp.maximum, [jnp.max(s, axis=1, keepdims=True) for s in parts])
    es = [jnp.exp(s - m) for s in parts]
    tot = functools.reduce(jnp.add, [jnp.sum(e, axis=1, keepdims=True) for e in es])
    return es, 1.0 / tot


def _attn_kernel(n_parts, lam_init, *refs):
    q_ref, qm_ref, lamp_ref, subw_ref = refs[:4]
    parts = [refs[4 + 4 * i:8 + 4 * i] for i in range(n_parts)]
    oa_ref, ob_ref = refs[4 + 4 * n_parts:]

    lp = lamp_ref[...]
    lam = (jnp.exp(jnp.sum(lp[0:1] * lp[1:2], keepdims=True))
           - jnp.exp(jnp.sum(lp[2:3] * lp[3:4], keepdims=True)) + lam_init)
    tq = q_ref.shape[1]
    lane = lax.broadcasted_iota(jnp.int32, (tq, LANES), 1)
    low_half = lane < DA_V
    scale_a = DA_QK ** -0.5
    scale_b = (MLA_NOPE + MLA_ROPE) ** -0.5
    subw = subw_ref[...] * (1.0 - lam_init)

    def diff_slab(s, carry):
        qs = q_ref[s]
        heads = []
        for half in range(2):
            w = None
            for c in range(2):
                sub = 2 * half + c
                qsel = jnp.where(lane // DA_QK == sub, qs, jnp.zeros_like(qs))
                sc = [_dot(qsel, kt[s]) * scale_a for kt, _, _, _ in parts]
                es, inv = _softmax_parts(sc)
                if c == 0:
                    w = [e * inv for e in es]
                else:
                    w = [w0 - (lam * inv) * e for w0, e in zip(w, es)]
            o = functools.reduce(jnp.add, [_dot(wp.astype(BF16), v[s]) for wp, (_, v, _, _) in zip(w, parts)])
            heads.append(o)
        o = jnp.where(low_half, heads[0], heads[1])
        o2 = o * o
        ss_lo = jnp.sum(jnp.where(low_half, o2, 0.0), axis=1, keepdims=True)
        ss_hi = jnp.sum(jnp.where(low_half, 0.0, o2), axis=1, keepdims=True)
        ms = jnp.where(low_half, ss_lo, ss_hi) * (1.0 / DA_V)
        oa_ref[s] = (o * lax.rsqrt(ms + EPS) * subw).astype(BF16)
        return carry

    lax.fori_loop(0, N_SLABS, diff_slab, 0)

    def mla_slab(s, carry):
        heads = []
        for half in range(2):
            hd = 2 * s + half
            qh = qm_ref[hd]
            sc = [_dot(qh, ktm[hd]) * scale_b for _, _, ktm, _ in parts]
            es, inv = _softmax_parts(sc)
            o = functools.reduce(
                jnp.add, [_dot((e * inv).astype(BF16), vm[s]) for e, (_, _, _, vm) in zip(es, parts)])
            heads.append(o)
        ob_ref[s] = jnp.where(low_half, heads[0], heads[1]).astype(BF16)
        return carry

    lax.fori_loop(0, N_SLABS, mla_slab, 0)


def _attn_call(q, qm, lamp, subw, parts, n_batch, lam_init):
    t = q.shape[1]
    seq = t // n_batch
    tq = min(Q_TILE, seq)
    nq = seq // tq
    qmap = lambda b, i: (0, b * nq + i, 0)
    in_specs = [
        pl.BlockSpec((N_SLABS, tq, LANES), qmap),
        pl.BlockSpec((MLA_HEADS, tq, LANES), qmap),
        pl.BlockSpec((4, DA_QK), lambda b, i: (0, 0)),
        pl.BlockSpec((1, LANES), lambda b, i: (0, 0)),
    ]
    args = [q, qm, lamp, subw]
    for kt, v, ktm, vm, n_keys in parts:
        in_specs += [
            pl.BlockSpec((N_SLABS, LANES, n_keys), lambda b, i: (0, 0, b)),
            pl.BlockSpec((N_SLABS, n_keys, LANES), lambda b, i: (0, b, 0)),
            pl.BlockSpec((MLA_HEADS, LANES, n_keys), lambda b, i: (0, 0, b)),
            pl.BlockSpec((N_SLABS, n_keys, LANES), lambda b, i: (0, b, 0)),
        ]
        args += [kt, v, ktm, vm]
    return pl.pallas_call(
        functools.partial(_attn_kernel, len(parts), lam_init),
        grid=(n_batch, nq),
        in_specs=in_specs,
        out_specs=[pl.BlockSpec((N_SLABS, tq, LANES), qmap)] * 2,
        out_shape=[jax.ShapeDtypeStruct((N_SLABS, t, LANES), BF16)] * 2,
        compiler_params=pltpu.CompilerParams(
            dimension_semantics=("arbitrary", "arbitrary"), vmem_limit_bytes=VMEM_LIMIT),
        name="attn_dec" if len(parts) > 1 else "attn_ctx",
    )(*args)


def _s5_kernel(n_batch, n_chunks, x_ref, kmat_ref, bst_ref, cst_ref, lamt_ref, h0_ref, y_ref, fin_ref,
               s_scr, hin_scr):
    x = x_ref[...]
    y = _dot(x, kmat_ref[...])
    for k in range(4):
        s_scr[k] = _dot(x, bst_ref[k])

    a_fr, a_fi, a_br, a_bi = (lamt_ref[k] for k in range(4))

    def chunk_rows(j):
        return pl.ds(j, n_batch, stride=n_chunks)

    def step(j, carry):
        fr, fi, br, bi = carry
        rf, rb = chunk_rows(j), chunk_rows(n_chunks - 1 - j)
        hin_scr[0, rf, :] = fr
        hin_scr[1, rf, :] = fi
        hin_scr[2, rb, :] = br
        hin_scr[3, rb, :] = bi
        nfr = a_fr * fr - a_fi * fi + s_scr[0, rf, :]
        nfi = a_fr * fi + a_fi * fr + s_scr[1, rf, :]
        nbr = a_br * br - a_bi * bi + s_scr[2, rb, :]
        nbi = a_br * bi + a_bi * br + s_scr[3, rb, :]
        return nfr, nfi, nbr, nbi

    fin = lax.fori_loop(0, n_chunks, step, tuple(h0_ref[k] for k in range(4)))
    for k in range(4):
        fin_ref[k] = fin[k]
        y += _dot(hin_scr[k].astype(BF16), cst_ref[k])
    y_ref[...] = y


def _s5_call(xg, kmat, bst, cst, lamt, h0, n_batch):
    rows = xg.shape[1]
    n_chunks = rows // n_batch
    wide = S5_CHUNK * S5_CH
    grp = lambda g: (g, 0, 0)
    grp4 = lambda g: (g, 0, 0, 0)
    return pl.pallas_call(
        functools.partial(_s5_kernel, n_batch, n_chunks),
        grid=(S5_GROUPS,),
        in_specs=[
            pl.BlockSpec((None, rows, wide), grp),
            pl.BlockSpec((None, wide, wide), grp),
            pl.BlockSpec((None, 4, wide, LANES), grp4),
            pl.BlockSpec((None, 4, LANES, wide), grp4),
            pl.BlockSpec((None, 4, 1, LANES), grp4),
            pl.BlockSpec((None, 4, n_batch, LANES), grp4),
        ],
        out_specs=[
            pl.BlockSpec((None, rows, wide), grp),
            pl.BlockSpec((None, 4, n_batch, LANES), grp4),
        ],
        out_shape=[
            jax.ShapeDtypeStruct((S5_GROUPS, rows, wide), F32),
            jax.ShapeDtypeStruct((S5_GROUPS, 4, n_batch, LANES), F32),
        ],
        scratch_shapes=[pltpu.VMEM((4, rows, LANES), F32), pltpu.VMEM((4, rows, LANES), F32)],
        compiler_params=pltpu.CompilerParams(dimension_semantics=("arbitrary",)),
        name="s5",
    )(xg, kmat, bst, cst, lamt, h0)


def _s5_operators(a_re, a_im, log_step, b_re, b_im, c_re, c_im, d):
    hi = lax.Precision.HIGHEST
    n = S5_CHUNK
    wide = n * S5_CH
    ar, ai = a_re.astype(F32), a_im.astype(F32)
    step = jnp.exp(log_step.astype(F32))[..., None]
    zr, zi = ar * step, ai * step
    k = jnp.arange(n + 1, dtype=F32)[:, None, None, None]
    mag = jnp.exp(k * zr[None])
    pr, pi = mag * jnp.cos(k * zi[None]), mag * jnp.sin(k * zi[None])

    xr, xi = pr[1] - 1.0, pi[1]
    den = ar * ar + ai * ai
    fr, fi = ((xr * ar + xi * ai) / den)[:, :, None, :], ((xi * ar - xr * ai) / den)[:, :, None, :]
    br, bi = (jnp.swapaxes(b.astype(F32), 2, 3) for b in (b_re, b_im))
    bbr, bbi = fr * br - fi * bi, fr * bi + fi * br
    cr, ci = c_re.astype(F32), c_im.astype(F32)

    cpr = cr[None] * pr[:n, :, :, None, :] - ci[None] * pi[:n, :, :, None, :]
    cpi = cr[None] * pi[:n, :, :, None, :] + ci[None] * pr[:n, :, :, None, :]
    resp = (jnp.einsum('tdgcp,dgkp->tdgck', cpr, bbr, precision=hi)
            - jnp.einsum('tdgcp,dgkp->tdgck', cpi, bbi, precision=hi))
    centre = resp[0, 0] + resp[0, 1] + d.astype(F32)[:, :, None] * jnp.eye(S5_CH, dtype=F32)
    lags = jnp.concatenate([resp[:0:-1, 1], centre[None], resp[1:, 0]], axis=0)
    strip = lags.transpose(1, 3, 0, 2).reshape(S5_GROUPS, S5_CH, (2 * n - 1) * S5_CH)
    kmat = jnp.stack([strip[:, :, (n - 1 - s) * S5_CH:(n - 1 - s) * S5_CH + wide] for s in range(n)], axis=1)
    kmat = kmat.reshape(S5_GROUPS, wide, wide)

    def edge_states(powers, dirn):
        qr = jnp.swapaxes(pr[powers, dirn], 0, 1)[:, :, None, :]
        qi = jnp.swapaxes(pi[powers, dirn], 0, 1)[:, :, None, :]
        er = qr * bbr[dirn][:, None] - qi * bbi[dirn][:, None]
        ei = qr * bbi[dirn][:, None] + qi * bbr[dirn][:, None]
        return er.reshape(S5_GROUPS, wide, S5_STATE), ei.reshape(S5_GROUPS, wide, S5_STATE)

    tt = np.arange(n)
    bst = jnp.stack([*edge_states(n - 1 - tt, 0), *edge_states(tt, 1)], axis=1)

    def state_readout(powers, dirn):
        qr = pr[powers, dirn].transpose(1, 2, 0)[:, :, :, None]
        qi = pi[powers, dirn].transpose(1, 2, 0)[:, :, :, None]
        ctr = jnp.swapaxes(cr[dirn], 1, 2)[:, :, None, :]
        cti = jnp.swapaxes(ci[dirn], 1, 2)[:, :, None, :]
        rr = ctr * qr - cti * qi
        ri = -(ctr * qi + cti * qr)
        return rr.reshape(S5_GROUPS, S5_STATE, wide), ri.reshape(S5_GROUPS, S5_STATE, wide)

    cst = jnp.stack([*state_readout(tt + 1, 0), *state_readout(n - tt, 1)], axis=1)
    lamt = jnp.stack([pr[n, 0], pi[n, 0], pr[n, 1], pi[n, 1]], axis=1)
    zpad = LANES - S5_STATE
    bst = jnp.pad(bst, ((0, 0), (0, 0), (0, 0), (0, zpad)))
    cst = jnp.pad(cst, ((0, 0), (0, 0), (0, zpad), (0, 0)))
    lamt = jnp.pad(lamt, ((0, 0), (0, 0), (0, zpad)))
    return kmat.astype(BF16), bst.astype(BF16), cst.astype(BF16), lamt[:, :, None, :]


def _swap_halves(w, unit):
    n = w.shape[-1]
    idx = np.arange(n)
    half = unit // 2
    return w[..., np.where(idx % unit < half, idx + half, idx - half)]


def _rope_tables(n_pos):
    n = DA_QK // 4
    inv = ROPE_THETA ** (-jnp.arange(n, dtype=F32) / n)
    pos = jnp.arange(n_pos, dtype=jnp.int32)
    ang_r = (pos // GRID_W).astype(F32)[:, None] * inv[None, :]
    ang_c = (pos % GRID_W).astype(F32)[:, None] * inv[None, :]
    cos32 = jnp.concatenate([jnp.cos(ang_r)] * 2 + [jnp.cos(ang_c)] * 2, axis=1)
    sin32 = jnp.concatenate([-jnp.sin(ang_r), jnp.sin(ang_r), -jnp.sin(ang_c), jnp.sin(ang_c)], axis=1)
    pad = LANES - MLA_ROPE
    cosm = jnp.concatenate([cos32, jnp.ones((n_pos, pad), F32)], axis=1)
    sinm = jnp.concatenate([sin32, jnp.zeros((n_pos, pad), F32)], axis=1)
    return jnp.tile(cos32, (1, 4)), jnp.tile(sin32, (1, 4)), cosm, sinm


def _prep_ffn(w_in, w_out):
    wa = w_in[:, :D_FF].reshape(D_MODEL, N_FF_CHUNKS, FF_CHUNK).transpose(1, 0, 2).astype(BF16)
    wg = w_in[:, D_FF:].reshape(D_MODEL, N_FF_CHUNKS, FF_CHUNK).transpose(1, 0, 2).astype(BF16)
    wo = w_out.reshape(N_FF_CHUNKS, FF_CHUNK, D_MODEL).astype(BF16)
    return wa, wg, wo


def _prep_inproj(w, rope):
    o = np.cumsum([0, DA_WIDTH, DA_WIDTH, DA_WIDTH, MLA_Q_RANK, MLA_KV_RANK, MLA_ROPE, S5_WIDTH])
    wq, wk, wv, wcq, wckv, wkpe, wu = (w[:, o[i]:o[i + 1]] for i in range(7))
    kpe_pad = lambda m: jnp.pad(m, ((0, 0), (0, LANES - MLA_ROPE)))
    cols = [wq, wk, wv, wcq, wckv, wu, kpe_pad(wkpe)]
    if rope:
        half = DA_QK // 2
        cols += [_swap_halves(wq, half), _swap_halves(wk, half), kpe_pad(_swap_halves(wkpe, half))]
    return jnp.concatenate(cols, axis=1).astype(BF16)


def _prep_mla(q_norm_w, w_q_up, kv_norm_w, w_kv_up):
    pad = LANES - MLA_ROPE - MLA_NOPE
    wq = w_q_up.reshape(MLA_Q_RANK, MLA_HEADS, MLA_NOPE + MLA_ROPE)
    nope, rp = wq[..., :MLA_NOPE], wq[..., MLA_NOPE:]
    zeros = lambda n: jnp.zeros((MLA_Q_RANK, MLA_HEADS, n), F32)
    wqup = jnp.concatenate([rp, nope, zeros(pad)], axis=-1).reshape(MLA_Q_RANK, MLA_HEADS * LANES)
    wqsw = jnp.concatenate([_swap_halves(rp, DA_QK // 2), zeros(MLA_NOPE + pad)], axis=-1)
    wqsw = wqsw.reshape(MLA_Q_RANK, MLA_HEADS * LANES)
    wkv = w_kv_up.reshape(MLA_KV_RANK, MLA_HEADS, MLA_NOPE + MLA_V)
    zk = lambda n: jnp.zeros((MLA_KV_RANK, MLA_HEADS, n), F32)
    wk = jnp.concatenate([zk(MLA_ROPE), wkv[..., :MLA_NOPE], zk(pad)], axis=-1)
    wk = wk.reshape(MLA_KV_RANK, MLA_HEADS * LANES)
    wv = wkv[..., MLA_NOPE:].reshape(MLA_KV_RANK, MLA_WIDTH)
    return (q_norm_w.reshape(1, MLA_Q_RANK), kv_norm_w.reshape(1, MLA_KV_RANK),
            wqup.astype(BF16), wqsw.astype(BF16), wk.astype(BF16), wv.astype(BF16))


def _prep_mix(w_glu, b_glu, w_out):
    return (w_glu.astype(BF16), b_glu.reshape(1, S5_WIDTH),
            w_out[:DA_WIDTH].reshape(N_SLABS, LANES, D_MODEL).astype(BF16),
            w_out[DA_WIDTH:DA_WIDTH + MLA_WIDTH].reshape(N_SLABS, LANES, D_MODEL).astype(BF16),
            w_out[DA_WIDTH + MLA_WIDTH:].astype(BF16))


def _slabs_t(a, n_batch):
    b, n, _ = a.shape
    return a.reshape(b * n, N_SLABS, LANES).transpose(1, 2, 0).astype(BF16)


def _slabs(a):
    b, n, _ = a.shape
    return a.reshape(b * n, N_SLABS, LANES).transpose(1, 0, 2).astype(BF16)


def kernel(x_prompt, x_sample, cache_diff_k, cache_diff_v, cache_mla_ckv, cache_mla_kpe, state_s5_re, state_s5_im, c, c_ctx, w_ada, b_ada, norm_w, ffn_w_in, ffn_w_out, w_in, diff_lambda, diff_subln_w, mla_q_norm_w, mla_w_q_up, mla_kv_norm_w, mla_w_kv_up, s5_a_re, s5_a_im, s5_log_step, s5_b_re, s5_b_im, s5_c_re, s5_c_im, s5_d, s5_w_glu, s5_b_glu, w_out, final_norm_w):
    n_ctx, seq, _ = x_prompt.shape
    n_dec, dec_seq, _ = x_sample.shape
    past = cache_diff_k.shape[2]

    cvec = jnp.concatenate(
        [c_ctx[None, :], c, jnp.zeros((MOD_ROWS - 1 - n_dec, D_MODEL), F32)], axis=0)
    mod = _adaln(cvec, w_ada, b_ada)
    rope_tabs = _rope_tables(dec_seq)

    groups = {
        'ctx': dict(x=x_prompt.reshape(n_ctx * seq, D_MODEL), seg=n_ctx * seq, row=0, nb=n_ctx),
        'dec': dict(x=x_sample.reshape(n_dec * dec_seq, D_MODEL), seg=dec_seq, row=1, nb=n_dec),
    }
    ctx_out = []
    for l in range(DEPTH):
        lam_init = 0.8 - 0.6 * math.exp(-0.3 * l)
        ffn1 = _prep_ffn(ffn_w_in[l, 0], ffn_w_out[l, 0])
        ffn2 = _prep_ffn(ffn_w_in[l, 1], ffn_w_out[l, 1])
        mla_w = _prep_mla(mla_q_norm_w[l], mla_w_q_up[l], mla_kv_norm_w[l], mla_w_kv_up[l])
        mix_w = _prep_mix(s5_w_glu[l], s5_b_glu[l], w_out[l])
        s5_ops = _s5_operators(s5_a_re[l], s5_a_im[l], s5_log_step[l], s5_b_re[l], s5_b_im[l],
                               s5_c_re[l], s5_c_im[l], s5_d[l])
        subw = jnp.tile(diff_subln_w[l], 2).reshape(1, LANES)
        final_w = final_norm_w if l == DEPTH - 1 else None
        for name, g in groups.items():
            dec = name == 'dec'
            x = _ffn_call(g['x'], mod[l], g['seg'], g['row'], norm_w[l, 0], ffn1, 0)
            proj = _inproj_call(x, mod[l], g['seg'], g['row'], norm_w[l, 1], _prep_inproj(w_in[l], dec),
                                mla_w, rope_tabs if dec else None)
            q, kt, v, qm, ktm, vm, xg = proj[:7]
            n_keys = g['x'].shape[0] // g['nb']
            parts = [(kt, v, ktm, vm, n_keys)]
            h0 = jnp.zeros((S5_GROUPS, 4, g['nb'], LANES), F32)
            if dec:
                ck = cache_diff_k[:, l].reshape(n_dec, past, DA_WIDTH)
                cv = cache_diff_v[:, l].reshape(n_dec, past, DA_WIDTH)
                kpe128 = jnp.pad(cache_mla_kpe[:, l].reshape(n_dec * past, MLA_ROPE),
                                 ((0, 0), (0, LANES - MLA_ROPE)))
                cktm, cvm = _mla_expand_call(cache_mla_ckv[:, l].reshape(n_dec * past, MLA_KV_RANK),
                                             kpe128, mla_w[4], mla_w[5], past)
                parts.append((_slabs_t(ck, n_dec), _slabs(cv), cktm, cvm, past))
                sr, si = state_s5_re[:, l], state_s5_im[:, l]
                h0 = jnp.stack([sr[:, 0], si[:, 0], sr[:, 1], si[:, 1]], 0).transpose(2, 0, 1, 3)
                h0 = jnp.pad(h0, ((0, 0), (0, 0), (0, 0), (0, LANES - S5_STATE)))
            o_a, o_b = _attn_call(q, qm, diff_lambda[l], subw, parts, g['nb'], lam_init)
            y_s5, fin = _s5_call(xg, *s5_ops, h0, g['nb'])
            g['x'] = _ffn_call(x, mod[l], g['seg'], g['row'], norm_w[l, 2], ffn2, 6,
                               mix_args=(o_a, o_b, y_s5), mix_w=mix_w, final_w=final_w)
            if not dec:
                ka, va, ckvn, kpe = proj[7:]
                fin = fin[..., :S5_STATE].transpose(1, 2, 0, 3)
                ctx_out.append((
                    ka.reshape(n_ctx, seq, DA_HEADS, 2 * DA_QK),
                    va.reshape(n_ctx, seq, DA_HEADS, DA_V),
                    ckvn.reshape(n_ctx, seq, MLA_KV_RANK),
                    kpe.reshape(n_ctx, seq, MLA_ROPE),
                    jnp.stack([fin[0], fin[2]], axis=1),
                    jnp.stack([fin[1], fin[3]], axis=1),
                ))

    y_prompt = groups['ctx']['x'].reshape(n_ctx, seq, D_MODEL)
    y_sample = groups['dec']['x'].reshape(n_dec, dec_seq, D_MODEL)
    new = [jnp.stack([t[i] for t in ctx_out], axis=1) for i in range(6)]
    return (y_prompt, y_sample, *new)
```

```python
import functools
import math

import jax
import jax.numpy as jnp
import numpy as np
from jax import lax
from jax.experimental import pallas as pl
from jax.experimental.pallas import tpu as pltpu

F32 = jnp.float32
BF16 = jnp.bfloat16

D_MODEL = 1024
DEPTH = 2
GRID_W = 64
ROPE_THETA = 10000.0
EPS = 1e-6
N_MOD = 9
D_FF = 2816
DA_HEADS = 6
DA_QK = 32
DA_V = 64
DA_WIDTH = DA_HEADS * DA_V
MLA_HEADS = 6
MLA_Q_RANK = 256
MLA_KV_RANK = 128
MLA_NOPE = 64
MLA_ROPE = 32
MLA_V = 64
MLA_WIDTH = MLA_HEADS * MLA_V
S5_WIDTH = D_MODEL - DA_WIDTH - MLA_WIDTH
S5_CH = 16
S5_GROUPS = S5_WIDTH // S5_CH
S5_STATE = 64

LANES = 128
FF_CHUNK = 256
N_FF_CHUNKS = D_FF // FF_CHUNK
TOKEN_TILE = 512
Q_TILE = 256
S5_CHUNK = 16
N_SLABS = DA_WIDTH // LANES
MOD_ROWS = 8
VMEM_LIMIT = 56 * 1024 * 1024


def _rms(x, w):
    return x * lax.rsqrt(jnp.mean(x * x, axis=-1, keepdims=True) + EPS) * w


def _silu(x):
    return x * jax.nn.sigmoid(x)


def _gelu_tanh(x):
    c = math.sqrt(2.0 / math.pi)
    return x * (0.5 * (1.0 + jnp.tanh(c * (x + 0.044715 * (x * x * x)))))


def _dot(a, b):
    return jnp.dot(a, b, preferred_element_type=F32)


def _resident(shape):
    nd = len(shape)
    return pl.BlockSpec(shape, lambda *_: (0,) * nd, pipeline_mode=pl.Buffered(1))


def _adaln_kernel(c_ref, w_ref, b_ref, o_ref):
    s = _silu(c_ref[...])
    o_ref[...] = jnp.dot(s, w_ref[...], preferred_element_type=F32,
                         precision=lax.Precision.HIGHEST) + b_ref[...]


def _adaln(cvec, w_ada, b_ada):
    n = N_MOD * D_MODEL
    tn = 1152
    out = pl.pallas_call(
        _adaln_kernel,
        grid=(DEPTH, n // tn),
        in_specs=[
            pl.BlockSpec((MOD_ROWS, D_MODEL), lambda l, j: (0, 0)),
            pl.BlockSpec((None, D_MODEL, tn), lambda l, j: (l, 0, j)),
            pl.BlockSpec((None, 1, tn), lambda l, j: (l, 0, j)),
        ],
        out_specs=pl.BlockSpec((None, MOD_ROWS, tn), lambda l, j: (l, 0, j)),
        out_shape=jax.ShapeDtypeStruct((DEPTH, MOD_ROWS, n), F32),
        compiler_params=pltpu.CompilerParams(
            dimension_semantics=("arbitrary", "arbitrary"), vmem_limit_bytes=VMEM_LIMIT),
        name="adaln",
    )(cvec, w_ada, b_ada.reshape(DEPTH, 1, n))
    return out.reshape(DEPTH, MOD_ROWS, N_MOD, D_MODEL)


_ROWS_PER_TILE = TOKEN_TILE // S5_CHUNK
_SLOTS = LANES // S5_CH


def _to_group_major(u_scr, xg_ref):
    slot = lax.broadcasted_iota(jnp.int32, (_ROWS_PER_TILE, LANES), 1) // S5_CH
    for g in range(S5_GROUPS):
        for tt in range(S5_CHUNK // _SLOTS):
            acc = None
            for j in range(_SLOTS):
                t = tt * _SLOTS + j
                src = u_scr[g // _SLOTS, pl.ds(t, _ROWS_PER_TILE, stride=S5_CHUNK), :]
                shift = ((j - g % _SLOTS) % _SLOTS) * S5_CH
                piece = pltpu.roll(src, shift, 1) if shift else src
                acc = piece if acc is None else jnp.where(slot == j, piece, acc)
            xg_ref[g, :, tt * LANES:(tt + 1) * LANES] = acc.astype(BF16)


def _from_group_major(yg_ref, y_scr):
    slot = lax.broadcasted_iota(jnp.int32, (_ROWS_PER_TILE, LANES), 1) // S5_CH
    for t in range(S5_CHUNK):
        for gc in range(S5_GROUPS // _SLOTS):
            acc = None
            for m in range(_SLOTS):
                src = yg_ref[gc * _SLOTS + m, :, (t // _SLOTS) * LANES:(t // _SLOTS + 1) * LANES]
                shift = ((m - t % _SLOTS) % _SLOTS) * S5_CH
                piece = pltpu.roll(src, shift, 1) if shift else src
                acc = piece if acc is None else jnp.where(slot == m, piece, acc)
            y_scr[gc, pl.ds(t, _ROWS_PER_TILE, stride=S5_CHUNK), :] = acc


def _ffn_kernel(mix, final, mod_base, *refs):
    it = iter(refs)
    x_ref, mod_ref, nw_ref, wi_ref, wo_ref = (next(it) for _ in range(5))
    if mix:
        oa_ref, ob_ref, ys_ref, wglu_ref, bglu_ref, woa_ref, wob_ref, woc_ref = (next(it) for _ in range(8))
    if final:
        fnw_ref = next(it)
    out_ref, h_scr, acc_scr = next(it), next(it), next(it)

    x = x_ref[...]
    if mix:
        y_scr = next(it)
        _from_group_major(ys_ref, y_scr)
        yc = _gelu_tanh(jnp.concatenate([y_scr[gc] for gc in range(S5_WIDTH // LANES)], axis=1))
        oc = yc * jax.nn.sigmoid(_dot(yc.astype(BF16), wglu_ref[...]) + bglu_ref[...])
        m = _dot(oc.astype(BF16), woc_ref[...])
        for s in range(N_SLABS):
            m += _dot(oa_ref[s], woa_ref[s]) + _dot(ob_ref[s], wob_ref[s])
        x = x + mod_ref[0, mod_base - 1:mod_base, :] * m

    shift = mod_ref[0, mod_base:mod_base + 1, :]
    scale = mod_ref[0, mod_base + 1:mod_base + 2, :]
    gate = mod_ref[0, mod_base + 2:mod_base + 3, :]
    h = _rms(x, nw_ref[...]) * (1.0 + scale) + shift
    h_scr[...] = h.astype(BF16)
    acc_scr[...] = jnp.zeros_like(acc_scr)

    def chunk(j, carry):
        hb = h_scr[...]
        col = pl.multiple_of(j * FF_CHUNK, FF_CHUNK)
        a = _dot(hb, wi_ref[:, pl.ds(col, FF_CHUNK)])
        g = _dot(hb, wi_ref[:, pl.ds(D_FF + col, FF_CHUNK)])
        acc_scr[...] += _dot((_silu(g) * a).astype(BF16), wo_ref[pl.ds(col, FF_CHUNK), :])
        return carry

    lax.fori_loop(0, N_FF_CHUNKS, chunk, 0)
    y = x + 0.5 * gate * acc_scr[...]
    if final:
        y = _rms(y, fnw_ref[...])
    out_ref[...] = y


def _ffn_call(x, mod_l, seg_len, row_base, nw, ffn_w, mod_base, mix_args=None, mix_w=None, final_w=None):
    t = x.shape[0]
    tm = TOKEN_TILE
    mix = mix_args is not None
    final = final_w is not None
    tok = lambda i: (i, 0)
    in_specs = [
        pl.BlockSpec((tm, D_MODEL), tok),
        pl.BlockSpec((1, N_MOD, D_MODEL), lambda i: (row_base + (i * tm) // seg_len, 0, 0)),
        _resident((1, D_MODEL)),
        _resident((D_MODEL, 2 * D_FF)),
        _resident((D_FF, D_MODEL)),
    ]
    args = [x, mod_l, nw.reshape(1, D_MODEL), *ffn_w]
    if mix:
        in_specs += [
            pl.BlockSpec((N_SLABS, tm, LANES), lambda i: (0, i, 0)),
            pl.BlockSpec((N_SLABS, tm, LANES), lambda i: (0, i, 0)),
            pl.BlockSpec((S5_GROUPS, _ROWS_PER_TILE, S5_CHUNK * S5_CH), lambda i: (0, i, 0)),
            _resident((S5_WIDTH, S5_WIDTH)),
            _resident((1, S5_WIDTH)),
            _resident((N_SLABS, LANES, D_MODEL)),
            _resident((N_SLABS, LANES, D_MODEL)),
            _resident((S5_WIDTH, D_MODEL)),
        ]
        args += [*mix_args, *mix_w]
    if final:
        in_specs.append(_resident((1, D_MODEL)))
        args.append(final_w.reshape(1, D_MODEL))
    return pl.pallas_call(
        functools.partial(_ffn_kernel, mix, final, mod_base),
        grid=(t // tm,),
        in_specs=in_specs,
        out_specs=pl.BlockSpec((tm, D_MODEL), tok),
        out_shape=jax.ShapeDtypeStruct((t, D_MODEL), F32),
        scratch_shapes=[pltpu.VMEM((tm, D_MODEL), BF16), pltpu.VMEM((tm, D_MODEL), F32)]
        + ([pltpu.VMEM((S5_WIDTH // LANES, tm, LANES), F32)] if mix else []),
        compiler_params=pltpu.CompilerParams(
            dimension_semantics=("arbitrary",), vmem_limit_bytes=VMEM_LIMIT),
        name="mix_ffn" if mix else "ffn",
    )(*args)


_C_Q, _C_K, _C_V = 0, DA_WIDTH, 2 * DA_WIDTH
_C_CQ = 3 * DA_WIDTH
_C_CKV = _C_CQ + MLA_Q_RANK
_C_U = _C_CKV + MLA_KV_RANK
_C_KPE = _C_U + S5_WIDTH
_C_QSW = _C_KPE + LANES
_C_KSW = _C_QSW + DA_WIDTH
_C_KPESW = _C_KSW + DA_WIDTH
_NCOL_CTX = _C_QSW
_NCOL_DEC = _C_KPESW + LANES

_QSCALE_DIFF = DA_QK ** -0.5 * math.log2(math.e)
_QSCALE_MLA = (MLA_NOPE + MLA_ROPE) ** -0.5 * math.log2(math.e)


def _inproj_kernel(rope, n_aliased, *refs):
    it = iter(refs)
    x_ref, mod_ref, nw_ref, w_ref, qnw_ref, kvnw_ref, wqup_ref = (next(it) for _ in range(7))
    if rope:
        wqsw_ref = next(it)
    wk_ref, wv_ref = next(it), next(it)
    if rope:
        cos_ref, sin_ref, cosm_ref, sinm_ref = (next(it) for _ in range(4))
    for _ in range(n_aliased):
        next(it)
    q_ref, kt_ref, v_ref, qm_ref, ktm_ref, vm_ref, xg_ref = (next(it) for _ in range(7))
    if not rope:
        ka_ref, va_ref, ckv_ref, kpe_ref = (next(it) for _ in range(4))
    u_scr = next(it)

    h = _rms(x_ref[...], nw_ref[...]) * (1.0 + mod_ref[0, 4:5, :]) + mod_ref[0, 3:4, :]
    big = _dot(h.astype(BF16), w_ref[...])

    def cols(start, width):
        return big[:, start:start + width]

    if rope:
        cos, sin, cosm, sinm = cos_ref[...], sin_ref[...], cosm_ref[...], sinm_ref[...]
    for s in range(N_SLABS):
        qs = cols(_C_Q + s * LANES, LANES)
        ks = cols(_C_K + s * LANES, LANES)
        if rope:
            qs = qs * cos + cols(_C_QSW + s * LANES, LANES) * sin
            ks = ks * cos + cols(_C_KSW + s * LANES, LANES) * sin
        q_ref[s] = (qs * _QSCALE_DIFF).astype(BF16)
        kt_ref[s] = ks.T.astype(BF16)
        v_ref[s] = cols(_C_V + s * LANES, LANES).astype(BF16)
    for gc in range(S5_WIDTH // LANES):
        u_scr[gc] = cols(_C_U + gc * LANES, LANES)
    _to_group_major(u_scr, xg_ref)

    cqn = _rms(cols(_C_CQ, MLA_Q_RANK), qnw_ref[...]).astype(BF16)
    qm = _dot(cqn, wqup_ref[...])
    if rope:
        qsw = _dot(cqn, wqsw_ref[...])
    ckvn = _rms(cols(_C_CKV, MLA_KV_RANK), kvnw_ref[...])
    ckvb = ckvn.astype(BF16)
    kn = _dot(ckvb, wk_ref[...])
    vm = _dot(ckvb, wv_ref[...])
    kpe = cols(_C_KPE, LANES)
    if rope:
        kpe = kpe * cosm + cols(_C_KPESW, LANES) * sinm
    for hd in range(MLA_HEADS):
        qh = qm[:, hd * LANES:(hd + 1) * LANES]
        if rope:
            qh = qh * cosm + qsw[:, hd * LANES:(hd + 1) * LANES] * sinm
        qm_ref[hd] = (qh * _QSCALE_MLA).astype(BF16)
        ktm_ref[hd] = (kn[:, hd * LANES:(hd + 1) * LANES] + kpe).T.astype(BF16)
    for s in range(N_SLABS):
        vm_ref[s] = vm[:, s * LANES:(s + 1) * LANES].astype(BF16)

    if not rope:
        n_seq, seq = ka_ref.shape[:2]
        for b in range(n_seq):
            rows = slice(b * seq, (b + 1) * seq)
            ka_ref[b] = cols(_C_K, DA_WIDTH)[rows]
            va_ref[b] = cols(_C_V, DA_WIDTH)[rows]
            ckv_ref[b] = ckvn[rows]
            kpe_ref[b] = cols(_C_KPE, MLA_ROPE)[rows]


def _inproj_call(x, mod_l, seg_len, row_base, nw, w, mla_w, rope_tabs, cache_out=None):
    t = x.shape[0]
    tm = TOKEN_TILE
    rope = rope_tabs is not None
    ncol = _NCOL_DEC if rope else _NCOL_CTX
    qnw, kvnw, wqup, wqsw, wk, wv = mla_w
    tok = lambda i: (i, 0)
    slab = lambda i: (0, i, 0)
    slab_t = lambda i: (0, 0, i)
    in_specs = [
        pl.BlockSpec((tm, D_MODEL), tok),
        pl.BlockSpec((1, N_MOD, D_MODEL), lambda i: (row_base + (i * tm) // seg_len, 0, 0)),
        _resident((1, D_MODEL)),
        _resident((D_MODEL, ncol)),
        _resident((1, MLA_Q_RANK)),
        _resident((1, MLA_KV_RANK)),
        _resident((MLA_Q_RANK, MLA_HEADS * LANES)),
    ]
    args = [x, mod_l, nw.reshape(1, D_MODEL), w, qnw, kvnw, wqup]
    if rope:
        in_specs.append(_resident((MLA_Q_RANK, MLA_HEADS * LANES)))
        args.append(wqsw)
    in_specs += [_resident((MLA_KV_RANK, MLA_HEADS * LANES)), _resident((MLA_KV_RANK, MLA_WIDTH))]
    args += [wk, wv]
    if rope:
        n_pos = rope_tabs[0].shape[0]
        pos = lambda i: (i % (n_pos // tm), 0)
        in_specs += [pl.BlockSpec((tm, LANES), pos)] * 4
        args += list(rope_tabs)
    out_shape = [
        jax.ShapeDtypeStruct((N_SLABS, t, LANES), BF16),
        jax.ShapeDtypeStruct((N_SLABS, LANES, t), BF16),
        jax.ShapeDtypeStruct((N_SLABS, t, LANES), BF16),
        jax.ShapeDtypeStruct((MLA_HEADS, t, LANES), BF16),
        jax.ShapeDtypeStruct((MLA_HEADS, LANES, t), BF16),
        jax.ShapeDtypeStruct((N_SLABS, t, LANES), BF16),
        jax.ShapeDtypeStruct((S5_GROUPS, t // S5_CHUNK, S5_CHUNK * S5_CH), BF16),
    ]
    out_specs = [
        pl.BlockSpec((N_SLABS, tm, LANES), slab),
        pl.BlockSpec((N_SLABS, LANES, tm), slab_t),
        pl.BlockSpec((N_SLABS, tm, LANES), slab),
        pl.BlockSpec((MLA_HEADS, tm, LANES), slab),
        pl.BlockSpec((MLA_HEADS, LANES, tm), slab_t),
        pl.BlockSpec((N_SLABS, tm, LANES), slab),
        pl.BlockSpec((S5_GROUPS, _ROWS_PER_TILE, S5_CHUNK * S5_CH), slab),
    ]
    aliases = {}
    if not rope:
        layer, seq, prev = cache_out
        n_first = len(out_shape)
        for width in (DA_WIDTH, DA_WIDTH, MLA_KV_RANK, MLA_ROPE):
            out_shape.append(jax.ShapeDtypeStruct((t // seq, DEPTH, seq, width), F32))
            out_specs.append(pl.BlockSpec((tm // seq, None, seq, width), lambda i: (i, layer, 0, 0)))
        if prev is not None:
            for k, arr in enumerate(prev):
                aliases[len(args)] = n_first + k
                in_specs.append(pl.BlockSpec(memory_space=pl.ANY))
                args.append(arr)
    return pl.pallas_call(
        functools.partial(_inproj_kernel, rope, len(aliases)),
        grid=(t // tm,),
        in_specs=in_specs,
        out_specs=out_specs,
        out_shape=out_shape,
        input_output_aliases=aliases,
        scratch_shapes=[pltpu.VMEM((S5_WIDTH // LANES, tm, LANES), F32)],
        compiler_params=pltpu.CompilerParams(
            dimension_semantics=("arbitrary",), vmem_limit_bytes=VMEM_LIMIT),
        name="inproj_dec" if rope else "inproj_ctx",
    )(*args)


def _mla_expand_kernel(ckv_ref, kpe_ref, wk_ref, wv_ref, ktm_ref, vm_ref):
    ckvb = ckv_ref[...].astype(BF16)
    kn = _dot(ckvb, wk_ref[...])
    vm = _dot(ckvb, wv_ref[...])
    kpe = kpe_ref[...]
    for hd in range(MLA_HEADS):
        ktm_ref[hd] = (kn[:, hd * LANES:(hd + 1) * LANES] + kpe).T.astype(BF16)
    for s in range(N_SLABS):
        vm_ref[s] = vm[:, s * LANES:(s + 1) * LANES].astype(BF16)


def _mla_expand_call(ckv, kpe128, wk, wv, rows):
    t = ckv.shape[0]
    return pl.pallas_call(
        _mla_expand_kernel,
        grid=(t // rows,),
        in_specs=[
            pl.BlockSpec((rows, MLA_KV_RANK), lambda i: (i, 0)),
            pl.BlockSpec((rows, LANES), lambda i: (i, 0)),
            _resident((MLA_KV_RANK, MLA_HEADS * LANES)),
            _resident((MLA_KV_RANK, MLA_WIDTH)),
        ],
        out_specs=[
            pl.BlockSpec((MLA_HEADS, LANES, rows), lambda i: (0, 0, i)),
            pl.BlockSpec((N_SLABS, rows, LANES), lambda i: (0, i, 0)),
        ],
        out_shape=[
            jax.ShapeDtypeStruct((MLA_HEADS, LANES, t), BF16),
            jax.ShapeDtypeStruct((N_SLABS, t, LANES), BF16),
        ],
        compiler_params=pltpu.CompilerParams(dimension_semantics=("arbitrary",)),
        name="mla_expand",
    )(ckv, kpe128, wk, wv)


def _exp2_parts(parts):
    m = functools.reduce(jnp.maximum, [jnp.max(s, axis=1, keepdims=True) for s in parts])
    es = [jnp.exp2(s - m) for s in parts]
    tot = functools.reduce(jnp.add, [jnp.sum(e, axis=1, keepdims=True) for e in es])
    return es, tot


def _attn_kernel(n_parts, lam_init, *refs):
    q_ref, qm_ref, lamp_ref, subw_ref = refs[:4]
    parts = [refs[4 + 4 * i:8 + 4 * i] for i in range(n_parts)]
    oa_ref, ob_ref = refs[4 + 4 * n_parts:]

    lp = lamp_ref[...]
    lam = (jnp.exp(jnp.sum(lp[0:1] * lp[1:2], keepdims=True))
           - jnp.exp(jnp.sum(lp[2:3] * lp[3:4], keepdims=True)) + lam_init)
    tq = q_ref.shape[1]
    lane = lax.broadcasted_iota(jnp.int32, (tq, LANES), 1)
    low_half = lane < DA_V
    subw = subw_ref[...] * (1.0 - lam_init)

    def diff_slab(s, carry):
        qs = q_ref[s]
        heads = []
        for half in range(2):
            comps = []
            for c in range(2):
                sub = 2 * half + c
                qsel = jnp.where(lane // DA_QK == sub, qs, jnp.zeros_like(qs))
                comps.append(_exp2_parts([_dot(qsel, kt[s]) for kt, _, _, _ in parts]))
            (e0, l0), (e1, l1) = comps
            r = lam * l0 / l1
            o = functools.reduce(jnp.add, [_dot((a - r * b).astype(BF16), v[s])
                                           for a, b, (_, v, _, _) in zip(e0, e1, parts)])
            heads.append(o * (1.0 / l0))
        o = jnp.where(low_half, heads[0], heads[1])
        o2 = o * o
        ss_lo = jnp.sum(jnp.where(low_half, o2, 0.0), axis=1, keepdims=True)
        ss_hi = jnp.sum(jnp.where(low_half, 0.0, o2), axis=1, keepdims=True)
        ms = jnp.where(low_half, ss_lo, ss_hi) * (1.0 / DA_V)
        oa_ref[s] = (o * lax.rsqrt(ms + EPS) * subw).astype(BF16)
        return carry

    def mla_slab(s, carry):
        heads = []
        for half in range(2):
            hd = 2 * s + half
            qh = qm_ref[hd]
            es, tot = _exp2_parts([_dot(qh, ktm[hd]) for _, _, ktm, _ in parts])
            o = functools.reduce(
                jnp.add, [_dot(e.astype(BF16), vm[s]) for e, (_, _, _, vm) in zip(es, parts)])
            heads.append(o * (1.0 / tot))
        ob_ref[s] = jnp.where(low_half, heads[0], heads[1]).astype(BF16)
        return carry

    lax.fori_loop(0, N_SLABS, lambda s, carry: mla_slab(s, diff_slab(s, carry)), 0)


def _attn_call(q, qm, lamp, subw, parts, n_batch, lam_init):
    t = q.shape[1]
    seq = t // n_batch
    tq = min(Q_TILE, seq)
    nq = seq // tq
    qmap = lambda b, i: (0, b * nq + i, 0)
    in_specs = [
        pl.BlockSpec((N_SLABS, tq, LANES), qmap),
        pl.BlockSpec((MLA_HEADS, tq, LANES), qmap),
        pl.BlockSpec((4, DA_QK), lambda b, i: (0, 0)),
        pl.BlockSpec((1, LANES), lambda b, i: (0, 0)),
    ]
    args = [q, qm, lamp, subw]
    for kt, v, ktm, vm, n_keys in parts:
        in_specs += [
            pl.BlockSpec((N_SLABS, LANES, n_keys), lambda b, i: (0, 0, b)),
            pl.BlockSpec((N_SLABS, n_keys, LANES), lambda b, i: (0, b, 0)),
            pl.BlockSpec((MLA_HEADS, LANES, n_keys), lambda b, i: (0, 0, b)),
            pl.BlockSpec((N_SLABS, n_keys, LANES), lambda b, i: (0, b, 0)),
        ]
        args += [kt, v, ktm, vm]
    return pl.pallas_call(
        functools.partial(_attn_kernel, len(parts), lam_init),
        grid=(n_batch, nq),
        in_specs=in_specs,
        out_specs=[pl.BlockSpec((N_SLABS, tq, LANES), qmap)] * 2,
        out_shape=[jax.ShapeDtypeStruct((N_SLABS, t, LANES), BF16)] * 2,
        compiler_params=pltpu.CompilerParams(
            dimension_semantics=("arbitrary", "arbitrary"), vmem_limit_bytes=VMEM_LIMIT),
        name="attn_dec" if len(parts) > 1 else "attn_ctx",
    )(*args)


def _s5_kernel(n_batch, n_chunks, x_ref, kmat_ref, bst_ref, cst_ref, lamt_ref, h0_ref, y_ref, fin_ref,
               s_scr, hin_scr):
    x = x_ref[...]
    y = _dot(x, kmat_ref[...])
    for k in range(4):
        s_scr[k] = _dot(x, bst_ref[k])

    a_fr, a_fi, a_br, a_bi = (lamt_ref[k] for k in range(4))

    def chunk_rows(j):
        return pl.ds(j, n_batch, stride=n_chunks)

    def step(j, carry):
        fr, fi, br, bi = carry
        rf, rb = chunk_rows(j), chunk_rows(n_chunks - 1 - j)
        hin_scr[0, rf, :] = fr
        hin_scr[1, rf, :] = fi
        hin_scr[2, rb, :] = br
        hin_scr[3, rb, :] = bi
        nfr = a_fr * fr - a_fi * fi + s_scr[0, rf, :]
        nfi = a_fr * fi + a_fi * fr + s_scr[1, rf, :]
        nbr = a_br * br - a_bi * bi + s_scr[2, rb, :]
        nbi = a_br * bi + a_bi * br + s_scr[3, rb, :]
        return nfr, nfi, nbr, nbi

    fin = lax.fori_loop(0, n_chunks, step, tuple(h0_ref[k] for k in range(4)))
    for k in range(4):
        fin_ref[k] = fin[k]
        y += _dot(hin_scr[k].astype(BF16), cst_ref[k])
    y_ref[...] = y


def _s5_call(xg, kmat, bst, cst, lamt, h0, n_batch):
    rows = xg.shape[1]
    n_chunks = rows // n_batch
    wide = S5_CHUNK * S5_CH
    grp = lambda g: (g, 0, 0)
    grp4 = lambda g: (g, 0, 0, 0)
    return pl.pallas_call(
        functools.partial(_s5_kernel, n_batch, n_chunks),
        grid=(S5_GROUPS,),
        in_specs=[
            pl.BlockSpec((None, rows, wide), grp),
            pl.BlockSpec((None, wide, wide), grp),
            pl.BlockSpec((None, 4, wide, LANES), grp4),
            pl.BlockSpec((None, 4, LANES, wide), grp4),
            pl.BlockSpec((None, 4, 1, LANES), grp4),
            pl.BlockSpec((None, 4, n_batch, LANES), grp4),
        ],
        out_specs=[
            pl.BlockSpec((None, rows, wide), grp),
            pl.BlockSpec((None, 4, n_batch, LANES), grp4),
        ],
        out_shape=[
            jax.ShapeDtypeStruct((S5_GROUPS, rows, wide), F32),
            jax.ShapeDtypeStruct((S5_GROUPS, 4, n_batch, LANES), F32),
        ],
        scratch_shapes=[pltpu.VMEM((4, rows, LANES), F32), pltpu.VMEM((4, rows, LANES), F32)],
        compiler_params=pltpu.CompilerParams(dimension_semantics=("arbitrary",)),
        name="s5",
    )(xg, kmat, bst, cst, lamt, h0)


def _s5_operators(a_re, a_im, log_step, b_re, b_im, c_re, c_im, d):
    hi = lax.Precision.HIGHEST
    n = S5_CHUNK
    wide = n * S5_CH
    ar, ai = a_re.astype(F32), a_im.astype(F32)
    step = jnp.exp(log_step.astype(F32))[..., None]
    zr, zi = ar * step, ai * step
    k = jnp.arange(n + 1, dtype=F32)[:, None, None, None]
    mag = jnp.exp(k * zr[None])
    pr, pi = mag * jnp.cos(k * zi[None]), mag * jnp.sin(k * zi[None])

    xr, xi = pr[1] - 1.0, pi[1]
    den = ar * ar + ai * ai
    fr, fi = ((xr * ar + xi * ai) / den)[:, :, None, :], ((xi * ar - xr * ai) / den)[:, :, None, :]
    br, bi = (jnp.swapaxes(b.astype(F32), 2, 3) for b in (b_re, b_im))
    bbr, bbi = fr * br - fi * bi, fr * bi + fi * br
    cr, ci = c_re.astype(F32), c_im.astype(F32)

    cpr = cr[None] * pr[:n, :, :, None, :] - ci[None] * pi[:n, :, :, None, :]
    cpi = cr[None] * pi[:n, :, :, None, :] + ci[None] * pr[:n, :, :, None, :]
    resp = (jnp.einsum('tdgcp,dgkp->tdgck', cpr, bbr, precision=hi)
            - jnp.einsum('tdgcp,dgkp->tdgck', cpi, bbi, precision=hi))
    centre = resp[0, 0] + resp[0, 1] + d.astype(F32)[:, :, None] * jnp.eye(S5_CH, dtype=F32)
    lags = jnp.concatenate([resp[:0:-1, 1], centre[None], resp[1:, 0]], axis=0)
    strip = lags.transpose(1, 3, 0, 2).reshape(S5_GROUPS, S5_CH, (2 * n - 1) * S5_CH)
    kmat = jnp.stack([strip[:, :, (n - 1 - s) * S5_CH:(n - 1 - s) * S5_CH + wide] for s in range(n)], axis=1)
    kmat = kmat.reshape(S5_GROUPS, wide, wide)

    def edge_states(powers, dirn):
        qr = jnp.swapaxes(pr[powers, dirn], 0, 1)[:, :, None, :]
        qi = jnp.swapaxes(pi[powers, dirn], 0, 1)[:, :, None, :]
        er = qr * bbr[dirn][:, None] - qi * bbi[dirn][:, None]
        ei = qr * bbi[dirn][:, None] + qi * bbr[dirn][:, None]
        return er.reshape(S5_GROUPS, wide, S5_STATE), ei.reshape(S5_GROUPS, wide, S5_STATE)

    tt = np.arange(n)
    bst = jnp.stack([*edge_states(n - 1 - tt, 0), *edge_states(tt, 1)], axis=1)

    def state_readout(powers, dirn):
        qr = pr[powers, dirn].transpose(1, 2, 0)[:, :, :, None]
        qi = pi[powers, dirn].transpose(1, 2, 0)[:, :, :, None]
        ctr = jnp.swapaxes(cr[dirn], 1, 2)[:, :, None, :]
        cti = jnp.swapaxes(ci[dirn], 1, 2)[:, :, None, :]
        rr = ctr * qr - cti * qi
        ri = -(ctr * qi + cti * qr)
        return rr.reshape(S5_GROUPS, S5_STATE, wide), ri.reshape(S5_GROUPS, S5_STATE, wide)

    cst = jnp.stack([*state_readout(tt + 1, 0), *state_readout(n - tt, 1)], axis=1)
    lamt = jnp.stack([pr[n, 0], pi[n, 0], pr[n, 1], pi[n, 1]], axis=1)
    zpad = LANES - S5_STATE
    bst = jnp.pad(bst, ((0, 0), (0, 0), (0, 0), (0, zpad)))
    cst = jnp.pad(cst, ((0, 0), (0, 0), (0, zpad), (0, 0)))
    lamt = jnp.pad(lamt, ((0, 0), (0, 0), (0, zpad)))
    return kmat.astype(BF16), bst.astype(BF16), cst.astype(BF16), lamt[:, :, None, :]


def _swap_halves(w, unit):
    n = w.shape[-1]
    idx = np.arange(n)
    half = unit // 2
    return w[..., np.where(idx % unit < half, idx + half, idx - half)]


def _rope_tables(n_pos):
    n = DA_QK // 4
    inv = ROPE_THETA ** (-jnp.arange(n, dtype=F32) / n)
    pos = jnp.arange(n_pos, dtype=jnp.int32)
    ang_r = (pos // GRID_W).astype(F32)[:, None] * inv[None, :]
    ang_c = (pos % GRID_W).astype(F32)[:, None] * inv[None, :]
    cos32 = jnp.concatenate([jnp.cos(ang_r)] * 2 + [jnp.cos(ang_c)] * 2, axis=1)
    sin32 = jnp.concatenate([-jnp.sin(ang_r), jnp.sin(ang_r), -jnp.sin(ang_c), jnp.sin(ang_c)], axis=1)
    pad = LANES - MLA_ROPE
    cosm = jnp.concatenate([cos32, jnp.ones((n_pos, pad), F32)], axis=1)
    sinm = jnp.concatenate([sin32, jnp.zeros((n_pos, pad), F32)], axis=1)
    return jnp.tile(cos32, (1, 4)), jnp.tile(sin32, (1, 4)), cosm, sinm


def _prep_ffn(w_in, w_out):
    return w_in.astype(BF16), w_out.astype(BF16)


def _prep_inproj(w, rope):
    o = np.cumsum([0, DA_WIDTH, DA_WIDTH, DA_WIDTH, MLA_Q_RANK, MLA_KV_RANK, MLA_ROPE, S5_WIDTH])
    wq, wk, wv, wcq, wckv, wkpe, wu = (w[:, o[i]:o[i + 1]] for i in range(7))
    kpe_pad = lambda m: jnp.pad(m, ((0, 0), (0, LANES - MLA_ROPE)))
    cols = [wq, wk, wv, wcq, wckv, wu, kpe_pad(wkpe)]
    if rope:
        half = DA_QK // 2
        cols += [_swap_halves(wq, half), _swap_halves(wk, half), kpe_pad(_swap_halves(wkpe, half))]
    return jnp.concatenate(cols, axis=1).astype(BF16)


def _prep_mla(q_norm_w, w_q_up, kv_norm_w, w_kv_up):
    pad = LANES - MLA_ROPE - MLA_NOPE
    wq = w_q_up.reshape(MLA_Q_RANK, MLA_HEADS, MLA_NOPE + MLA_ROPE)
    nope, rp = wq[..., :MLA_NOPE], wq[..., MLA_NOPE:]
    zeros = lambda n: jnp.zeros((MLA_Q_RANK, MLA_HEADS, n), F32)
    wqup = jnp.concatenate([rp, nope, zeros(pad)], axis=-1).reshape(MLA_Q_RANK, MLA_HEADS * LANES)
    wqsw = jnp.concatenate([_swap_halves(rp, DA_QK // 2), zeros(MLA_NOPE + pad)], axis=-1)
    wqsw = wqsw.reshape(MLA_Q_RANK, MLA_HEADS * LANES)
    wkv = w_kv_up.reshape(MLA_KV_RANK, MLA_HEADS, MLA_NOPE + MLA_V)
    zk = lambda n: jnp.zeros((MLA_KV_RANK, MLA_HEADS, n), F32)
    wk = jnp.concatenate([zk(MLA_ROPE), wkv[..., :MLA_NOPE], zk(pad)], axis=-1)
    wk = wk.reshape(MLA_KV_RANK, MLA_HEADS * LANES)
    wv = wkv[..., MLA_NOPE:].reshape(MLA_KV_RANK, MLA_WIDTH)
    return (q_norm_w.reshape(1, MLA_Q_RANK), kv_norm_w.reshape(1, MLA_KV_RANK),
            wqup.astype(BF16), wqsw.astype(BF16), wk.astype(BF16), wv.astype(BF16))


def _prep_mix(w_glu, b_glu, w_out):
    return (w_glu.astype(BF16), b_glu.reshape(1, S5_WIDTH),
            w_out[:DA_WIDTH].reshape(N_SLABS, LANES, D_MODEL).astype(BF16),
            w_out[DA_WIDTH:DA_WIDTH + MLA_WIDTH].reshape(N_SLABS, LANES, D_MODEL).astype(BF16),
            w_out[DA_WIDTH + MLA_WIDTH:].astype(BF16))


def _slabs_t(a, n_batch):
    b, n, _ = a.shape
    return a.reshape(b * n, N_SLABS, LANES).transpose(1, 2, 0).astype(BF16)


def _slabs(a):
    b, n, _ = a.shape
    return a.reshape(b * n, N_SLABS, LANES).transpose(1, 0, 2).astype(BF16)


def kernel(x_prompt, x_sample, cache_diff_k, cache_diff_v, cache_mla_ckv, cache_mla_kpe, state_s5_re, state_s5_im, c, c_ctx, w_ada, b_ada, norm_w, ffn_w_in, ffn_w_out, w_in, diff_lambda, diff_subln_w, mla_q_norm_w, mla_w_q_up, mla_kv_norm_w, mla_w_kv_up, s5_a_re, s5_a_im, s5_log_step, s5_b_re, s5_b_im, s5_c_re, s5_c_im, s5_d, s5_w_glu, s5_b_glu, w_out, final_norm_w):
    n_ctx, seq, _ = x_prompt.shape
    n_dec, dec_seq, _ = x_sample.shape
    past = cache_diff_k.shape[2]

    cvec = jnp.concatenate(
        [c_ctx[None, :], c, jnp.zeros((MOD_ROWS - 1 - n_dec, D_MODEL), F32)], axis=0)
    mod = _adaln(cvec, w_ada, b_ada)
    rope_tabs = _rope_tables(dec_seq)

    groups = {
        'ctx': dict(x=x_prompt.reshape(n_ctx * seq, D_MODEL), seg=n_ctx * seq, row=0, nb=n_ctx),
        'dec': dict(x=x_sample.reshape(n_dec * dec_seq, D_MODEL), seg=dec_seq, row=1, nb=n_dec),
    }
    caches = None
    s5_states = []
    for l in range(DEPTH):
        lam_init = 0.8 - 0.6 * math.exp(-0.3 * l)
        ffn1 = _prep_ffn(ffn_w_in[l, 0], ffn_w_out[l, 0])
        ffn2 = _prep_ffn(ffn_w_in[l, 1], ffn_w_out[l, 1])
        mla_w = _prep_mla(mla_q_norm_w[l], mla_w_q_up[l], mla_kv_norm_w[l], mla_w_kv_up[l])
        mix_w = _prep_mix(s5_w_glu[l], s5_b_glu[l], w_out[l])
        s5_ops = _s5_operators(s5_a_re[l], s5_a_im[l], s5_log_step[l], s5_b_re[l], s5_b_im[l],
                               s5_c_re[l], s5_c_im[l], s5_d[l])
        subw = jnp.tile(diff_subln_w[l], 2).reshape(1, LANES)
        final_w = final_norm_w if l == DEPTH - 1 else None
        for name, g in groups.items():
            dec = name == 'dec'
            x = _ffn_call(g['x'], mod[l], g['seg'], g['row'], norm_w[l, 0], ffn1, 0)
            proj = _inproj_call(x, mod[l], g['seg'], g['row'], norm_w[l, 1], _prep_inproj(w_in[l], dec),
                                mla_w, rope_tabs if dec else None,
                                cache_out=None if dec else (l, seq, caches))
            q, kt, v, qm, ktm, vm, xg = proj[:7]
            n_keys = g['x'].shape[0] // g['nb']
            parts = [(kt, v, ktm, vm, n_keys)]
            h0 = jnp.zeros((S5_GROUPS, 4, g['nb'], LANES), F32)
            if dec:
                ck = cache_diff_k[:, l].reshape(n_dec, past, DA_WIDTH)
                cv = cache_diff_v[:, l].reshape(n_dec, past, DA_WIDTH)
                kpe128 = jnp.pad(cache_mla_kpe[:, l].reshape(n_dec * past, MLA_ROPE),
                                 ((0, 0), (0, LANES - MLA_ROPE)))
                cktm, cvm = _mla_expand_call(cache_mla_ckv[:, l].reshape(n_dec * past, MLA_KV_RANK),
                                             kpe128, mla_w[4], mla_w[5], past)
                parts.append((_slabs_t(ck, n_dec), _slabs(cv), cktm, cvm, past))
                sr, si = state_s5_re[:, l], state_s5_im[:, l]
                h0 = jnp.stack([sr[:, 0], si[:, 0], sr[:, 1], si[:, 1]], 0).transpose(2, 0, 1, 3)
                h0 = jnp.pad(h0, ((0, 0), (0, 0), (0, 0), (0, LANES - S5_STATE)))
            o_a, o_b = _attn_call(q, qm, diff_lambda[l], subw, parts, g['nb'], lam_init)
            y_s5, fin = _s5_call(xg, *s5_ops, h0, g['nb'])
            g['x'] = _ffn_call(x, mod[l], g['seg'], g['row'], norm_w[l, 2], ffn2, 6,
                               mix_args=(o_a, o_b, y_s5), mix_w=mix_w, final_w=final_w)
            if not dec:
                caches = proj[7:]
                fin = fin[..., :S5_STATE].transpose(1, 2, 0, 3)
                s5_states.append((jnp.stack([fin[0], fin[2]], axis=1), jnp.stack([fin[1], fin[3]], axis=1)))

    y_prompt = groups['ctx']['x'].reshape(n_ctx, seq, D_MODEL)
    y_sample = groups['dec']['x'].reshape(n_dec, dec_seq, D_MODEL)
    new_k, new_v, new_ckv, new_kpe = caches
    new_s5_re = jnp.stack([t[0] for t in s5_states], axis=1)
    new_s5_im = jnp.stack([t[1] for t in s5_states], axis=1)
    return (y_prompt, y_sample,
            new_k.reshape(n_ctx, DEPTH, seq, DA_HEADS, 2 * DA_QK),
            new_v.reshape(n_ctx, DEPTH, seq, DA_HEADS, DA_V),
            new_ckv, new_kpe, new_s5_re, new_s5_im)
```

```python
import functools
import math

import jax
import jax.numpy as jnp
import numpy as np
from jax import lax
from jax.experimental import pallas as pl
from jax.experimental.pallas import tpu as pltpu

F32 = jnp.float32
BF16 = jnp.bfloat16

D_MODEL = 1024
DEPTH = 2
GRID_W = 64
ROPE_THETA = 10000.0
EPS = 1e-6
N_MOD = 9
D_FF = 2816
DA_HEADS = 6
DA_QK = 32
DA_V = 64
DA_WIDTH = DA_HEADS * DA_V
MLA_HEADS = 6
MLA_Q_RANK = 256
MLA_KV_RANK = 128
MLA_NOPE = 64
MLA_ROPE = 32
MLA_V = 64
MLA_WIDTH = MLA_HEADS * MLA_V
S5_WIDTH = D_MODEL - DA_WIDTH - MLA_WIDTH
S5_CH = 16
S5_GROUPS = S5_WIDTH // S5_CH
S5_STATE = 64

LANES = 128
FF_CHUNK = 256
N_FF_CHUNKS = D_FF // FF_CHUNK
TOKEN_TILE = 512
Q_TILE = 256
S5_CHUNK = 16
N_SLABS = DA_WIDTH // LANES
MOD_ROWS = 8
VMEM_LIMIT = 56 * 1024 * 1024


def _rms(x, w):
    return x * lax.rsqrt(jnp.mean(x * x, axis=-1, keepdims=True) + EPS) * w


def _silu(x):
    return x * jax.nn.sigmoid(x)


def _gelu_tanh(x):
    c = math.sqrt(2.0 / math.pi)
    return x * (0.5 * (1.0 + jnp.tanh(c * (x + 0.044715 * (x * x * x)))))


def _dot(a, b):
    return jnp.dot(a, b, preferred_element_type=F32)


def _resident(shape):
    nd = len(shape)
    return pl.BlockSpec(shape, lambda *_: (0,) * nd, pipeline_mode=pl.Buffered(1))


def _adaln_kernel(c_ref, w_ref, b_ref, o_ref):
    s = _silu(c_ref[...])
    o_ref[...] = jnp.dot(s, w_ref[...], preferred_element_type=F32,
                         precision=lax.Precision.HIGHEST) + b_ref[...]


def _adaln(cvec, w_ada, b_ada):
    n = N_MOD * D_MODEL
    tn = 1152
    out = pl.pallas_call(
        _adaln_kernel,
        grid=(DEPTH, n // tn),
        in_specs=[
            pl.BlockSpec((MOD_ROWS, D_MODEL), lambda l, j: (0, 0)),
            pl.BlockSpec((None, D_MODEL, tn), lambda l, j: (l, 0, j)),
            pl.BlockSpec((None, 1, tn), lambda l, j: (l, 0, j)),
        ],
        out_specs=pl.BlockSpec((None, MOD_ROWS, tn), lambda l, j: (l, 0, j)),
        out_shape=jax.ShapeDtypeStruct((DEPTH, MOD_ROWS, n), F32),
        compiler_params=pltpu.CompilerParams(
            dimension_semantics=("arbitrary", "arbitrary"), vmem_limit_bytes=VMEM_LIMIT),
        name="adaln",
    )(cvec, w_ada, b_ada.reshape(DEPTH, 1, n))
    return out.reshape(DEPTH, MOD_ROWS, N_MOD, D_MODEL)


_ROWS_PER_TILE = TOKEN_TILE // S5_CHUNK
_SLOTS = LANES // S5_CH


def _to_group_major(u_scr, xg_ref):
    slot = lax.broadcasted_iota(jnp.int32, (_ROWS_PER_TILE, LANES), 1) // S5_CH
    for g in range(S5_GROUPS):
        for tt in range(S5_CHUNK // _SLOTS):
            acc = None
            for j in range(_SLOTS):
                t = tt * _SLOTS + j
                src = u_scr[g // _SLOTS, pl.ds(t, _ROWS_PER_TILE, stride=S5_CHUNK), :]
                shift = ((j - g % _SLOTS) % _SLOTS) * S5_CH
                piece = pltpu.roll(src, shift, 1) if shift else src
                acc = piece if acc is None else jnp.where(slot == j, piece, acc)
            xg_ref[g, :, tt * LANES:(tt + 1) * LANES] = acc.astype(BF16)


def _from_group_major(yg_ref, y_scr):
    slot = lax.broadcasted_iota(jnp.int32, (_ROWS_PER_TILE, LANES), 1) // S5_CH
    for t in range(S5_CHUNK):
        for gc in range(S5_GROUPS // _SLOTS):
            acc = None
            for m in range(_SLOTS):
                src = yg_ref[gc * _SLOTS + m, :, (t // _SLOTS) * LANES:(t // _SLOTS + 1) * LANES]
                shift = ((m - t % _SLOTS) % _SLOTS) * S5_CH
                piece = pltpu.roll(src, shift, 1) if shift else src
                acc = piece if acc is None else jnp.where(slot == m, piece, acc)
            y_scr[gc, pl.ds(t, _ROWS_PER_TILE, stride=S5_CHUNK), :] = acc


def _ffn_kernel(mix, final, mod_base, *refs):
    it = iter(refs)
    x_ref, mod_ref, nw_ref, wi_ref, wo_ref = (next(it) for _ in range(5))
    if mix:
        oa_ref, ob_ref, ys_ref, wglu_ref, bglu_ref, woa_ref, wob_ref, woc_ref = (next(it) for _ in range(8))
    if final:
        fnw_ref = next(it)
    out_ref, h_scr, acc_scr = next(it), next(it), next(it)

    x = x_ref[...]
    if mix:
        y_scr = next(it)
        _from_group_major(ys_ref, y_scr)
        yc = _gelu_tanh(jnp.concatenate([y_scr[gc] for gc in range(S5_WIDTH // LANES)], axis=1))
        oc = yc * jax.nn.sigmoid(_dot(yc.astype(BF16), wglu_ref[...]) + bglu_ref[...])
        m = _dot(oc.astype(BF16), woc_ref[...])
        for s in range(N_SLABS):
            m += _dot(oa_ref[s], woa_ref[s]) + _dot(ob_ref[s], wob_ref[s])
        x = x + mod_ref[0, mod_base - 1:mod_base, :] * m

    shift = mod_ref[0, mod_base:mod_base + 1, :]
    scale = mod_ref[0, mod_base + 1:mod_base + 2, :]
    gate = mod_ref[0, mod_base + 2:mod_base + 3, :]
    h = _rms(x, nw_ref[...]) * (1.0 + scale) + shift
    h_scr[...] = h.astype(BF16)
    acc_scr[...] = jnp.zeros_like(acc_scr)

    def chunk(j, carry):
        hb = h_scr[...]
        col = pl.multiple_of(j * FF_CHUNK, FF_CHUNK)
        a = _dot(hb, wi_ref[:, pl.ds(col, FF_CHUNK)])
        g = _dot(hb, wi_ref[:, pl.ds(D_FF + col, FF_CHUNK)])
        acc_scr[...] += _dot((_silu(g) * a).astype(BF16), wo_ref[pl.ds(col, FF_CHUNK), :])
        return carry

    lax.fori_loop(0, N_FF_CHUNKS, chunk, 0)
    y = x + 0.5 * gate * acc_scr[...]
    if final:
        y = _rms(y, fnw_ref[...])
    out_ref[...] = y


def _ffn_call(x, mod_l, seg_len, row_base, nw, ffn_w, mod_base, mix_args=None, mix_w=None, final_w=None):
    t = x.shape[0]
    tm = TOKEN_TILE
    mix = mix_args is not None
    final = final_w is not None
    layer, which, w_in_all, w_out_all = ffn_w
    tok = lambda i: (i, 0)
    in_specs = [
        pl.BlockSpec((tm, D_MODEL), tok),
        pl.BlockSpec((1, N_MOD, D_MODEL), lambda i: (row_base + (i * tm) // seg_len, 0, 0)),
        _resident((1, D_MODEL)),
        pl.BlockSpec((None, None, D_MODEL, 2 * D_FF), lambda i: (layer, which, 0, 0), pipeline_mode=pl.Buffered(1)),
        pl.BlockSpec((None, None, D_FF, D_MODEL), lambda i: (layer, which, 0, 0), pipeline_mode=pl.Buffered(1)),
    ]
    args = [x, mod_l, nw.reshape(1, D_MODEL), w_in_all, w_out_all]
    if mix:
        in_specs += [
            pl.BlockSpec((N_SLABS, tm, LANES), lambda i: (0, i, 0)),
            pl.BlockSpec((N_SLABS, tm, LANES), lambda i: (0, i, 0)),
            pl.BlockSpec((S5_GROUPS, _ROWS_PER_TILE, S5_CHUNK * S5_CH), lambda i: (0, i, 0)),
            _resident((S5_WIDTH, S5_WIDTH)),
            _resident((1, S5_WIDTH)),
            _resident((N_SLABS, LANES, D_MODEL)),
            _resident((N_SLABS, LANES, D_MODEL)),
            _resident((S5_WIDTH, D_MODEL)),
        ]
        args += [*mix_args, *mix_w]
    if final:
        in_specs.append(_resident((1, D_MODEL)))
        args.append(final_w.reshape(1, D_MODEL))
    return pl.pallas_call(
        functools.partial(_ffn_kernel, mix, final, mod_base),
        grid=(t // tm,),
        in_specs=in_specs,
        out_specs=pl.BlockSpec((tm, D_MODEL), tok),
        out_shape=jax.ShapeDtypeStruct((t, D_MODEL), F32),
        scratch_shapes=[pltpu.VMEM((tm, D_MODEL), BF16), pltpu.VMEM((tm, D_MODEL), F32)]
        + ([pltpu.VMEM((S5_WIDTH // LANES, tm, LANES), F32)] if mix else []),
        compiler_params=pltpu.CompilerParams(
            dimension_semantics=("arbitrary",), vmem_limit_bytes=VMEM_LIMIT),
        name="mix_ffn" if mix else "ffn",
    )(*args)


_C_Q, _C_K, _C_V = 0, DA_WIDTH, 2 * DA_WIDTH
_C_CQ = 3 * DA_WIDTH
_C_CKV = _C_CQ + MLA_Q_RANK
_C_U = _C_CKV + MLA_KV_RANK
_C_KPE = _C_U + S5_WIDTH
_C_QSW = _C_KPE + LANES
_C_KSW = _C_QSW + DA_WIDTH
_C_KPESW = _C_KSW + DA_WIDTH
_NCOL_CTX = _C_QSW
_NCOL_DEC = _C_KPESW + LANES

_QSCALE_DIFF = DA_QK ** -0.5 * math.log2(math.e)
_QSCALE_MLA = (MLA_NOPE + MLA_ROPE) ** -0.5 * math.log2(math.e)


def _inproj_kernel(rope, n_aliased, *refs):
    it = iter(refs)
    x_ref, mod_ref, nw_ref, w_ref, qnw_ref, kvnw_ref, wqup_ref = (next(it) for _ in range(7))
    if rope:
        wqsw_ref = next(it)
    wk_ref, wv_ref = next(it), next(it)
    if rope:
        cos_ref, sin_ref, cosm_ref, sinm_ref = (next(it) for _ in range(4))
    for _ in range(n_aliased):
        next(it)
    q_ref, kt_ref, v_ref, qm_ref, ktm_ref, vm_ref, xg_ref = (next(it) for _ in range(7))
    if not rope:
        ka_ref, va_ref, ckv_ref, kpe_ref = (next(it) for _ in range(4))
    u_scr = next(it)

    h = _rms(x_ref[...], nw_ref[...]) * (1.0 + mod_ref[0, 4:5, :]) + mod_ref[0, 3:4, :]
    big = _dot(h.astype(BF16), w_ref[...])

    def cols(start, width):
        return big[:, start:start + width]

    if rope:
        cos, sin, cosm, sinm = cos_ref[...], sin_ref[...], cosm_ref[...], sinm_ref[...]
    for s in range(N_SLABS):
        qs = cols(_C_Q + s * LANES, LANES)
        ks = cols(_C_K + s * LANES, LANES)
        if rope:
            qs = qs * cos + cols(_C_QSW + s * LANES, LANES) * sin
            ks = ks * cos + cols(_C_KSW + s * LANES, LANES) * sin
        q_ref[s] = (qs * _QSCALE_DIFF).astype(BF16)
        kt_ref[s] = ks.T.astype(BF16)
        v_ref[s] = cols(_C_V + s * LANES, LANES).astype(BF16)
    for gc in range(S5_WIDTH // LANES):
        u_scr[gc] = cols(_C_U + gc * LANES, LANES)
    _to_group_major(u_scr, xg_ref)

    cqn = _rms(cols(_C_CQ, MLA_Q_RANK), qnw_ref[...]).astype(BF16)
    qm = _dot(cqn, wqup_ref[...])
    if rope:
        qsw = _dot(cqn, wqsw_ref[...])
    ckvn = _rms(cols(_C_CKV, MLA_KV_RANK), kvnw_ref[...])
    ckvb = ckvn.astype(BF16)
    kn = _dot(ckvb, wk_ref[...])
    vm = _dot(ckvb, wv_ref[...])
    kpe = cols(_C_KPE, LANES)
    if rope:
        kpe = kpe * cosm + cols(_C_KPESW, LANES) * sinm
    for hd in range(MLA_HEADS):
        qh = qm[:, hd * LANES:(hd + 1) * LANES]
        if rope:
            qh = qh * cosm + qsw[:, hd * LANES:(hd + 1) * LANES] * sinm
        qm_ref[hd] = (qh * _QSCALE_MLA).astype(BF16)
        ktm_ref[hd] = (kn[:, hd * LANES:(hd + 1) * LANES] + kpe).T.astype(BF16)
    for s in range(N_SLABS):
        vm_ref[s] = vm[:, s * LANES:(s + 1) * LANES].astype(BF16)

    if not rope:
        n_seq, seq = ka_ref.shape[:2]
        for b in range(n_seq):
            rows = slice(b * seq, (b + 1) * seq)
            ka_ref[b] = cols(_C_K, DA_WIDTH)[rows]
            va_ref[b] = cols(_C_V, DA_WIDTH)[rows]
            ckv_ref[b] = ckvn[rows]
            kpe_ref[b] = cols(_C_KPE, MLA_ROPE)[rows]


def _inproj_call(x, mod_l, seg_len, row_base, nw, w, mla_w, rope_tabs, cache_out=None):
    t = x.shape[0]
    tm = TOKEN_TILE
    rope = rope_tabs is not None
    ncol = _NCOL_DEC if rope else _NCOL_CTX
    qnw, kvnw, wqup, wqsw, wk, wv = mla_w
    tok = lambda i: (i, 0)
    slab = lambda i: (0, i, 0)
    slab_t = lambda i: (0, 0, i)
    in_specs = [
        pl.BlockSpec((tm, D_MODEL), tok),
        pl.BlockSpec((1, N_MOD, D_MODEL), lambda i: (row_base + (i * tm) // seg_len, 0, 0)),
        _resident((1, D_MODEL)),
        _resident((D_MODEL, ncol)),
        _resident((1, MLA_Q_RANK)),
        _resident((1, MLA_KV_RANK)),
        _resident((MLA_Q_RANK, MLA_HEADS * LANES)),
    ]
    args = [x, mod_l, nw.reshape(1, D_MODEL), w, qnw, kvnw, wqup]
    if rope:
        in_specs.append(_resident((MLA_Q_RANK, MLA_HEADS * LANES)))
        args.append(wqsw)
    in_specs += [_resident((MLA_KV_RANK, MLA_HEADS * LANES)), _resident((MLA_KV_RANK, MLA_WIDTH))]
    args += [wk, wv]
    if rope:
        n_pos = rope_tabs[0].shape[0]
        pos = lambda i: (i % (n_pos // tm), 0)
        in_specs += [pl.BlockSpec((tm, LANES), pos)] * 4
        args += list(rope_tabs)
    out_shape = [
        jax.ShapeDtypeStruct((N_SLABS, t, LANES), BF16),
        jax.ShapeDtypeStruct((N_SLABS, LANES, t), BF16),
        jax.ShapeDtypeStruct((N_SLABS, t, LANES), BF16),
        jax.ShapeDtypeStruct((MLA_HEADS, t, LANES), BF16),
        jax.ShapeDtypeStruct((MLA_HEADS, LANES, t), BF16),
        jax.ShapeDtypeStruct((N_SLABS, t, LANES), BF16),
        jax.ShapeDtypeStruct((S5_GROUPS, t // S5_CHUNK, S5_CHUNK * S5_CH), BF16),
    ]
    out_specs = [
        pl.BlockSpec((N_SLABS, tm, LANES), slab),
        pl.BlockSpec((N_SLABS, LANES, tm), slab_t),
        pl.BlockSpec((N_SLABS, tm, LANES), slab),
        pl.BlockSpec((MLA_HEADS, tm, LANES), slab),
        pl.BlockSpec((MLA_HEADS, LANES, tm), slab_t),
        pl.BlockSpec((N_SLABS, tm, LANES), slab),
        pl.BlockSpec((S5_GROUPS, _ROWS_PER_TILE, S5_CHUNK * S5_CH), slab),
    ]
    aliases = {}
    if not rope:
        layer, seq, prev = cache_out
        n_first = len(out_shape)
        for width in (DA_WIDTH, DA_WIDTH, MLA_KV_RANK, MLA_ROPE):
            out_shape.append(jax.ShapeDtypeStruct((t // seq, DEPTH, seq, width), F32))
            out_specs.append(pl.BlockSpec((tm // seq, None, seq, width), lambda i: (i, layer, 0, 0)))
        if prev is not None:
            for k, arr in enumerate(prev):
                aliases[len(args)] = n_first + k
                in_specs.append(pl.BlockSpec(memory_space=pl.ANY))
                args.append(arr)
    return pl.pallas_call(
        functools.partial(_inproj_kernel, rope, len(aliases)),
        grid=(t // tm,),
        in_specs=in_specs,
        out_specs=out_specs,
        out_shape=out_shape,
        input_output_aliases=aliases,
        scratch_shapes=[pltpu.VMEM((S5_WIDTH // LANES, tm, LANES), F32)],
        compiler_params=pltpu.CompilerParams(
            dimension_semantics=("arbitrary",), vmem_limit_bytes=VMEM_LIMIT),
        name="inproj_dec" if rope else "inproj_ctx",
    )(*args)


def _mla_expand_kernel(ckv_ref, kpe_ref, wk_ref, wv_ref, ktm_ref, vm_ref):
    ckvb = ckv_ref[...].astype(BF16)
    kn = _dot(ckvb, wk_ref[...])
    vm = _dot(ckvb, wv_ref[...])
    kpe = kpe_ref[...]
    for hd in range(MLA_HEADS):
        ktm_ref[hd] = (kn[:, hd * LANES:(hd + 1) * LANES] + kpe).T.astype(BF16)
    for s in range(N_SLABS):
        vm_ref[s] = vm[:, s * LANES:(s + 1) * LANES].astype(BF16)


def _mla_expand_call(ckv, kpe128, wk, wv, rows):
    t = ckv.shape[0]
    return pl.pallas_call(
        _mla_expand_kernel,
        grid=(t // rows,),
        in_specs=[
            pl.BlockSpec((rows, MLA_KV_RANK), lambda i: (i, 0)),
            pl.BlockSpec((rows, LANES), lambda i: (i, 0)),
            _resident((MLA_KV_RANK, MLA_HEADS * LANES)),
            _resident((MLA_KV_RANK, MLA_WIDTH)),
        ],
        out_specs=[
            pl.BlockSpec((MLA_HEADS, LANES, rows), lambda i: (0, 0, i)),
            pl.BlockSpec((N_SLABS, rows, LANES), lambda i: (0, i, 0)),
        ],
        out_shape=[
            jax.ShapeDtypeStruct((MLA_HEADS, LANES, t), BF16),
            jax.ShapeDtypeStruct((N_SLABS, t, LANES), BF16),
        ],
        compiler_params=pltpu.CompilerParams(dimension_semantics=("arbitrary",)),
        name="mla_expand",
    )(ckv, kpe128, wk, wv)


KEY_TILE = 256
N_SCORE_BUFS = 4


def _interleave(stages):
    stages = list(stages)
    while stages:
        for g in list(stages):
            try:
                next(g)
            except StopIteration:
                stages.remove(g)


def _attn_kernel(n_parts, lam_init, *refs):
    q_ref, qm_ref, lamp_ref, subw_ref = refs[:4]
    parts = [refs[4 + 4 * i:8 + 4 * i] for i in range(n_parts)]
    oa_ref, ob_ref, sbuf = refs[4 + 4 * n_parts:]

    lp = lamp_ref[...]
    lam = (jnp.exp(jnp.sum(lp[0:1] * lp[1:2], keepdims=True))
           - jnp.exp(jnp.sum(lp[2:3] * lp[3:4], keepdims=True)) + lam_init)
    tq = q_ref.shape[1]
    lane = lax.broadcasted_iota(jnp.int32, (tq, LANES), 1)
    low_half = lane < DA_V
    subw = subw_ref[...] * (1.0 - lam_init)

    tiles, col = [], 0
    for p, (kt, _, _, _) in enumerate(parts):
        for k0 in range(0, kt.shape[2], KEY_TILE):
            tiles.append((p, k0, col))
            col += KEY_TILE

    def lane_fold(op, acc, x):
        for j in range(x.shape[1] // LANES):
            piece = x[:, j * LANES:(j + 1) * LANES]
            acc = piece if acc is None else op(acc, piece)
        return acc

    def scores(unit):
        buf = unit['buf']
        if unit['mla']:
            lhs, kref = qm_ref[unit['head']], 2
            kidx = unit['head']
        else:
            qs = q_ref[unit['slab']]
            lhs, kref = jnp.where(lane // DA_QK == unit['sub'], qs, jnp.zeros_like(qs)), 0
            kidx = unit['slab']
        mx = None
        for p, k0, c0 in tiles:
            s = _dot(lhs, parts[p][kref][kidx, :, k0:k0 + KEY_TILE])
            sbuf[buf, :, c0:c0 + KEY_TILE] = s
            mx = lane_fold(jnp.maximum, mx, s)
            yield
        unit['max'] = jnp.max(mx, axis=1, keepdims=True)

    def expo(unit):
        buf, tot = unit['buf'], None
        for _, _, c0 in tiles:
            e = jnp.exp2(sbuf[buf, :, c0:c0 + KEY_TILE] - unit['max'])
            sbuf[buf, :, c0:c0 + KEY_TILE] = e
            tot = lane_fold(jnp.add, tot, e)
            yield
        unit['sum'] = jnp.sum(tot, axis=1, keepdims=True)

    def pv(head):
        units, slab = head['units'], head['slab']
        vref = 3 if units[0]['mla'] else 1
        l0 = units[0]['sum']
        if len(units) == 2:
            r = lam * l0 / units[1]['sum']
        acc = None
        for p, k0, c0 in tiles:
            w = sbuf[units[0]['buf'], :, c0:c0 + KEY_TILE]
            if len(units) == 2:
                w = w - r * sbuf[units[1]['buf'], :, c0:c0 + KEY_TILE]
            o = _dot(w.astype(BF16), parts[p][vref][slab, k0:k0 + KEY_TILE, :])
            acc = o if acc is None else acc + o
            yield
        head['out'] = acc * (1.0 / l0)

    def finish(slab, mla, lo, hi):
        o = jnp.where(low_half, lo['out'], hi['out'])
        if mla:
            ob_ref[slab] = o.astype(BF16)
            return
        o2 = o * o
        ss_lo = jnp.sum(jnp.where(low_half, o2, 0.0), axis=1, keepdims=True)
        ss_hi = jnp.sum(jnp.where(low_half, 0.0, o2), axis=1, keepdims=True)
        ms = jnp.where(low_half, ss_lo, ss_hi) * (1.0 / DA_V)
        oa_ref[slab] = (o * lax.rsqrt(ms + EPS) * subw).astype(BF16)

    units, heads = [], []
    for slab in range(N_SLABS):
        for mla in (False, True):
            for half in range(2):
                mine = [dict(mla=mla, slab=slab, head=2 * slab + half, sub=2 * half + c)
                        for c in range(1 if mla else 2)]
                heads.append(dict(units=mine, slab=slab, mla=mla, half=half, last=len(units) + len(mine) - 1))
                units += mine
    for k, unit in enumerate(units):
        unit['buf'] = k % N_SCORE_BUFS

    pending = None
    for k in range(len(units) + 2):
        stages = []
        if k < len(units):
            stages.append(scores(units[k]))
        if 1 <= k <= len(units):
            stages.append(expo(units[k - 1]))
        ready = [h for h in heads if h['last'] == k - 2]
        if ready:
            stages.append(pv(ready[0]))
        _interleave(stages)
        if ready and ready[0]['half'] == 1:
            finish(ready[0]['slab'], ready[0]['mla'], pending[(ready[0]['slab'], ready[0]['mla'])], ready[0])
        elif ready:
            pending = dict(pending or {})
            pending[(ready[0]['slab'], ready[0]['mla'])] = ready[0]


def _attn_call(q, qm, lamp, subw, parts, n_batch, lam_init):
    t = q.shape[1]
    seq = t // n_batch
    tq = min(Q_TILE, seq)
    nq = seq // tq
    qmap = lambda b, i: (0, b * nq + i, 0)
    in_specs = [
        pl.BlockSpec((N_SLABS, tq, LANES), qmap),
        pl.BlockSpec((MLA_HEADS, tq, LANES), qmap),
        pl.BlockSpec((4, DA_QK), lambda b, i: (0, 0)),
        pl.BlockSpec((1, LANES), lambda b, i: (0, 0)),
    ]
    args = [q, qm, lamp, subw]
    for kt, v, ktm, vm, n_keys in parts:
        in_specs += [
            pl.BlockSpec((N_SLABS, LANES, n_keys), lambda b, i: (0, 0, b)),
            pl.BlockSpec((N_SLABS, n_keys, LANES), lambda b, i: (0, b, 0)),
            pl.BlockSpec((MLA_HEADS, LANES, n_keys), lambda b, i: (0, 0, b)),
            pl.BlockSpec((N_SLABS, n_keys, LANES), lambda b, i: (0, b, 0)),
        ]
        args += [kt, v, ktm, vm]
    return pl.pallas_call(
        functools.partial(_attn_kernel, len(parts), lam_init),
        grid=(n_batch, nq),
        in_specs=in_specs,
        out_specs=[pl.BlockSpec((N_SLABS, tq, LANES), qmap)] * 2,
        out_shape=[jax.ShapeDtypeStruct((N_SLABS, t, LANES), BF16)] * 2,
        scratch_shapes=[pltpu.VMEM((N_SCORE_BUFS, tq, sum(p[4] for p in parts)), F32)],
        compiler_params=pltpu.CompilerParams(
            dimension_semantics=("arbitrary", "arbitrary"), vmem_limit_bytes=VMEM_LIMIT),
        name="attn_dec" if len(parts) > 1 else "attn_ctx",
    )(*args)


def _s5_kernel(n_batch, n_chunks, x_ref, kmat_ref, bst_ref, cst_ref, lamt_ref, h0_ref, y_ref, fin_ref,
               s_scr, hin_scr):
    x = x_ref[...]
    y = _dot(x, kmat_ref[...])
    for k in range(4):
        s_scr[k] = _dot(x, bst_ref[k])

    a_fr, a_fi, a_br, a_bi = (lamt_ref[k] for k in range(4))

    def chunk_rows(j):
        return pl.ds(j, n_batch, stride=n_chunks)

    def step(j, carry):
        fr, fi, br, bi = carry
        rf, rb = chunk_rows(j), chunk_rows(n_chunks - 1 - j)
        hin_scr[0, rf, :] = fr
        hin_scr[1, rf, :] = fi
        hin_scr[2, rb, :] = br
        hin_scr[3, rb, :] = bi
        nfr = a_fr * fr - a_fi * fi + s_scr[0, rf, :]
        nfi = a_fr * fi + a_fi * fr + s_scr[1, rf, :]
        nbr = a_br * br - a_bi * bi + s_scr[2, rb, :]
        nbi = a_br * bi + a_bi * br + s_scr[3, rb, :]
        return nfr, nfi, nbr, nbi

    fin = lax.fori_loop(0, n_chunks, step, tuple(h0_ref[k] for k in range(4)))
    for k in range(4):
        fin_ref[k] = fin[k]
        y += _dot(hin_scr[k].astype(BF16), cst_ref[k])
    y_ref[...] = y


def _s5_call(xg, kmat, bst, cst, lamt, h0, n_batch):
    rows = xg.shape[1]
    n_chunks = rows // n_batch
    wide = S5_CHUNK * S5_CH
    grp = lambda g: (g, 0, 0)
    grp4 = lambda g: (g, 0, 0, 0)
    return pl.pallas_call(
        functools.partial(_s5_kernel, n_batch, n_chunks),
        grid=(S5_GROUPS,),
        in_specs=[
            pl.BlockSpec((None, rows, wide), grp),
            pl.BlockSpec((None, wide, wide), grp),
            pl.BlockSpec((None, 4, wide, LANES), grp4),
            pl.BlockSpec((None, 4, LANES, wide), grp4),
            pl.BlockSpec((None, 4, 1, LANES), grp4),
            pl.BlockSpec((None, 4, n_batch, LANES), grp4),
        ],
        out_specs=[
            pl.BlockSpec((None, rows, wide), grp),
            pl.BlockSpec((None, 4, n_batch, LANES), grp4),
        ],
        out_shape=[
            jax.ShapeDtypeStruct((S5_GROUPS, rows, wide), F32),
            jax.ShapeDtypeStruct((S5_GROUPS, 4, n_batch, LANES), F32),
        ],
        scratch_shapes=[pltpu.VMEM((4, rows, LANES), F32), pltpu.VMEM((4, rows, LANES), F32)],
        compiler_params=pltpu.CompilerParams(dimension_semantics=("arbitrary",)),
        name="s5",
    )(xg, kmat, bst, cst, lamt, h0)


def _s5_operators(a_re, a_im, log_step, b_re, b_im, c_re, c_im, d):
    hi = lax.Precision.HIGHEST
    n = S5_CHUNK
    wide = n * S5_CH
    ar, ai = a_re.astype(F32), a_im.astype(F32)
    step = jnp.exp(log_step.astype(F32))[..., None]
    zr, zi = ar * step, ai * step
    k = jnp.arange(n + 1, dtype=F32)[:, None, None, None]
    mag = jnp.exp(k * zr[None])
    pr, pi = mag * jnp.cos(k * zi[None]), mag * jnp.sin(k * zi[None])

    xr, xi = pr[1] - 1.0, pi[1]
    den = ar * ar + ai * ai
    fr, fi = ((xr * ar + xi * ai) / den)[:, :, None, :], ((xi * ar - xr * ai) / den)[:, :, None, :]
    br, bi = (jnp.swapaxes(b.astype(F32), 2, 3) for b in (b_re, b_im))
    bbr, bbi = fr * br - fi * bi, fr * bi + fi * br
    cr, ci = c_re.astype(F32), c_im.astype(F32)

    tt = np.arange(n)

    def state_readout(powers, dirn):
        qr = pr[powers, dirn].transpose(1, 2, 0)[:, :, :, None]
        qi = pi[powers, dirn].transpose(1, 2, 0)[:, :, :, None]
        ctr = jnp.swapaxes(cr[dirn], 1, 2)[:, :, None, :]
        cti = jnp.swapaxes(ci[dirn], 1, 2)[:, :, None, :]
        rr = ctr * qr - cti * qi
        ri = -(ctr * qi + cti * qr)
        return rr.reshape(S5_GROUPS, S5_STATE, wide), ri.reshape(S5_GROUPS, S5_STATE, wide)

    def lag_response(powers, dirn):
        rr, ri = state_readout(powers, dirn)
        return (jnp.einsum('gkp,gpn->gkn', bbr[dirn], rr, precision=hi)
                + jnp.einsum('gkp,gpn->gkn', bbi[dirn], ri, precision=hi))

    fwd, bwd = lag_response(tt, 0), lag_response(n - 1 - tt, 1)
    skip = d.astype(F32)[:, None, :] * jnp.eye(S5_CH, dtype=F32)[None]
    strip = jnp.concatenate([bwd[:, :, :wide - S5_CH], bwd[:, :, wide - S5_CH:] + fwd[:, :, :S5_CH] + skip,
                             fwd[:, :, S5_CH:]], axis=-1)
    kmat = jnp.stack([strip[:, :, (n - 1 - s) * S5_CH:(n - 1 - s) * S5_CH + wide] for s in range(n)], axis=1)
    kmat = kmat.reshape(S5_GROUPS, wide, wide)

    def edge_states(powers, dirn):
        qr = jnp.swapaxes(pr[powers, dirn], 0, 1)[:, :, None, :]
        qi = jnp.swapaxes(pi[powers, dirn], 0, 1)[:, :, None, :]
        er = qr * bbr[dirn][:, None] - qi * bbi[dirn][:, None]
        ei = qr * bbi[dirn][:, None] + qi * bbr[dirn][:, None]
        return er.reshape(S5_GROUPS, wide, S5_STATE), ei.reshape(S5_GROUPS, wide, S5_STATE)

    bst = jnp.stack([*edge_states(n - 1 - tt, 0), *edge_states(tt, 1)], axis=1)
    cst = jnp.stack([*state_readout(tt + 1, 0), *state_readout(n - tt, 1)], axis=1)
    lamt = jnp.stack([pr[n, 0], pi[n, 0], pr[n, 1], pi[n, 1]], axis=1)
    zpad = LANES - S5_STATE
    bst = jnp.pad(bst, ((0, 0), (0, 0), (0, 0), (0, zpad)))
    cst = jnp.pad(cst, ((0, 0), (0, 0), (0, zpad), (0, 0)))
    lamt = jnp.pad(lamt, ((0, 0), (0, 0), (0, zpad)))
    return kmat.astype(BF16), bst.astype(BF16), cst.astype(BF16), lamt[:, :, None, :]


def _swap_halves(w, unit):
    n = w.shape[-1]
    idx = np.arange(n)
    half = unit // 2
    return w[..., np.where(idx % unit < half, idx + half, idx - half)]


def _rope_tables(n_pos):
    n = DA_QK // 4
    inv = ROPE_THETA ** (-jnp.arange(n, dtype=F32) / n)
    pos = jnp.arange(n_pos, dtype=jnp.int32)
    ang_r = (pos // GRID_W).astype(F32)[:, None] * inv[None, :]
    ang_c = (pos % GRID_W).astype(F32)[:, None] * inv[None, :]
    cos32 = jnp.concatenate([jnp.cos(ang_r)] * 2 + [jnp.cos(ang_c)] * 2, axis=1)
    sin32 = jnp.concatenate([-jnp.sin(ang_r), jnp.sin(ang_r), -jnp.sin(ang_c), jnp.sin(ang_c)], axis=1)
    pad = LANES - MLA_ROPE
    cosm = jnp.concatenate([cos32, jnp.ones((n_pos, pad), F32)], axis=1)
    sinm = jnp.concatenate([sin32, jnp.zeros((n_pos, pad), F32)], axis=1)
    return jnp.tile(cos32, (1, 4)), jnp.tile(sin32, (1, 4)), cosm, sinm


def _prep_inproj(w, rope):
    o = np.cumsum([0, DA_WIDTH, DA_WIDTH, DA_WIDTH, MLA_Q_RANK, MLA_KV_RANK, MLA_ROPE, S5_WIDTH])
    wq, wk, wv, wcq, wckv, wkpe, wu = (w[:, o[i]:o[i + 1]] for i in range(7))
    kpe_pad = lambda m: jnp.pad(m, ((0, 0), (0, LANES - MLA_ROPE)))
    cols = [wq, wk, wv, wcq, wckv, wu, kpe_pad(wkpe)]
    if rope:
        half = DA_QK // 2
        cols += [_swap_halves(wq, half), _swap_halves(wk, half), kpe_pad(_swap_halves(wkpe, half))]
    return jnp.concatenate(cols, axis=1).astype(BF16)


def _prep_mla(q_norm_w, w_q_up, kv_norm_w, w_kv_up):
    pad = LANES - MLA_ROPE - MLA_NOPE
    wq = w_q_up.reshape(MLA_Q_RANK, MLA_HEADS, MLA_NOPE + MLA_ROPE)
    nope, rp = wq[..., :MLA_NOPE], wq[..., MLA_NOPE:]
    zeros = lambda n: jnp.zeros((MLA_Q_RANK, MLA_HEADS, n), F32)
    wqup = jnp.concatenate([rp, nope, zeros(pad)], axis=-1).reshape(MLA_Q_RANK, MLA_HEADS * LANES)
    wqsw = jnp.concatenate([_swap_halves(rp, DA_QK // 2), zeros(MLA_NOPE + pad)], axis=-1)
    wqsw = wqsw.reshape(MLA_Q_RANK, MLA_HEADS * LANES)
    wkv = w_kv_up.reshape(MLA_KV_RANK, MLA_HEADS, MLA_NOPE + MLA_V)
    zk = lambda n: jnp.zeros((MLA_KV_RANK, MLA_HEADS, n), F32)
    wk = jnp.concatenate([zk(MLA_ROPE), wkv[..., :MLA_NOPE], zk(pad)], axis=-1)
    wk = wk.reshape(MLA_KV_RANK, MLA_HEADS * LANES)
    wv = wkv[..., MLA_NOPE:].reshape(MLA_KV_RANK, MLA_WIDTH)
    return (q_norm_w.reshape(1, MLA_Q_RANK), kv_norm_w.reshape(1, MLA_KV_RANK),
            wqup.astype(BF16), wqsw.astype(BF16), wk.astype(BF16), wv.astype(BF16))


def _prep_mix(w_glu, b_glu, w_out):
    return (w_glu.astype(BF16), b_glu.reshape(1, S5_WIDTH),
            w_out[:DA_WIDTH].reshape(N_SLABS, LANES, D_MODEL).astype(BF16),
            w_out[DA_WIDTH:DA_WIDTH + MLA_WIDTH].reshape(N_SLABS, LANES, D_MODEL).astype(BF16),
            w_out[DA_WIDTH + MLA_WIDTH:].astype(BF16))


def _slabs_t(a, n_batch):
    b, n, _ = a.shape
    return a.reshape(b * n, N_SLABS, LANES).transpose(1, 2, 0).astype(BF16)


def _slabs(a):
    b, n, _ = a.shape
    return a.reshape(b * n, N_SLABS, LANES).transpose(1, 0, 2).astype(BF16)


def kernel(x_prompt, x_sample, cache_diff_k, cache_diff_v, cache_mla_ckv, cache_mla_kpe, state_s5_re, state_s5_im, c, c_ctx, w_ada, b_ada, norm_w, ffn_w_in, ffn_w_out, w_in, diff_lambda, diff_subln_w, mla_q_norm_w, mla_w_q_up, mla_kv_norm_w, mla_w_kv_up, s5_a_re, s5_a_im, s5_log_step, s5_b_re, s5_b_im, s5_c_re, s5_c_im, s5_d, s5_w_glu, s5_b_glu, w_out, final_norm_w):
    n_ctx, seq, _ = x_prompt.shape
    n_dec, dec_seq, _ = x_sample.shape
    past = cache_diff_k.shape[2]

    cvec = jnp.concatenate(
        [c_ctx[None, :], c, jnp.zeros((MOD_ROWS - 1 - n_dec, D_MODEL), F32)], axis=0)
    mod = _adaln(cvec, w_ada, b_ada)
    rope_tabs = _rope_tables(dec_seq)

    groups = {
        'ctx': dict(x=x_prompt.reshape(n_ctx * seq, D_MODEL), seg=n_ctx * seq, row=0, nb=n_ctx),
        'dec': dict(x=x_sample.reshape(n_dec * dec_seq, D_MODEL), seg=dec_seq, row=1, nb=n_dec),
    }
    ffn_in_b, ffn_out_b = ffn_w_in.astype(BF16), ffn_w_out.astype(BF16)
    caches = None
    s5_states = []
    for l in range(DEPTH):
        lam_init = 0.8 - 0.6 * math.exp(-0.3 * l)
        ffn1 = (l, 0, ffn_in_b, ffn_out_b)
        ffn2 = (l, 1, ffn_in_b, ffn_out_b)
        mla_w = _prep_mla(mla_q_norm_w[l], mla_w_q_up[l], mla_kv_norm_w[l], mla_w_kv_up[l])
        mix_w = _prep_mix(s5_w_glu[l], s5_b_glu[l], w_out[l])
        s5_ops = _s5_operators(s5_a_re[l], s5_a_im[l], s5_log_step[l], s5_b_re[l], s5_b_im[l],
                               s5_c_re[l], s5_c_im[l], s5_d[l])
        subw = jnp.tile(diff_subln_w[l], 2).reshape(1, LANES)
        final_w = final_norm_w if l == DEPTH - 1 else None
        for name, g in groups.items():
            dec = name == 'dec'
            x = _ffn_call(g['x'], mod[l], g['seg'], g['row'], norm_w[l, 0], ffn1, 0)
            proj = _inproj_call(x, mod[l], g['seg'], g['row'], norm_w[l, 1], _prep_inproj(w_in[l], dec),
                                mla_w, rope_tabs if dec else None,
                                cache_out=None if dec else (l, seq, caches))
            q, kt, v, qm, ktm, vm, xg = proj[:7]
            n_keys = g['x'].shape[0] // g['nb']
            parts = [(kt, v, ktm, vm, n_keys)]
            h0 = jnp.zeros((S5_GROUPS, 4, g['nb'], LANES), F32)
            if dec:
                ck = cache_diff_k[:, l].reshape(n_dec, past, DA_WIDTH)
                cv = cache_diff_v[:, l].reshape(n_dec, past, DA_WIDTH)
                kpe128 = jnp.pad(cache_mla_kpe[:, l].reshape(n_dec * past, MLA_ROPE),
                                 ((0, 0), (0, LANES - MLA_ROPE)))
                cktm, cvm = _mla_expand_call(cache_mla_ckv[:, l].reshape(n_dec * past, MLA_KV_RANK),
                                             kpe128, mla_w[4], mla_w[5], past)
                parts.append((_slabs_t(ck, n_dec), _slabs(cv), cktm, cvm, past))
                sr, si = state_s5_re[:, l], state_s5_im[:, l]
                h0 = jnp.stack([sr[:, 0], si[:, 0], sr[:, 1], si[:, 1]], 0).transpose(2, 0, 1, 3)
                h0 = jnp.pad(h0, ((0, 0), (0, 0), (0, 0), (0, LANES - S5_STATE)))
            o_a, o_b = _attn_call(q, qm, diff_lambda[l], subw, parts, g['nb'], lam_init)
            y_s5, fin = _s5_call(xg, *s5_ops, h0, g['nb'])
            g['x'] = _ffn_call(x, mod[l], g['seg'], g['row'], norm_w[l, 2], ffn2, 6,
                               mix_args=(o_a, o_b, y_s5), mix_w=mix_w, final_w=final_w)
            if not dec:
                caches = proj[7:]
                fin = fin[..., :S5_STATE].transpose(1, 2, 0, 3)
                s5_states.append((jnp.stack([fin[0], fin[2]], axis=1), jnp.stack([fin[1], fin[3]], axis=1)))

    y_prompt = groups['ctx']['x'].reshape(n_ctx, seq, D_MODEL)
    y_sample = groups['dec']['x'].reshape(n_dec, dec_seq, D_MODEL)
    new_k, new_v, new_ckv, new_kpe = caches
    new_s5_re = jnp.stack([t[0] for t in s5_states], axis=1)
    new_s5_im = jnp.stack([t[1] for t in s5_states], axis=1)
    return (y_prompt, y_sample,
            new_k.reshape(n_ctx, DEPTH, seq, DA_HEADS, 2 * DA_QK),
            new_v.reshape(n_ctx, DEPTH, seq, DA_HEADS, DA_V),
            new_ckv, new_kpe, new_s5_re, new_s5_im)
```

```python
import functools
import math

import jax
import jax.numpy as jnp
import numpy as np
from jax import lax
from jax.experimental import pallas as pl
from jax.experimental.pallas import tpu as pltpu

F32 = jnp.float32
BF16 = jnp.bfloat16

D_MODEL = 1024
DEPTH = 2
GRID_W = 64
ROPE_THETA = 10000.0
EPS = 1e-6
N_MOD = 9
D_FF = 2816
DA_HEADS = 6
DA_QK = 32
DA_V = 64
DA_WIDTH = DA_HEADS * DA_V
MLA_HEADS = 6
MLA_Q_RANK = 256
MLA_KV_RANK = 128
MLA_NOPE = 64
MLA_ROPE = 32
MLA_V = 64
MLA_WIDTH = MLA_HEADS * MLA_V
S5_WIDTH = D_MODEL - DA_WIDTH - MLA_WIDTH
S5_CH = 16
S5_GROUPS = S5_WIDTH // S5_CH
S5_STATE = 64

LANES = 128
FF_CHUNK = 256
N_FF_CHUNKS = D_FF // FF_CHUNK
TOKEN_TILE = 512
Q_TILE = 256
S5_CHUNK = 16
N_SLABS = DA_WIDTH // LANES
MOD_ROWS = 8
VMEM_LIMIT = 56 * 1024 * 1024


def _rms(x, w):
    return x * lax.rsqrt(jnp.mean(x * x, axis=-1, keepdims=True) + EPS) * w


def _silu(x):
    return x * jax.nn.sigmoid(x)


def _gelu_tanh(x):
    c = math.sqrt(2.0 / math.pi)
    return x * (0.5 * (1.0 + jnp.tanh(c * (x + 0.044715 * (x * x * x)))))


def _dot(a, b):
    return jnp.dot(a, b, preferred_element_type=F32)


def _resident(shape):
    nd = len(shape)
    return pl.BlockSpec(shape, lambda *_: (0,) * nd, pipeline_mode=pl.Buffered(1))


def _adaln_kernel(c_ref, w_ref, b_ref, o_ref):
    s = _silu(c_ref[...])
    o_ref[...] = jnp.dot(s, w_ref[...], preferred_element_type=F32,
                         precision=lax.Precision.HIGHEST) + b_ref[...]


def _adaln(cvec, w_ada, b_ada):
    n = N_MOD * D_MODEL
    tn = 1152
    out = pl.pallas_call(
        _adaln_kernel,
        grid=(DEPTH, n // tn),
        in_specs=[
            pl.BlockSpec((MOD_ROWS, D_MODEL), lambda l, j: (0, 0)),
            pl.BlockSpec((None, D_MODEL, tn), lambda l, j: (l, 0, j)),
            pl.BlockSpec((None, 1, tn), lambda l, j: (l, 0, j)),
        ],
        out_specs=pl.BlockSpec((None, MOD_ROWS, tn), lambda l, j: (l, 0, j)),
        out_shape=jax.ShapeDtypeStruct((DEPTH, MOD_ROWS, n), F32),
        compiler_params=pltpu.CompilerParams(
            dimension_semantics=("arbitrary", "arbitrary"), vmem_limit_bytes=VMEM_LIMIT),
        name="adaln",
    )(cvec, w_ada, b_ada.reshape(DEPTH, 1, n))
    return out.reshape(DEPTH, MOD_ROWS, N_MOD, D_MODEL)


_ROWS_PER_TILE = TOKEN_TILE // S5_CHUNK
_SLOTS = LANES // S5_CH


def _to_group_major(u_scr, xg_ref):
    slot = lax.broadcasted_iota(jnp.int32, (_ROWS_PER_TILE, LANES), 1) // S5_CH
    for g in range(S5_GROUPS):
        for tt in range(S5_CHUNK // _SLOTS):
            acc = None
            for j in range(_SLOTS):
                t = tt * _SLOTS + j
                src = u_scr[g // _SLOTS, pl.ds(t, _ROWS_PER_TILE, stride=S5_CHUNK), :]
                shift = ((j - g % _SLOTS) % _SLOTS) * S5_CH
                piece = pltpu.roll(src, shift, 1) if shift else src
                acc = piece if acc is None else jnp.where(slot == j, piece, acc)
            xg_ref[g, :, tt * LANES:(tt + 1) * LANES] = acc.astype(BF16)


def _from_group_major(yg_ref, y_scr):
    slot = lax.broadcasted_iota(jnp.int32, (_ROWS_PER_TILE, LANES), 1) // S5_CH
    for t in range(S5_CHUNK):
        for gc in range(S5_GROUPS // _SLOTS):
            acc = None
            for m in range(_SLOTS):
                src = yg_ref[gc * _SLOTS + m, :, (t // _SLOTS) * LANES:(t // _SLOTS + 1) * LANES]
                shift = ((m - t % _SLOTS) % _SLOTS) * S5_CH
                piece = pltpu.roll(src, shift, 1) if shift else src
                acc = piece if acc is None else jnp.where(slot == m, piece, acc)
            y_scr[gc, pl.ds(t, _ROWS_PER_TILE, stride=S5_CHUNK), :] = acc


def _ffn_kernel(mix, final, mod_base, *refs):
    it = iter(refs)
    x_ref, mod_ref, nw_ref, wi_ref, wo_ref = (next(it) for _ in range(5))
    if mix:
        oa_ref, ob_ref, ys_ref, wglu_ref, bglu_ref, wout_ref = (next(it) for _ in range(6))
    if final:
        fnw_ref = next(it)
    out_ref, h_scr, acc_scr = next(it), next(it), next(it)

    x = x_ref[...]
    if mix:
        y_scr = next(it)
        _from_group_major(ys_ref, y_scr)
        yc = _gelu_tanh(jnp.concatenate([y_scr[gc] for gc in range(S5_WIDTH // LANES)], axis=1))
        oc = yc * jax.nn.sigmoid(_dot(yc.astype(BF16), wglu_ref[...]) + bglu_ref[...])
        mixed = jnp.concatenate([oa_ref[s] for s in range(N_SLABS)] + [ob_ref[s] for s in range(N_SLABS)]
                                + [oc.astype(BF16)], axis=1)
        x = x + mod_ref[0, mod_base - 1:mod_base, :] * _dot(mixed, wout_ref[...])

    shift = mod_ref[0, mod_base:mod_base + 1, :]
    scale = mod_ref[0, mod_base + 1:mod_base + 2, :]
    gate = mod_ref[0, mod_base + 2:mod_base + 3, :]
    h = _rms(x, nw_ref[...]) * (1.0 + scale) + shift
    h_scr[...] = h.astype(BF16)
    acc_scr[...] = jnp.zeros_like(acc_scr)

    for j in range(N_FF_CHUNKS):
        hb = h_scr[...]
        col = j * FF_CHUNK
        a = _dot(hb, wi_ref[:, col:col + FF_CHUNK])
        g = _dot(hb, wi_ref[:, D_FF + col:D_FF + col + FF_CHUNK])
        acc_scr[...] += _dot((_silu(g) * a).astype(BF16), wo_ref[col:col + FF_CHUNK, :])
    y = x + 0.5 * gate * acc_scr[...]
    if final:
        y = _rms(y, fnw_ref[...])
    out_ref[...] = y


def _ffn_call(x, mod_l, seg_len, row_base, nw, ffn_w, mod_base, mix_args=None, mix_w=None, final_w=None):
    t = x.shape[0]
    tm = TOKEN_TILE
    mix = mix_args is not None
    final = final_w is not None
    layer, which, w_in_all, w_out_all = ffn_w
    tok = lambda i: (i, 0)
    in_specs = [
        pl.BlockSpec((tm, D_MODEL), tok),
        pl.BlockSpec((1, N_MOD, D_MODEL), lambda i: (row_base + (i * tm) // seg_len, 0, 0)),
        _resident((1, D_MODEL)),
        pl.BlockSpec((None, None, D_MODEL, 2 * D_FF), lambda i: (layer, which, 0, 0), pipeline_mode=pl.Buffered(1)),
        pl.BlockSpec((None, None, D_FF, D_MODEL), lambda i: (layer, which, 0, 0), pipeline_mode=pl.Buffered(1)),
    ]
    args = [x, mod_l, nw.reshape(1, D_MODEL), w_in_all, w_out_all]
    if mix:
        in_specs += [
            pl.BlockSpec((N_SLABS, tm, LANES), lambda i: (0, i, 0)),
            pl.BlockSpec((N_SLABS, tm, LANES), lambda i: (0, i, 0)),
            pl.BlockSpec((S5_GROUPS, _ROWS_PER_TILE, S5_CHUNK * S5_CH), lambda i: (0, i, 0)),
            _resident((S5_WIDTH, S5_WIDTH)),
            _resident((1, S5_WIDTH)),
            _resident((D_MODEL, D_MODEL)),
        ]
        args += [*mix_args, *mix_w]
    if final:
        in_specs.append(_resident((1, D_MODEL)))
        args.append(final_w.reshape(1, D_MODEL))
    return pl.pallas_call(
        functools.partial(_ffn_kernel, mix, final, mod_base),
        grid=(t // tm,),
        in_specs=in_specs,
        out_specs=pl.BlockSpec((tm, D_MODEL), tok),
        out_shape=jax.ShapeDtypeStruct((t, D_MODEL), F32),
        scratch_shapes=[pltpu.VMEM((tm, D_MODEL), BF16), pltpu.VMEM((tm, D_MODEL), F32)]
        + ([pltpu.VMEM((S5_WIDTH // LANES, tm, LANES), F32)] if mix else []),
        compiler_params=pltpu.CompilerParams(
            dimension_semantics=("arbitrary",), vmem_limit_bytes=VMEM_LIMIT),
        name="mix_ffn" if mix else "ffn",
    )(*args)


_C_Q, _C_K, _C_V = 0, DA_WIDTH, 2 * DA_WIDTH
_C_CQ = 3 * DA_WIDTH
_C_CKV = _C_CQ + MLA_Q_RANK
_C_U = _C_CKV + MLA_KV_RANK
_C_KPE = _C_U + S5_WIDTH
_C_QSW = _C_KPE + LANES
_C_KSW = _C_QSW + DA_WIDTH
_C_KPESW = _C_KSW + DA_WIDTH
_NCOL_CTX = _C_QSW
_NCOL_DEC = _C_KPESW + LANES

_QSCALE_DIFF = DA_QK ** -0.5 * math.log2(math.e)
_QSCALE_MLA = (MLA_NOPE + MLA_ROPE) ** -0.5 * math.log2(math.e)


def _inproj_kernel(rope, n_aliased, *refs):
    it = iter(refs)
    x_ref, mod_ref, nw_ref, w_ref, qnw_ref, kvnw_ref, wqup_ref = (next(it) for _ in range(7))
    if rope:
        wqsw_ref = next(it)
    wk_ref, wv_ref = next(it), next(it)
    if rope:
        cos_ref, sin_ref, cosm_ref, sinm_ref = (next(it) for _ in range(4))
    for _ in range(n_aliased):
        next(it)
    q_ref, kt_ref, v_ref, qm_ref, ktm_ref, vm_ref, xg_ref = (next(it) for _ in range(7))
    if not rope:
        ka_ref, va_ref, ckv_ref, kpe_ref = (next(it) for _ in range(4))
    u_scr = next(it)

    h = _rms(x_ref[...], nw_ref[...]) * (1.0 + mod_ref[0, 4:5, :]) + mod_ref[0, 3:4, :]
    big = _dot(h.astype(BF16), w_ref[...])

    def cols(start, width):
        return big[:, start:start + width]

    if rope:
        cos, sin, cosm, sinm = cos_ref[...], sin_ref[...], cosm_ref[...], sinm_ref[...]
    for s in range(N_SLABS):
        qs = cols(_C_Q + s * LANES, LANES)
        ks = cols(_C_K + s * LANES, LANES)
        if rope:
            qs = qs * cos + cols(_C_QSW + s * LANES, LANES) * sin
            ks = ks * cos + cols(_C_KSW + s * LANES, LANES) * sin
        q_ref[s] = (qs * _QSCALE_DIFF).astype(BF16)
        kt_ref[s] = ks.T.astype(BF16)
        v_ref[s] = cols(_C_V + s * LANES, LANES).astype(BF16)
    for gc in range(S5_WIDTH // LANES):
        u_scr[gc] = cols(_C_U + gc * LANES, LANES)
    _to_group_major(u_scr, xg_ref)

    cqn = _rms(cols(_C_CQ, MLA_Q_RANK), qnw_ref[...]).astype(BF16)
    qm = _dot(cqn, wqup_ref[...])
    if rope:
        qsw = _dot(cqn, wqsw_ref[...])
    ckvn = _rms(cols(_C_CKV, MLA_KV_RANK), kvnw_ref[...])
    ckvb = ckvn.astype(BF16)
    kn = _dot(ckvb, wk_ref[...])
    vm = _dot(ckvb, wv_ref[...])
    kpe = cols(_C_KPE, LANES)
    if rope:
        kpe = kpe * cosm + cols(_C_KPESW, LANES) * sinm
    for hd in range(MLA_HEADS):
        qh = qm[:, hd * LANES:(hd + 1) * LANES]
        if rope:
            qh = qh * cosm + qsw[:, hd * LANES:(hd + 1) * LANES] * sinm
        qm_ref[hd] = (qh * _QSCALE_MLA).astype(BF16)
        ktm_ref[hd] = (kn[:, hd * LANES:(hd + 1) * LANES] + kpe).T.astype(BF16)
    for s in range(N_SLABS):
        vm_ref[s] = vm[:, s * LANES:(s + 1) * LANES].astype(BF16)

    if not rope:
        n_seq, seq = ka_ref.shape[:2]
        for b in range(n_seq):
            rows = slice(b * seq, (b + 1) * seq)
            ka_ref[b] = cols(_C_K, DA_WIDTH)[rows]
            va_ref[b] = cols(_C_V, DA_WIDTH)[rows]
            ckv_ref[b] = ckvn[rows]
            kpe_ref[b] = cols(_C_KPE, MLA_ROPE)[rows]


def _inproj_call(x, mod_l, seg_len, row_base, nw, w, mla_w, rope_tabs, cache_out=None):
    t = x.shape[0]
    tm = TOKEN_TILE
    rope = rope_tabs is not None
    ncol = _NCOL_DEC if rope else _NCOL_CTX
    qnw, kvnw, wqup, wqsw, wk, wv = mla_w
    tok = lambda i: (i, 0)
    slab = lambda i: (0, i, 0)
    slab_t = lambda i: (0, 0, i)
    in_specs = [
        pl.BlockSpec((tm, D_MODEL), tok),
        pl.BlockSpec((1, N_MOD, D_MODEL), lambda i: (row_base + (i * tm) // seg_len, 0, 0)),
        _resident((1, D_MODEL)),
        _resident((D_MODEL, ncol)),
        _resident((1, MLA_Q_RANK)),
        _resident((1, MLA_KV_RANK)),
        _resident((MLA_Q_RANK, MLA_HEADS * LANES)),
    ]
    args = [x, mod_l, nw.reshape(1, D_MODEL), w, qnw, kvnw, wqup]
    if rope:
        in_specs.append(_resident((MLA_Q_RANK, MLA_HEADS * LANES)))
        args.append(wqsw)
    in_specs += [_resident((MLA_KV_RANK, MLA_HEADS * LANES)), _resident((MLA_KV_RANK, MLA_WIDTH))]
    args += [wk, wv]
    if rope:
        n_pos = rope_tabs[0].shape[0]
        pos = lambda i: (i % (n_pos // tm), 0)
        in_specs += [pl.BlockSpec((tm, LANES), pos)] * 4
        args += list(rope_tabs)
    out_shape = [
        jax.ShapeDtypeStruct((N_SLABS, t, LANES), BF16),
        jax.ShapeDtypeStruct((N_SLABS, LANES, t), BF16),
        jax.ShapeDtypeStruct((N_SLABS, t, LANES), BF16),
        jax.ShapeDtypeStruct((MLA_HEADS, t, LANES), BF16),
        jax.ShapeDtypeStruct((MLA_HEADS, LANES, t), BF16),
        jax.ShapeDtypeStruct((N_SLABS, t, LANES), BF16),
        jax.ShapeDtypeStruct((S5_GROUPS, t // S5_CHUNK, S5_CHUNK * S5_CH), BF16),
    ]
    out_specs = [
        pl.BlockSpec((N_SLABS, tm, LANES), slab),
        pl.BlockSpec((N_SLABS, LANES, tm), slab_t),
        pl.BlockSpec((N_SLABS, tm, LANES), slab),
        pl.BlockSpec((MLA_HEADS, tm, LANES), slab),
        pl.BlockSpec((MLA_HEADS, LANES, tm), slab_t),
        pl.BlockSpec((N_SLABS, tm, LANES), slab),
        pl.BlockSpec((S5_GROUPS, _ROWS_PER_TILE, S5_CHUNK * S5_CH), slab),
    ]
    aliases = {}
    if not rope:
        layer, seq, stacked = cache_out
        n_first = len(out_shape)
        for k, arr in enumerate(stacked):
            out_shape.append(jax.ShapeDtypeStruct(arr.shape, F32))
            out_specs.append(pl.BlockSpec((tm // seq, None, seq, arr.shape[-1]), lambda i: (i, layer, 0, 0)))
            aliases[len(args)] = n_first + k
            in_specs.append(pl.BlockSpec(memory_space=pl.ANY))
            args.append(arr)
    return pl.pallas_call(
        functools.partial(_inproj_kernel, rope, len(aliases)),
        grid=(t // tm,),
        in_specs=in_specs,
        out_specs=out_specs,
        out_shape=out_shape,
        input_output_aliases=aliases,
        scratch_shapes=[pltpu.VMEM((S5_WIDTH // LANES, tm, LANES), F32)],
        compiler_params=pltpu.CompilerParams(
            dimension_semantics=("arbitrary",), vmem_limit_bytes=VMEM_LIMIT),
        name="inproj_dec" if rope else "inproj_ctx",
    )(*args)


def _mla_expand_kernel(ckv_ref, kpe_ref, wk_ref, wv_ref, ktm_ref, vm_ref):
    ckvb = ckv_ref[...].astype(BF16)
    kn = _dot(ckvb, wk_ref[...])
    vm = _dot(ckvb, wv_ref[...])
    kpe = kpe_ref[...]
    for hd in range(MLA_HEADS):
        ktm_ref[hd] = (kn[:, hd * LANES:(hd + 1) * LANES] + kpe).T.astype(BF16)
    for s in range(N_SLABS):
        vm_ref[s] = vm[:, s * LANES:(s + 1) * LANES].astype(BF16)


def _mla_expand_call(ckv, kpe128, wk, wv, rows):
    t = ckv.shape[0]
    return pl.pallas_call(
        _mla_expand_kernel,
        grid=(t // rows,),
        in_specs=[
            pl.BlockSpec((rows, MLA_KV_RANK), lambda i: (i, 0)),
            pl.BlockSpec((rows, LANES), lambda i: (i, 0)),
            _resident((MLA_KV_RANK, MLA_HEADS * LANES)),
            _resident((MLA_KV_RANK, MLA_WIDTH)),
        ],
        out_specs=[
            pl.BlockSpec((MLA_HEADS, LANES, rows), lambda i: (0, 0, i)),
            pl.BlockSpec((N_SLABS, rows, LANES), lambda i: (0, i, 0)),
        ],
        out_shape=[
            jax.ShapeDtypeStruct((MLA_HEADS, LANES, t), BF16),
            jax.ShapeDtypeStruct((N_SLABS, t, LANES), BF16),
        ],
        compiler_params=pltpu.CompilerParams(dimension_semantics=("arbitrary",)),
        name="mla_expand",
    )(ckv, kpe128, wk, wv)


KEY_TILE = 256
N_SCORE_BUFS = 4


def _interleave(stages):
    stages = list(stages)
    while stages:
        for g in list(stages):
            try:
                next(g)
            except StopIteration:
                stages.remove(g)


def _attn_kernel(n_parts, lam_init, *refs):
    q_ref, qm_ref, lamp_ref, subw_ref = refs[:4]
    parts = [refs[4 + 4 * i:8 + 4 * i] for i in range(n_parts)]
    oa_ref, ob_ref, sbuf = refs[4 + 4 * n_parts:]

    lp = lamp_ref[...]
    lam = (jnp.exp(jnp.sum(lp[0:1] * lp[1:2], keepdims=True))
           - jnp.exp(jnp.sum(lp[2:3] * lp[3:4], keepdims=True)) + lam_init)
    tq = q_ref.shape[1]
    lane = lax.broadcasted_iota(jnp.int32, (tq, LANES), 1)
    low_half = lane < DA_V
    subw = subw_ref[...] * (1.0 - lam_init)

    tiles, col = [], 0
    for p, (kt, _, _, _) in enumerate(parts):
        for k0 in range(0, kt.shape[2], KEY_TILE):
            tiles.append((p, k0, col))
            col += KEY_TILE

    def lane_fold(op, acc, x):
        for j in range(x.shape[1] // LANES):
            piece = x[:, j * LANES:(j + 1) * LANES]
            acc = piece if acc is None else op(acc, piece)
        return acc

    def scores(unit):
        buf = unit['buf']
        if unit['mla']:
            lhs, kref = qm_ref[unit['head']], 2
            kidx = unit['head']
        else:
            qs = q_ref[unit['slab']]
            lhs, kref = jnp.where(lane // DA_QK == unit['sub'], qs, jnp.zeros_like(qs)), 0
            kidx = unit['slab']
        mx = None
        for p, k0, c0 in tiles:
            s = _dot(lhs, parts[p][kref][kidx, :, k0:k0 + KEY_TILE])
            sbuf[buf, :, c0:c0 + KEY_TILE] = s
            mx = lane_fold(jnp.maximum, mx, s)
            yield
        unit['max'] = jnp.max(mx, axis=1, keepdims=True)

    def expo(unit):
        buf, tot = unit['buf'], None
        for _, _, c0 in tiles:
            e = jnp.exp2(sbuf[buf, :, c0:c0 + KEY_TILE] - unit['max'])
            sbuf[buf, :, c0:c0 + KEY_TILE] = e
            tot = lane_fold(jnp.add, tot, e)
            yield
        unit['sum'] = jnp.sum(tot, axis=1, keepdims=True)

    def pv(head):
        units, slab = head['units'], head['slab']
        vref = 3 if units[0]['mla'] else 1
        l0 = units[0]['sum']
        if len(units) == 2:
            r = lam * l0 / units[1]['sum']
        acc = None
        for p, k0, c0 in tiles:
            w = sbuf[units[0]['buf'], :, c0:c0 + KEY_TILE]
            if len(units) == 2:
                w = w - r * sbuf[units[1]['buf'], :, c0:c0 + KEY_TILE]
            o = _dot(w.astype(BF16), parts[p][vref][slab, k0:k0 + KEY_TILE, :])
            acc = o if acc is None else acc + o
            yield
        head['out'] = acc * (1.0 / l0)

    def finish(slab, mla, lo, hi):
        o = jnp.where(low_half, lo['out'], hi['out'])
        if mla:
            ob_ref[slab] = o.astype(BF16)
            return
        o2 = o * o
        ss_lo = jnp.sum(jnp.where(low_half, o2, 0.0), axis=1, keepdims=True)
        ss_hi = jnp.sum(jnp.where(low_half, 0.0, o2), axis=1, keepdims=True)
        ms = jnp.where(low_half, ss_lo, ss_hi) * (1.0 / DA_V)
        oa_ref[slab] = (o * lax.rsqrt(ms + EPS) * subw).astype(BF16)

    units, heads = [], []
    for slab in range(N_SLABS):
        for mla in (False, True):
            for half in range(2):
                mine = [dict(mla=mla, slab=slab, head=2 * slab + half, sub=2 * half + c)
                        for c in range(1 if mla else 2)]
                heads.append(dict(units=mine, slab=slab, mla=mla, half=half, last=len(units) + len(mine) - 1))
                units += mine
    for k, unit in enumerate(units):
        unit['buf'] = k % N_SCORE_BUFS

    pending = None
    for k in range(len(units) + 2):
        stages = []
        if k < len(units):
            stages.append(scores(units[k]))
        if 1 <= k <= len(units):
            stages.append(expo(units[k - 1]))
        ready = [h for h in heads if h['last'] == k - 2]
        if ready:
            stages.append(pv(ready[0]))
        _interleave(stages)
        if ready and ready[0]['half'] == 1:
            finish(ready[0]['slab'], ready[0]['mla'], pending[(ready[0]['slab'], ready[0]['mla'])], ready[0])
        elif ready:
            pending = dict(pending or {})
            pending[(ready[0]['slab'], ready[0]['mla'])] = ready[0]


def _attn_call(q, qm, lamp, subw, parts, n_batch, lam_init):
    t = q.shape[1]
    seq = t // n_batch
    tq = min(Q_TILE, seq)
    nq = seq // tq
    qmap = lambda b, i: (0, b * nq + i, 0)
    in_specs = [
        pl.BlockSpec((N_SLABS, tq, LANES), qmap),
        pl.BlockSpec((MLA_HEADS, tq, LANES), qmap),
        pl.BlockSpec((4, DA_QK), lambda b, i: (0, 0)),
        pl.BlockSpec((1, LANES), lambda b, i: (0, 0)),
    ]
    args = [q, qm, lamp, subw]
    for kt, v, ktm, vm, n_keys in parts:
        in_specs += [
            pl.BlockSpec((N_SLABS, LANES, n_keys), lambda b, i: (0, 0, b)),
            pl.BlockSpec((N_SLABS, n_keys, LANES), lambda b, i: (0, b, 0)),
            pl.BlockSpec((MLA_HEADS, LANES, n_keys), lambda b, i: (0, 0, b)),
            pl.BlockSpec((N_SLABS, n_keys, LANES), lambda b, i: (0, b, 0)),
        ]
        args += [kt, v, ktm, vm]
    return pl.pallas_call(
        functools.partial(_attn_kernel, len(parts), lam_init),
        grid=(n_batch, nq),
        in_specs=in_specs,
        out_specs=[pl.BlockSpec((N_SLABS, tq, LANES), qmap)] * 2,
        out_shape=[jax.ShapeDtypeStruct((N_SLABS, t, LANES), BF16)] * 2,
        scratch_shapes=[pltpu.VMEM((N_SCORE_BUFS, tq, sum(p[4] for p in parts)), F32)],
        compiler_params=pltpu.CompilerParams(
            dimension_semantics=("arbitrary", "arbitrary"), vmem_limit_bytes=VMEM_LIMIT),
        name="attn_dec" if len(parts) > 1 else "attn_ctx",
    )(*args)


def _s5_kernel(n_batch, n_chunks, x_ref, kmat_ref, bst_ref, cst_ref, lamt_ref, h0_ref, y_ref, fin_ref,
               s_scr, hin_scr):
    x = x_ref[...]
    y = _dot(x, kmat_ref[...])
    for k in range(4):
        s_scr[k] = _dot(x, bst_ref[k])

    a_fr, a_fi, a_br, a_bi = (lamt_ref[k] for k in range(4))

    def chunk_rows(j):
        return pl.ds(j, n_batch, stride=n_chunks)

    def step(j, carry):
        fr, fi, br, bi = carry
        rf, rb = chunk_rows(j), chunk_rows(n_chunks - 1 - j)
        hin_scr[0, rf, :] = fr
        hin_scr[1, rf, :] = fi
        hin_scr[2, rb, :] = br
        hin_scr[3, rb, :] = bi
        nfr = a_fr * fr - a_fi * fi + s_scr[0, rf, :]
        nfi = a_fr * fi + a_fi * fr + s_scr[1, rf, :]
        nbr = a_br * br - a_bi * bi + s_scr[2, rb, :]
        nbi = a_br * bi + a_bi * br + s_scr[3, rb, :]
        return nfr, nfi, nbr, nbi

    fin = lax.fori_loop(0, n_chunks, step, tuple(h0_ref[k] for k in range(4)))
    for k in range(4):
        fin_ref[k] = fin[k]
        y += _dot(hin_scr[k].astype(BF16), cst_ref[k])
    y_ref[...] = y


def _s5_call(xg, kmat, bst, cst, lamt, h0, n_batch):
    rows = xg.shape[1]
    n_chunks = rows // n_batch
    wide = S5_CHUNK * S5_CH
    grp = lambda g: (g, 0, 0)
    grp4 = lambda g: (g, 0, 0, 0)
    return pl.pallas_call(
        functools.partial(_s5_kernel, n_batch, n_chunks),
        grid=(S5_GROUPS,),
        in_specs=[
            pl.BlockSpec((None, rows, wide), grp),
            pl.BlockSpec((None, wide, wide), grp),
            pl.BlockSpec((None, 4, wide, LANES), grp4),
            pl.BlockSpec((None, 4, LANES, wide), grp4),
            pl.BlockSpec((None, 4, 1, LANES), grp4),
            pl.BlockSpec((None, 4, n_batch, LANES), grp4),
        ],
        out_specs=[
            pl.BlockSpec((None, rows, wide), grp),
            pl.BlockSpec((None, 4, n_batch, LANES), grp4),
        ],
        out_shape=[
            jax.ShapeDtypeStruct((S5_GROUPS, rows, wide), F32),
            jax.ShapeDtypeStruct((S5_GROUPS, 4, n_batch, LANES), F32),
        ],
        scratch_shapes=[pltpu.VMEM((4, rows, LANES), F32), pltpu.VMEM((4, rows, LANES), F32)],
        compiler_params=pltpu.CompilerParams(dimension_semantics=("arbitrary",)),
        name="s5",
    )(xg, kmat, bst, cst, lamt, h0)


def _s5_operators(a_re, a_im, log_step, b_re, b_im, c_re, c_im, d):
    hi = lax.Precision.HIGHEST
    n = S5_CHUNK
    wide = n * S5_CH
    ar, ai = a_re.astype(F32), a_im.astype(F32)
    step = jnp.exp(log_step.astype(F32))[..., None]
    zr, zi = ar * step, ai * step
    k = jnp.arange(n + 1, dtype=F32)[:, None, None, None]
    mag = jnp.exp(k * zr[None])
    pr, pi = mag * jnp.cos(k * zi[None]), mag * jnp.sin(k * zi[None])

    xr, xi = pr[1] - 1.0, pi[1]
    den = ar * ar + ai * ai
    fr, fi = ((xr * ar + xi * ai) / den)[:, :, None, :], ((xi * ar - xr * ai) / den)[:, :, None, :]
    br, bi = (jnp.swapaxes(b.astype(F32), 2, 3) for b in (b_re, b_im))
    bbr, bbi = fr * br - fi * bi, fr * bi + fi * br
    cr, ci = c_re.astype(F32), c_im.astype(F32)

    tt = np.arange(n)

    def state_readout(powers, dirn):
        qr = pr[powers, dirn].transpose(1, 2, 0)[:, :, :, None]
        qi = pi[powers, dirn].transpose(1, 2, 0)[:, :, :, None]
        ctr = jnp.swapaxes(cr[dirn], 1, 2)[:, :, None, :]
        cti = jnp.swapaxes(ci[dirn], 1, 2)[:, :, None, :]
        rr = ctr * qr - cti * qi
        ri = -(ctr * qi + cti * qr)
        return rr.reshape(S5_GROUPS, S5_STATE, wide), ri.reshape(S5_GROUPS, S5_STATE, wide)

    def lag_response(powers, dirn):
        rr, ri = state_readout(powers, dirn)
        return (jnp.einsum('gkp,gpn->gkn', bbr[dirn], rr, precision=hi)
                + jnp.einsum('gkp,gpn->gkn', bbi[dirn], ri, precision=hi))

    fwd, bwd = lag_response(tt, 0), lag_response(n - 1 - tt, 1)
    skip = d.astype(F32)[:, None, :] * jnp.eye(S5_CH, dtype=F32)[None]
    strip = jnp.concatenate([bwd[:, :, :wide - S5_CH], bwd[:, :, wide - S5_CH:] + fwd[:, :, :S5_CH] + skip,
                             fwd[:, :, S5_CH:]], axis=-1)
    kmat = jnp.stack([strip[:, :, (n - 1 - s) * S5_CH:(n - 1 - s) * S5_CH + wide] for s in range(n)], axis=1)
    kmat = kmat.reshape(S5_GROUPS, wide, wide)

    def edge_states(powers, dirn):
        qr = jnp.swapaxes(pr[powers, dirn], 0, 1)[:, :, None, :]
        qi = jnp.swapaxes(pi[powers, dirn], 0, 1)[:, :, None, :]
        er = qr * bbr[dirn][:, None] - qi * bbi[dirn][:, None]
        ei = qr * bbi[dirn][:, None] + qi * bbr[dirn][:, None]
        return er.reshape(S5_GROUPS, wide, S5_STATE), ei.reshape(S5_GROUPS, wide, S5_STATE)

    bst = jnp.stack([*edge_states(n - 1 - tt, 0), *edge_states(tt, 1)], axis=1)
    cst = jnp.stack([*state_readout(tt + 1, 0), *state_readout(n - tt, 1)], axis=1)
    lamt = jnp.stack([pr[n, 0], pi[n, 0], pr[n, 1], pi[n, 1]], axis=1)
    zpad = LANES - S5_STATE
    bst = jnp.pad(bst, ((0, 0), (0, 0), (0, 0), (0, zpad)))
    cst = jnp.pad(cst, ((0, 0), (0, 0), (0, zpad), (0, 0)))
    lamt = jnp.pad(lamt, ((0, 0), (0, 0), (0, zpad)))
    return kmat.astype(BF16), bst.astype(BF16), cst.astype(BF16), lamt[:, :, None, :]


def _swap_halves(w, unit):
    n = w.shape[-1]
    idx = np.arange(n)
    half = unit // 2
    return w[..., np.where(idx % unit < half, idx + half, idx - half)]


def _rope_tables(n_pos):
    n = DA_QK // 4
    inv = ROPE_THETA ** (-jnp.arange(n, dtype=F32) / n)
    pos = jnp.arange(n_pos, dtype=jnp.int32)
    ang_r = (pos // GRID_W).astype(F32)[:, None] * inv[None, :]
    ang_c = (pos % GRID_W).astype(F32)[:, None] * inv[None, :]
    cos32 = jnp.concatenate([jnp.cos(ang_r)] * 2 + [jnp.cos(ang_c)] * 2, axis=1)
    sin32 = jnp.concatenate([-jnp.sin(ang_r), jnp.sin(ang_r), -jnp.sin(ang_c), jnp.sin(ang_c)], axis=1)
    pad = LANES - MLA_ROPE
    cosm = jnp.concatenate([cos32, jnp.ones((n_pos, pad), F32)], axis=1)
    sinm = jnp.concatenate([sin32, jnp.zeros((n_pos, pad), F32)], axis=1)
    return jnp.tile(cos32, (1, 4)), jnp.tile(sin32, (1, 4)), cosm, sinm


def _prep_inproj(w, rope):
    o = np.cumsum([0, DA_WIDTH, DA_WIDTH, DA_WIDTH, MLA_Q_RANK, MLA_KV_RANK, MLA_ROPE, S5_WIDTH])
    wq, wk, wv, wcq, wckv, wkpe, wu = (w[:, o[i]:o[i + 1]] for i in range(7))
    kpe_pad = lambda m: jnp.pad(m, ((0, 0), (0, LANES - MLA_ROPE)))
    cols = [wq, wk, wv, wcq, wckv, wu, kpe_pad(wkpe)]
    if rope:
        half = DA_QK // 2
        cols += [_swap_halves(wq, half), _swap_halves(wk, half), kpe_pad(_swap_halves(wkpe, half))]
    return jnp.concatenate(cols, axis=1).astype(BF16)


def _prep_mla(q_norm_w, w_q_up, kv_norm_w, w_kv_up):
    pad = LANES - MLA_ROPE - MLA_NOPE
    wq = w_q_up.reshape(MLA_Q_RANK, MLA_HEADS, MLA_NOPE + MLA_ROPE)
    nope, rp = wq[..., :MLA_NOPE], wq[..., MLA_NOPE:]
    zeros = lambda n: jnp.zeros((MLA_Q_RANK, MLA_HEADS, n), F32)
    wqup = jnp.concatenate([rp, nope, zeros(pad)], axis=-1).reshape(MLA_Q_RANK, MLA_HEADS * LANES)
    wqsw = jnp.concatenate([_swap_halves(rp, DA_QK // 2), zeros(MLA_NOPE + pad)], axis=-1)
    wqsw = wqsw.reshape(MLA_Q_RANK, MLA_HEADS * LANES)
    wkv = w_kv_up.reshape(MLA_KV_RANK, MLA_HEADS, MLA_NOPE + MLA_V)
    zk = lambda n: jnp.zeros((MLA_KV_RANK, MLA_HEADS, n), F32)
    wk = jnp.concatenate([zk(MLA_ROPE), wkv[..., :MLA_NOPE], zk(pad)], axis=-1)
    wk = wk.reshape(MLA_KV_RANK, MLA_HEADS * LANES)
    wv = wkv[..., MLA_NOPE:].reshape(MLA_KV_RANK, MLA_WIDTH)
    return (q_norm_w.reshape(1, MLA_Q_RANK), kv_norm_w.reshape(1, MLA_KV_RANK),
            wqup.astype(BF16), wqsw.astype(BF16), wk.astype(BF16), wv.astype(BF16))


def _prep_mix(w_glu, b_glu, w_out):
    return w_glu.astype(BF16), b_glu.reshape(1, S5_WIDTH), w_out.astype(BF16)


def _slabs_t(a, n_batch):
    b, n, _ = a.shape
    return a.reshape(b * n, N_SLABS, LANES).transpose(1, 2, 0).astype(BF16)


def _slabs(a):
    b, n, _ = a.shape
    return a.reshape(b * n, N_SLABS, LANES).transpose(1, 0, 2).astype(BF16)


def kernel(x_prompt, x_sample, cache_diff_k, cache_diff_v, cache_mla_ckv, cache_mla_kpe, state_s5_re, state_s5_im, c, c_ctx, w_ada, b_ada, norm_w, ffn_w_in, ffn_w_out, w_in, diff_lambda, diff_subln_w, mla_q_norm_w, mla_w_q_up, mla_kv_norm_w, mla_w_kv_up, s5_a_re, s5_a_im, s5_log_step, s5_b_re, s5_b_im, s5_c_re, s5_c_im, s5_d, s5_w_glu, s5_b_glu, w_out, final_norm_w):
    n_ctx, seq, _ = x_prompt.shape
    n_dec, dec_seq, _ = x_sample.shape
    past = cache_diff_k.shape[2]

    cvec = jnp.concatenate(
        [c_ctx[None, :], c, jnp.zeros((MOD_ROWS - 1 - n_dec, D_MODEL), F32)], axis=0)
    mod = _adaln(cvec, w_ada, b_ada)
    rope_tabs = _rope_tables(dec_seq)

    groups = {
        'ctx': dict(x=x_prompt.reshape(n_ctx * seq, D_MODEL), seg=n_ctx * seq, row=0, nb=n_ctx),
        'dec': dict(x=x_sample.reshape(n_dec * dec_seq, D_MODEL), seg=dec_seq, row=1, nb=n_dec),
    }
    ffn_in_b, ffn_out_b = ffn_w_in.astype(BF16), ffn_w_out.astype(BF16)
    caches = tuple(jnp.zeros((n_ctx, DEPTH, seq, width), F32)
                   for width in (DA_WIDTH, DA_WIDTH, MLA_KV_RANK, MLA_ROPE))
    s5_states = []
    for l in range(DEPTH):
        lam_init = 0.8 - 0.6 * math.exp(-0.3 * l)
        ffn1 = (l, 0, ffn_in_b, ffn_out_b)
        ffn2 = (l, 1, ffn_in_b, ffn_out_b)
        mla_w = _prep_mla(mla_q_norm_w[l], mla_w_q_up[l], mla_kv_norm_w[l], mla_w_kv_up[l])
        mix_w = _prep_mix(s5_w_glu[l], s5_b_glu[l], w_out[l])
        s5_ops = _s5_operators(s5_a_re[l], s5_a_im[l], s5_log_step[l], s5_b_re[l], s5_b_im[l],
                               s5_c_re[l], s5_c_im[l], s5_d[l])
        subw = jnp.tile(diff_subln_w[l], 2).reshape(1, LANES)
        final_w = final_norm_w if l == DEPTH - 1 else None
        for name, g in groups.items():
            dec = name == 'dec'
            x = _ffn_call(g['x'], mod[l], g['seg'], g['row'], norm_w[l, 0], ffn1, 0)
            proj = _inproj_call(x, mod[l], g['seg'], g['row'], norm_w[l, 1], _prep_inproj(w_in[l], dec),
                                mla_w, rope_tabs if dec else None,
                                cache_out=None if dec else (l, seq, caches))
            q, kt, v, qm, ktm, vm, xg = proj[:7]
            n_keys = g['x'].shape[0] // g['nb']
            parts = [(kt, v, ktm, vm, n_keys)]
            h0 = jnp.zeros((S5_GROUPS, 4, g['nb'], LANES), F32)
            if dec:
                ck = cache_diff_k[:, l].reshape(n_dec, past, DA_WIDTH)
                cv = cache_diff_v[:, l].reshape(n_dec, past, DA_WIDTH)
                kpe128 = jnp.pad(cache_mla_kpe[:, l].reshape(n_dec * past, MLA_ROPE),
                                 ((0, 0), (0, LANES - MLA_ROPE)))
                cktm, cvm = _mla_expand_call(cache_mla_ckv[:, l].reshape(n_dec * past, MLA_KV_RANK),
                                             kpe128, mla_w[4], mla_w[5], past)
                parts.append((_slabs_t(ck, n_dec), _slabs(cv), cktm, cvm, past))
                sr, si = state_s5_re[:, l], state_s5_im[:, l]
                h0 = jnp.stack([sr[:, 0], si[:, 0], sr[:, 1], si[:, 1]], 0).transpose(2, 0, 1, 3)
                h0 = jnp.pad(h0, ((0, 0), (0, 0), (0, 0), (0, LANES - S5_STATE)))
            o_a, o_b = _attn_call(q, qm, diff_lambda[l], subw, parts, g['nb'], lam_init)
            y_s5, fin = _s5_call(xg, *s5_ops, h0, g['nb'])
            g['x'] = _ffn_call(x, mod[l], g['seg'], g['row'], norm_w[l, 2], ffn2, 6,
                               mix_args=(o_a, o_b, y_s5), mix_w=mix_w, final_w=final_w)
            if not dec:
                caches = proj[7:]
                fin = fin[..., :S5_STATE].transpose(1, 2, 0, 3)
                s5_states.append((jnp.stack([fin[0], fin[2]], axis=1), jnp.stack([fin[1], fin[3]], axis=1)))

    y_prompt = groups['ctx']['x'].reshape(n_ctx, seq, D_MODEL)
    y_sample = groups['dec']['x'].reshape(n_dec, dec_seq, D_MODEL)
    new_k, new_v, new_ckv, new_kpe = caches
    new_s5_re = jnp.stack([t[0] for t in s5_states], axis=1)
    new_s5_im = jnp.stack([t[1] for t in s5_states], axis=1)
    return (y_prompt, y_sample,
            new_k.reshape(n_ctx, DEPTH, seq, DA_HEADS, 2 * DA_QK),
            new_v.reshape(n_ctx, DEPTH, seq, DA_HEADS, DA_V),
            new_ckv, new_kpe, new_s5_re, new_s5_im)
```

```python
import functools
import math

import jax
import jax.numpy as jnp
import numpy as np
from jax import lax
from jax.experimental import pallas as pl
from jax.experimental.pallas import tpu as pltpu

F32 = jnp.float32
BF16 = jnp.bfloat16

D_MODEL = 1024
DEPTH = 2
GRID_W = 64
ROPE_THETA = 10000.0
EPS = 1e-6
N_MOD = 9
D_FF = 2816
DA_HEADS = 6
DA_QK = 32
DA_V = 64
DA_WIDTH = DA_HEADS * DA_V
MLA_HEADS = 6
MLA_Q_RANK = 256
MLA_KV_RANK = 128
MLA_NOPE = 64
MLA_ROPE = 32
MLA_V = 64
MLA_WIDTH = MLA_HEADS * MLA_V
S5_WIDTH = D_MODEL - DA_WIDTH - MLA_WIDTH
S5_CH = 16
S5_GROUPS = S5_WIDTH // S5_CH
S5_STATE = 64

LANES = 128
FF_CHUNK = 256
N_FF_CHUNKS = D_FF // FF_CHUNK
TOKEN_TILE = 512
Q_TILE = 256
S5_CHUNK = 16
N_SLABS = DA_WIDTH // LANES
MOD_ROWS = 8
VMEM_LIMIT = 56 * 1024 * 1024


def _rms(x, w):
    return x * lax.rsqrt(jnp.mean(x * x, axis=-1, keepdims=True) + EPS) * w


def _silu(x):
    return x * jax.nn.sigmoid(x)


def _gelu_tanh(x):
    c = math.sqrt(2.0 / math.pi)
    return x * (0.5 * (1.0 + jnp.tanh(c * (x + 0.044715 * (x * x * x)))))


def _dot(a, b):
    return jnp.dot(a, b, preferred_element_type=F32)


def _resident(shape):
    nd = len(shape)
    return pl.BlockSpec(shape, lambda *_: (0,) * nd, pipeline_mode=pl.Buffered(1))


def _resident_at(shape, *lead):
    nd = len(shape)
    return pl.BlockSpec((None,) * len(lead) + tuple(shape), lambda *_: tuple(lead) + (0,) * nd,
                        pipeline_mode=pl.Buffered(1))


def _adaln_kernel(c_ref, w_ref, b_ref, o_ref):
    s = _silu(c_ref[...])
    o_ref[...] = jnp.dot(s, w_ref[...], preferred_element_type=F32,
                         precision=lax.Precision.HIGHEST) + b_ref[...]


def _adaln(cvec, w_ada, b_ada):
    n = N_MOD * D_MODEL
    tn = 1152
    out = pl.pallas_call(
        _adaln_kernel,
        grid=(DEPTH, n // tn),
        in_specs=[
            pl.BlockSpec((MOD_ROWS, D_MODEL), lambda l, j: (0, 0)),
            pl.BlockSpec((None, D_MODEL, tn), lambda l, j: (l, 0, j)),
            pl.BlockSpec((None, 1, tn), lambda l, j: (l, 0, j)),
        ],
        out_specs=pl.BlockSpec((None, MOD_ROWS, tn), lambda l, j: (l, 0, j)),
        out_shape=jax.ShapeDtypeStruct((DEPTH, MOD_ROWS, n), F32),
        compiler_params=pltpu.CompilerParams(
            dimension_semantics=("arbitrary", "arbitrary"), vmem_limit_bytes=VMEM_LIMIT),
        name="adaln",
    )(cvec, w_ada, b_ada.reshape(DEPTH, 1, n))
    return out.reshape(DEPTH, MOD_ROWS, N_MOD, D_MODEL)


_ROWS_PER_TILE = TOKEN_TILE // S5_CHUNK
_SLOTS = LANES // S5_CH


def _to_group_major(u_scr, xg_ref, row0, n_rows):
    c0, n_chunks = row0 // S5_CHUNK, n_rows // S5_CHUNK
    slot = lax.broadcasted_iota(jnp.int32, (n_chunks, LANES), 1) // S5_CH
    for g in range(S5_GROUPS):
        for tt in range(S5_CHUNK // _SLOTS):
            acc = None
            for j in range(_SLOTS):
                t = tt * _SLOTS + j
                src = u_scr[g // _SLOTS, pl.ds(row0 + t, n_chunks, stride=S5_CHUNK), :]
                shift = ((j - g % _SLOTS) % _SLOTS) * S5_CH
                piece = pltpu.roll(src, shift, 1) if shift else src
                acc = piece if acc is None else jnp.where(slot == j, piece, acc)
            xg_ref[g, c0:c0 + n_chunks, tt * LANES:(tt + 1) * LANES] = acc.astype(BF16)
        yield


def _from_group_major(yg_ref, y_scr):
    slot = lax.broadcasted_iota(jnp.int32, (_ROWS_PER_TILE, LANES), 1) // S5_CH
    for t in range(S5_CHUNK):
        for gc in range(S5_GROUPS // _SLOTS):
            acc = None
            for m in range(_SLOTS):
                src = yg_ref[gc * _SLOTS + m, :, (t // _SLOTS) * LANES:(t // _SLOTS + 1) * LANES]
                shift = ((m - t % _SLOTS) % _SLOTS) * S5_CH
                piece = pltpu.roll(src, shift, 1) if shift else src
                acc = piece if acc is None else jnp.where(slot == m, piece, acc)
            y_scr[gc, pl.ds(t, _ROWS_PER_TILE, stride=S5_CHUNK), :] = acc


def _ffn_kernel(mix, final, mod_base, *refs):
    it = iter(refs)
    x_ref, mod_ref, nw_ref, wi_ref, wo_ref = (next(it) for _ in range(5))
    if mix:
        oa_ref, ob_ref, ys_ref, wglu_ref, bglu_ref, wout_ref = (next(it) for _ in range(6))
    if final:
        fnw_ref = next(it)
    out_ref, h_scr, acc_scr = next(it), next(it), next(it)

    x = x_ref[...]
    if mix:
        y_scr = next(it)
        _from_group_major(ys_ref, y_scr)
        yc = _gelu_tanh(jnp.concatenate([y_scr[gc] for gc in range(S5_WIDTH // LANES)], axis=1))
        oc = yc * jax.nn.sigmoid(_dot(yc.astype(BF16), wglu_ref[...]) + bglu_ref[...])
        mixed = jnp.concatenate([oa_ref[s] for s in range(N_SLABS)] + [ob_ref[s] for s in range(N_SLABS)]
                                + [oc.astype(BF16)], axis=1)
        x = x + mod_ref[0, mod_base - 1:mod_base, :] * _dot(mixed, wout_ref[...])

    shift = mod_ref[0, mod_base:mod_base + 1, :]
    scale = mod_ref[0, mod_base + 1:mod_base + 2, :]
    gate = mod_ref[0, mod_base + 2:mod_base + 3, :]
    h = _rms(x, nw_ref[...]) * (1.0 + scale) + shift
    h_scr[...] = h.astype(BF16)
    acc_scr[...] = jnp.zeros_like(acc_scr)

    for j in range(N_FF_CHUNKS):
        hb = h_scr[...]
        col = j * FF_CHUNK
        a = _dot(hb, wi_ref[:, col:col + FF_CHUNK])
        g = _dot(hb, wi_ref[:, D_FF + col:D_FF + col + FF_CHUNK])
        acc_scr[...] += _dot((_silu(g) * a).astype(BF16), wo_ref[col:col + FF_CHUNK, :])
    y = x + 0.5 * gate * acc_scr[...]
    if final:
        y = _rms(y, fnw_ref[...])
    out_ref[...] = y


def _mod_spec(layer, seg_len, row_base):
    return pl.BlockSpec((None, 1, N_MOD, D_MODEL),
                        lambda i: (layer, row_base + (i * TOKEN_TILE) // seg_len, 0, 0))


def _ffn_call(x, mod, seg_len, row_base, norm_w4, layer, which, ffn_w, mix_args=None, mix_w=None, final_w=None):
    t = x.shape[0]
    tm = TOKEN_TILE
    mix = mix_args is not None
    final = final_w is not None
    w_in_all, w_out_all = ffn_w
    tok = lambda i: (i, 0)
    in_specs = [
        pl.BlockSpec((tm, D_MODEL), tok),
        _mod_spec(layer, seg_len, row_base),
        _resident_at((1, D_MODEL), layer, 2 * which),
        _resident_at((D_MODEL, 2 * D_FF), layer, which),
        _resident_at((D_FF, D_MODEL), layer, which),
    ]
    args = [x, mod, norm_w4, w_in_all, w_out_all]
    if mix:
        in_specs += [
            pl.BlockSpec((N_SLABS, tm, LANES), lambda i: (0, i, 0)),
            pl.BlockSpec((N_SLABS, tm, LANES), lambda i: (0, i, 0)),
            pl.BlockSpec((S5_GROUPS, _ROWS_PER_TILE, S5_CHUNK * S5_CH), lambda i: (0, i, 0)),
            _resident_at((S5_WIDTH, S5_WIDTH), layer),
            _resident_at((1, S5_WIDTH), layer),
            _resident_at((D_MODEL, D_MODEL), layer),
        ]
        args += [*mix_args, *mix_w]
    if final:
        in_specs.append(_resident((1, D_MODEL)))
        args.append(final_w.reshape(1, D_MODEL))
    return pl.pallas_call(
        functools.partial(_ffn_kernel, mix, final, 6 * which),
        grid=(t // tm,),
        in_specs=in_specs,
        out_specs=pl.BlockSpec((tm, D_MODEL), tok),
        out_shape=jax.ShapeDtypeStruct((t, D_MODEL), F32),
        scratch_shapes=[pltpu.VMEM((tm, D_MODEL), BF16), pltpu.VMEM((tm, D_MODEL), F32)]
        + ([pltpu.VMEM((S5_WIDTH // LANES, tm, LANES), F32)] if mix else []),
        compiler_params=pltpu.CompilerParams(
            dimension_semantics=("arbitrary",), vmem_limit_bytes=VMEM_LIMIT),
        name="mix_ffn" if mix else "ffn",
    )(*args)


_C_Q, _C_K, _C_V = 0, DA_WIDTH, 2 * DA_WIDTH
_C_CQ = 3 * DA_WIDTH
_C_CKV = _C_CQ + MLA_Q_RANK
_C_U = _C_CKV + MLA_KV_RANK
_C_KPE = _C_U + S5_WIDTH
_NCOL = _C_KPE + LANES

_QSCALE_DIFF = DA_QK ** -0.5 * math.log2(math.e)
_QSCALE_MLA = (MLA_NOPE + MLA_ROPE) ** -0.5 * math.log2(math.e)


_INPROJ_PARTS = 2
_INPROJ_STAGGER = 6


def _rotate_half(x):
    half = DA_QK // 4
    lane = lax.broadcasted_iota(jnp.int32, x.shape, 1)
    return jnp.where(lane % (2 * half) < half, pltpu.roll(x, LANES - half, 1), pltpu.roll(x, half, 1))


def _inproj_kernel(rope, n_aliased, *refs):
    it = iter(refs)
    x_ref, mod_ref, nw_ref, w_ref, qnw_ref, kvnw_ref, wqup_ref, wk_ref, wv_ref = (next(it) for _ in range(9))
    if rope:
        cos_ref, sin_ref, cosm_ref, sinm_ref = (next(it) for _ in range(4))
    for _ in range(n_aliased):
        next(it)
    q_ref, kt_ref, v_ref, qm_ref, ktm_ref, vm_ref, xg_ref = (next(it) for _ in range(7))
    if not rope:
        ka_ref, va_ref, ckv_ref, kpe_ref = (next(it) for _ in range(4))
    u_scr = next(it)

    tm = x_ref.shape[0]
    n_part = _INPROJ_PARTS
    rows_part = tm // n_part

    def part(p):
        r0 = p * rows_part
        rows = slice(r0, r0 + rows_part)
        h = _rms(x_ref[rows, :], nw_ref[...]) * (1.0 + mod_ref[0, 4:5, :]) + mod_ref[0, 3:4, :]
        h = h.astype(BF16)
        yield
        qkv = _dot(h, w_ref[:, :_C_CQ])
        yield
        rest = _dot(h, w_ref[:, _C_CQ:])
        yield

        def cols(start, width):
            src, off = (qkv, 0) if start < _C_CQ else (rest, _C_CQ)
            return src[:, start - off:start - off + width]

        if rope:
            cos, sin, cosm, sinm = (r[rows, :] for r in (cos_ref, sin_ref, cosm_ref, sinm_ref))
        for s in range(N_SLABS):
            qs = cols(_C_Q + s * LANES, LANES)
            ks = cols(_C_K + s * LANES, LANES)
            if rope:
                qs = qs * cos + _rotate_half(qs) * sin
                ks = ks * cos + _rotate_half(ks) * sin
            q_ref[s, rows, :] = (qs * _QSCALE_DIFF).astype(BF16)
            kt_ref[s, :, rows] = ks.T.astype(BF16)
            v_ref[s, rows, :] = cols(_C_V + s * LANES, LANES).astype(BF16)
            yield
        for gc in range(S5_WIDTH // LANES):
            u_scr[gc, rows, :] = cols(_C_U + gc * LANES, LANES)
        yield from _to_group_major(u_scr, xg_ref, r0, rows_part)

        cqn = _rms(cols(_C_CQ, MLA_Q_RANK), qnw_ref[...]).astype(BF16)
        qm = _dot(cqn, wqup_ref[...])
        ckvn = _rms(cols(_C_CKV, MLA_KV_RANK), kvnw_ref[...])
        ckvb = ckvn.astype(BF16)
        kn = _dot(ckvb, wk_ref[...])
        vm = _dot(ckvb, wv_ref[...])
        yield
        kpe = cols(_C_KPE, LANES)
        if rope:
            kpe = kpe * cosm + _rotate_half(kpe) * sinm
        for hd in range(MLA_HEADS):
            qh = qm[:, hd * LANES:(hd + 1) * LANES]
            if rope:
                qh = qh * cosm + _rotate_half(qh) * sinm
            qm_ref[hd, rows, :] = (qh * _QSCALE_MLA).astype(BF16)
            ktm_ref[hd, :, rows] = (kn[:, hd * LANES:(hd + 1) * LANES] + kpe).T.astype(BF16)
            yield
        for s in range(N_SLABS):
            vm_ref[s, rows, :] = vm[:, s * LANES:(s + 1) * LANES].astype(BF16)

        if not rope:
            n_seq, seq = ka_ref.shape[:2]
            for b in range(r0 // seq, (r0 + rows_part) // seq):
                mine = slice(b * seq - r0, (b + 1) * seq - r0)
                ka_ref[b] = cols(_C_K, DA_WIDTH)[mine]
                va_ref[b] = cols(_C_V, DA_WIDTH)[mine]
                ckv_ref[b] = ckvn[mine]
                kpe_ref[b] = cols(_C_KPE, MLA_ROPE)[mine]

    def delayed(gen, steps):
        for _ in range(steps):
            yield
        yield from gen

    _interleave(delayed(part(p), p * _INPROJ_STAGGER) for p in range(n_part))


def _inproj_call(x, mod, seg_len, row_base, norm_w4, layer, w, mla_w, rope_tabs, cache_out=None):
    t = x.shape[0]
    tm = TOKEN_TILE
    rope = rope_tabs is not None
    qnw, kvnw, wqup, wk, wv = mla_w
    tok = lambda i: (i, 0)
    slab = lambda i: (0, i, 0)
    slab_t = lambda i: (0, 0, i)
    in_specs = [
        pl.BlockSpec((tm, D_MODEL), tok),
        _mod_spec(layer, seg_len, row_base),
        _resident_at((1, D_MODEL), layer, 1),
        _resident_at((D_MODEL, _NCOL), layer),
        _resident_at((1, MLA_Q_RANK), layer),
        _resident_at((1, MLA_KV_RANK), layer),
        _resident_at((MLA_Q_RANK, MLA_HEADS * LANES), layer),
        _resident_at((MLA_KV_RANK, MLA_HEADS * LANES), layer),
        _resident_at((MLA_KV_RANK, MLA_WIDTH), layer),
    ]
    args = [x, mod, norm_w4, w, qnw, kvnw, wqup, wk, wv]
    if rope:
        n_pos = rope_tabs[0].shape[0]
        pos = lambda i: (i % (n_pos // tm), 0)
        in_specs += [pl.BlockSpec((tm, LANES), pos)] * 4
        args += list(rope_tabs)
    out_shape = [
        jax.ShapeDtypeStruct((N_SLABS, t, LANES), BF16),
        jax.ShapeDtypeStruct((N_SLABS, LANES, t), BF16),
        jax.ShapeDtypeStruct((N_SLABS, t, LANES), BF16),
        jax.ShapeDtypeStruct((MLA_HEADS, t, LANES), BF16),
        jax.ShapeDtypeStruct((MLA_HEADS, LANES, t), BF16),
        jax.ShapeDtypeStruct((N_SLABS, t, LANES), BF16),
        jax.ShapeDtypeStruct((S5_GROUPS, t // S5_CHUNK, S5_CHUNK * S5_CH), BF16),
    ]
    out_specs = [
        pl.BlockSpec((N_SLABS, tm, LANES), slab),
        pl.BlockSpec((N_SLABS, LANES, tm), slab_t),
        pl.BlockSpec((N_SLABS, tm, LANES), slab),
        pl.BlockSpec((MLA_HEADS, tm, LANES), slab),
        pl.BlockSpec((MLA_HEADS, LANES, tm), slab_t),
        pl.BlockSpec((N_SLABS, tm, LANES), slab),
        pl.BlockSpec((S5_GROUPS, _ROWS_PER_TILE, S5_CHUNK * S5_CH), slab),
    ]
    aliases = {}
    if not rope:
        seq, stacked = cache_out
        n_first = len(out_shape)
        for k, arr in enumerate(stacked):
            out_shape.append(jax.ShapeDtypeStruct(arr.shape, F32))
            out_specs.append(pl.BlockSpec((tm // seq, None, seq, arr.shape[-1]), lambda i: (i, layer, 0, 0)))
            aliases[len(args)] = n_first + k
            in_specs.append(pl.BlockSpec(memory_space=pl.ANY))
            args.append(arr)
    return pl.pallas_call(
        functools.partial(_inproj_kernel, rope, len(aliases)),
        grid=(t // tm,),
        in_specs=in_specs,
        out_specs=out_specs,
        out_shape=out_shape,
        input_output_aliases=aliases,
        scratch_shapes=[pltpu.VMEM((S5_WIDTH // LANES, tm, LANES), F32)],
        compiler_params=pltpu.CompilerParams(
            dimension_semantics=("arbitrary",), vmem_limit_bytes=VMEM_LIMIT),
        name="inproj_dec" if rope else "inproj_ctx",
    )(*args)


def _mla_expand_kernel(ckv_ref, kpe_ref, wk_ref, wv_ref, ktm_ref, vm_ref):
    ckvb = ckv_ref[...].astype(BF16)
    kn = _dot(ckvb, wk_ref[...])
    vm = _dot(ckvb, wv_ref[...])
    kpe = kpe_ref[...]
    for hd in range(MLA_HEADS):
        ktm_ref[hd] = (kn[:, hd * LANES:(hd + 1) * LANES] + kpe).T.astype(BF16)
    for s in range(N_SLABS):
        vm_ref[s] = vm[:, s * LANES:(s + 1) * LANES].astype(BF16)


def _mla_expand_call(ckv, kpe128, wk, wv, layer):
    n_batch, _, rows, _ = ckv.shape
    t = n_batch * rows
    return pl.pallas_call(
        _mla_expand_kernel,
        grid=(n_batch,),
        in_specs=[
            pl.BlockSpec((None, None, rows, MLA_KV_RANK), lambda i: (i, layer, 0, 0)),
            pl.BlockSpec((None, None, rows, LANES), lambda i: (i, layer, 0, 0)),
            _resident_at((MLA_KV_RANK, MLA_HEADS * LANES), layer),
            _resident_at((MLA_KV_RANK, MLA_WIDTH), layer),
        ],
        out_specs=[
            pl.BlockSpec((MLA_HEADS, LANES, rows), lambda i: (0, 0, i)),
            pl.BlockSpec((N_SLABS, rows, LANES), lambda i: (0, i, 0)),
        ],
        out_shape=[
            jax.ShapeDtypeStruct((MLA_HEADS, LANES, t), BF16),
            jax.ShapeDtypeStruct((N_SLABS, t, LANES), BF16),
        ],
        compiler_params=pltpu.CompilerParams(dimension_semantics=("arbitrary",)),
        name="mla_expand",
    )(ckv, kpe128, wk, wv)


KEY_TILE = 256
N_SCORE_BUFS = 4


def _interleave(stages):
    stages = list(stages)
    while stages:
        for g in list(stages):
            try:
                next(g)
            except StopIteration:
                stages.remove(g)


def _attn_kernel(n_parts, lam_init, *refs):
    q_ref, qm_ref, lamp_ref, subw_ref = refs[:4]
    parts = [refs[4 + 4 * i:8 + 4 * i] for i in range(n_parts)]
    oa_ref, ob_ref, sbuf = refs[4 + 4 * n_parts:]

    lp = lamp_ref[...]
    lam = (jnp.exp(jnp.sum(lp[0:1] * lp[1:2], keepdims=True))
           - jnp.exp(jnp.sum(lp[2:3] * lp[3:4], keepdims=True)) + lam_init)
    tq = q_ref.shape[1]
    lane = lax.broadcasted_iota(jnp.int32, (tq, LANES), 1)
    low_half = lane < DA_V
    subw = subw_ref[...] * (1.0 - lam_init)

    tiles, col = [], 0
    for p, (kt, _, _, _) in enumerate(parts):
        for k0 in range(0, kt.shape[2], KEY_TILE):
            tiles.append((p, k0, col))
            col += KEY_TILE

    def lane_fold(op, acc, x):
        for j in range(x.shape[1] // LANES):
            piece = x[:, j * LANES:(j + 1) * LANES]
            acc = piece if acc is None else op(acc, piece)
        return acc

    def scores(unit):
        buf = unit['buf']
        if unit['mla']:
            lhs, kref = qm_ref[unit['head']], 2
            kidx = unit['head']
        else:
            qs = q_ref[unit['slab']]
            lhs, kref = jnp.where(lane // DA_QK == unit['sub'], qs, jnp.zeros_like(qs)), 0
            kidx = unit['slab']
        mx = None
        for p, k0, c0 in tiles:
            s = _dot(lhs, parts[p][kref][kidx, :, k0:k0 + KEY_TILE])
            sbuf[buf, :, c0:c0 + KEY_TILE] = s
            mx = lane_fold(jnp.maximum, mx, s)
            yield
        unit['max'] = jnp.max(mx, axis=1, keepdims=True)

    def expo(unit):
        buf, tot = unit['buf'], None
        for _, _, c0 in tiles:
            e = jnp.exp2(sbuf[buf, :, c0:c0 + KEY_TILE] - unit['max'])
            sbuf[buf, :, c0:c0 + KEY_TILE] = e
            tot = lane_fold(jnp.add, tot, e)
            yield
        unit['sum'] = jnp.sum(tot, axis=1, keepdims=True)

    def pv(head):
        units, slab = head['units'], head['slab']
        vref = 3 if units[0]['mla'] else 1
        l0 = units[0]['sum']
        if len(units) == 2:
            r = lam * l0 / units[1]['sum']
        acc = None
        for p, k0, c0 in tiles:
            w = sbuf[units[0]['buf'], :, c0:c0 + KEY_TILE]
            if len(units) == 2:
                w = w - r * sbuf[units[1]['buf'], :, c0:c0 + KEY_TILE]
            o = _dot(w.astype(BF16), parts[p][vref][slab, k0:k0 + KEY_TILE, :])
            acc = o if acc is None else acc + o
            yield
        head['out'] = acc * (1.0 / l0)

    def finish(slab, mla, lo, hi):
        o = jnp.where(low_half, lo['out'], hi['out'])
        if mla:
            ob_ref[slab] = o.astype(BF16)
            return
        o2 = o * o
        ss_lo = jnp.sum(jnp.where(low_half, o2, 0.0), axis=1, keepdims=True)
        ss_hi = jnp.sum(jnp.where(low_half, 0.0, o2), axis=1, keepdims=True)
        ms = jnp.where(low_half, ss_lo, ss_hi) * (1.0 / DA_V)
        oa_ref[slab] = (o * lax.rsqrt(ms + EPS) * subw).astype(BF16)

    units, heads = [], []
    for slab in range(N_SLABS):
        for mla in (False, True):
            for half in range(2):
                mine = [dict(mla=mla, slab=slab, head=2 * slab + half, sub=2 * half + c)
                        for c in range(1 if mla else 2)]
                heads.append(dict(units=mine, slab=slab, mla=mla, half=half, last=len(units) + len(mine) - 1))
                units += mine
    for k, unit in enumerate(units):
        unit['buf'] = k % N_SCORE_BUFS

    pending = None
    for k in range(len(units) + 2):
        stages = []
        if k < len(units):
            stages.append(scores(units[k]))
        if 1 <= k <= len(units):
            stages.append(expo(units[k - 1]))
        ready = [h for h in heads if h['last'] == k - 2]
        if ready:
            stages.append(pv(ready[0]))
        _interleave(stages)
        if ready and ready[0]['half'] == 1:
            finish(ready[0]['slab'], ready[0]['mla'], pending[(ready[0]['slab'], ready[0]['mla'])], ready[0])
        elif ready:
            pending = dict(pending or {})
            pending[(ready[0]['slab'], ready[0]['mla'])] = ready[0]


def _attn_call(q, qm, lamp, subw, parts, n_batch, layer, lam_init):
    t = q.shape[1]
    seq = t // n_batch
    tq = min(Q_TILE, seq)
    nq = seq // tq
    qmap = lambda b, i: (0, b * nq + i, 0)
    in_specs = [
        pl.BlockSpec((N_SLABS, tq, LANES), qmap),
        pl.BlockSpec((MLA_HEADS, tq, LANES), qmap),
        pl.BlockSpec((None, 4, DA_QK), lambda b, i: (layer, 0, 0)),
        pl.BlockSpec((None, 1, LANES), lambda b, i: (layer, 0, 0)),
    ]
    args = [q, qm, lamp, subw]

    def keys_spec(arr, n_keys, transposed):
        block = (arr.shape[-3], LANES, n_keys) if transposed else (arr.shape[-3], n_keys, LANES)
        at = (lambda b: (0, 0, b)) if transposed else (lambda b: (0, b, 0))
        if arr.ndim == 4:
            return pl.BlockSpec((None,) + block, lambda b, i: (layer,) + at(b))
        return pl.BlockSpec(block, lambda b, i: at(b))

    for kt, v, ktm, vm, n_keys in parts:
        in_specs += [keys_spec(kt, n_keys, True), keys_spec(v, n_keys, False),
                     keys_spec(ktm, n_keys, True), keys_spec(vm, n_keys, False)]
        args += [kt, v, ktm, vm]
    return pl.pallas_call(
        functools.partial(_attn_kernel, len(parts), lam_init),
        grid=(n_batch, nq),
        in_specs=in_specs,
        out_specs=[pl.BlockSpec((N_SLABS, tq, LANES), qmap)] * 2,
        out_shape=[jax.ShapeDtypeStruct((N_SLABS, t, LANES), BF16)] * 2,
        scratch_shapes=[pltpu.VMEM((N_SCORE_BUFS, tq, sum(p[4] for p in parts)), F32)],
        compiler_params=pltpu.CompilerParams(
            dimension_semantics=("arbitrary", "arbitrary"), vmem_limit_bytes=VMEM_LIMIT),
        name="attn_dec" if len(parts) > 1 else "attn_ctx",
    )(*args)


def _s5_kernel(n_batch, n_chunks, x_ref, kmat_ref, bst_ref, cst_ref, lamt_ref, h0_ref, y_ref, fin_ref,
               s_scr, hin_scr):
    x = x_ref[...]
    y = _dot(x, kmat_ref[...])
    for k in range(4):
        s_scr[k] = _dot(x, bst_ref[k])

    a_fr, a_fi, a_br, a_bi = (lamt_ref[k] for k in range(4))

    def chunk_rows(j):
        return pl.ds(j, n_batch, stride=n_chunks)

    def step(j, carry):
        fr, fi, br, bi = carry
        rf, rb = chunk_rows(j), chunk_rows(n_chunks - 1 - j)
        hin_scr[0, rf, :] = fr
        hin_scr[1, rf, :] = fi
        hin_scr[2, rb, :] = br
        hin_scr[3, rb, :] = bi
        nfr = a_fr * fr - a_fi * fi + s_scr[0, rf, :]
        nfi = a_fr * fi + a_fi * fr + s_scr[1, rf, :]
        nbr = a_br * br - a_bi * bi + s_scr[2, rb, :]
        nbi = a_br * bi + a_bi * br + s_scr[3, rb, :]
        return nfr, nfi, nbr, nbi

    fin = lax.fori_loop(0, n_chunks, step, tuple(h0_ref[k] for k in range(4)))
    for k in range(4):
        fin_ref[k] = fin[k]
        y += _dot(hin_scr[k].astype(BF16), cst_ref[k])
    y_ref[...] = y


def _s5_call(xg, kmat, bst, cst, lamt, h0, n_batch, layer, h0_layer):
    rows = xg.shape[1]
    n_chunks = rows // n_batch
    wide = S5_CHUNK * S5_CH
    grp = lambda g: (g, 0, 0)
    grp4 = lambda g: (g, 0, 0, 0)
    return pl.pallas_call(
        functools.partial(_s5_kernel, n_batch, n_chunks),
        grid=(S5_GROUPS,),
        in_specs=[
            pl.BlockSpec((None, rows, wide), grp),
            pl.BlockSpec((None, None, wide, wide), lambda g: (layer, g, 0, 0)),
            pl.BlockSpec((None, None, 4, wide, LANES), lambda g: (layer, g, 0, 0, 0)),
            pl.BlockSpec((None, None, 4, LANES, wide), lambda g: (layer, g, 0, 0, 0)),
            pl.BlockSpec((None, None, 4, 1, LANES), lambda g: (layer, g, 0, 0, 0)),
            pl.BlockSpec((None, None, 4, n_batch, LANES), lambda g: (h0_layer, g, 0, 0, 0)),
        ],
        out_specs=[
            pl.BlockSpec((None, rows, wide), grp),
            pl.BlockSpec((None, 4, n_batch, LANES), grp4),
        ],
        out_shape=[
            jax.ShapeDtypeStruct((S5_GROUPS, rows, wide), F32),
            jax.ShapeDtypeStruct((S5_GROUPS, 4, n_batch, LANES), F32),
        ],
        scratch_shapes=[pltpu.VMEM((4, rows, LANES), F32), pltpu.VMEM((4, rows, LANES), F32)],
        compiler_params=pltpu.CompilerParams(dimension_semantics=("arbitrary",)),
        name="s5",
    )(xg, kmat, bst, cst, lamt, h0)


def _s5_operators(a_re, a_im, log_step, b_re, b_im, c_re, c_im, d):
    hi = lax.Precision.HIGHEST
    n = S5_CHUNK
    wide = n * S5_CH
    ar, ai = a_re.astype(F32), a_im.astype(F32)
    step = jnp.exp(log_step.astype(F32))[..., None]
    zr, zi = ar * step, ai * step
    k = jnp.arange(n + 1, dtype=F32)[:, None, None, None]
    mag = jnp.exp(k * zr[None])
    pr, pi = mag * jnp.cos(k * zi[None]), mag * jnp.sin(k * zi[None])

    xr, xi = pr[1] - 1.0, pi[1]
    den = ar * ar + ai * ai
    fr, fi = ((xr * ar + xi * ai) / den)[:, :, None, :], ((xi * ar - xr * ai) / den)[:, :, None, :]
    br, bi = (jnp.swapaxes(b.astype(F32), 2, 3) for b in (b_re, b_im))
    bbr, bbi = fr * br - fi * bi, fr * bi + fi * br
    cr, ci = c_re.astype(F32), c_im.astype(F32)

    tt = np.arange(n)

    def state_readout(powers, dirn):
        qr = pr[powers, dirn].transpose(1, 2, 0)[:, :, :, None]
        qi = pi[powers, dirn].transpose(1, 2, 0)[:, :, :, None]
        ctr = jnp.swapaxes(cr[dirn], 1, 2)[:, :, None, :]
        cti = jnp.swapaxes(ci[dirn], 1, 2)[:, :, None, :]
        rr = ctr * qr - cti * qi
        ri = -(ctr * qi + cti * qr)
        return rr.reshape(S5_GROUPS, S5_STATE, wide), ri.reshape(S5_GROUPS, S5_STATE, wide)

    def lag_response(powers, dirn):
        rr, ri = state_readout(powers, dirn)
        return (jnp.einsum('gkp,gpn->gkn', bbr[dirn], rr, precision=hi)
                + jnp.einsum('gkp,gpn->gkn', bbi[dirn], ri, precision=hi))

    fwd, bwd = lag_response(tt, 0), lag_response(n - 1 - tt, 1)
    skip = d.astype(F32)[:, None, :] * jnp.eye(S5_CH, dtype=F32)[None]
    strip = jnp.concatenate([bwd[:, :, :wide - S5_CH], bwd[:, :, wide - S5_CH:] + fwd[:, :, :S5_CH] + skip,
                             fwd[:, :, S5_CH:]], axis=-1)
    kmat = jnp.stack([strip[:, :, (n - 1 - s) * S5_CH:(n - 1 - s) * S5_CH + wide] for s in range(n)], axis=1)
    kmat = kmat.reshape(S5_GROUPS, wide, wide)

    def edge_states(powers, dirn):
        qr = jnp.swapaxes(pr[powers, dirn], 0, 1)[:, :, None, :]
        qi = jnp.swapaxes(pi[powers, dirn], 0, 1)[:, :, None, :]
        er = qr * bbr[dirn][:, None] - qi * bbi[dirn][:, None]
        ei = qr * bbi[dirn][:, None] + qi * bbr[dirn][:, None]
        return er.reshape(S5_GROUPS, wide, S5_STATE), ei.reshape(S5_GROUPS, wide, S5_STATE)

    bst = jnp.stack([*edge_states(n - 1 - tt, 0), *edge_states(tt, 1)], axis=1)
    cst = jnp.stack([*state_readout(tt + 1, 0), *state_readout(n - tt, 1)], axis=1)
    lamt = jnp.stack([pr[n, 0], pi[n, 0], pr[n, 1], pi[n, 1]], axis=1)
    zpad = LANES - S5_STATE
    bst = jnp.pad(bst, ((0, 0), (0, 0), (0, 0), (0, zpad)))
    cst = jnp.pad(cst, ((0, 0), (0, 0), (0, zpad), (0, 0)))
    lamt = jnp.pad(lamt, ((0, 0), (0, 0), (0, zpad)))
    return kmat.astype(BF16), bst.astype(BF16), cst.astype(BF16), lamt[:, :, None, :]


def _rope_tables(n_pos):
    n = DA_QK // 4
    inv = ROPE_THETA ** (-jnp.arange(n, dtype=F32) / n)
    pos = jnp.arange(n_pos, dtype=jnp.int32)
    ang_r = (pos // GRID_W).astype(F32)[:, None] * inv[None, :]
    ang_c = (pos % GRID_W).astype(F32)[:, None] * inv[None, :]
    cos32 = jnp.concatenate([jnp.cos(ang_r)] * 2 + [jnp.cos(ang_c)] * 2, axis=1)
    sin32 = jnp.concatenate([-jnp.sin(ang_r), jnp.sin(ang_r), -jnp.sin(ang_c), jnp.sin(ang_c)], axis=1)
    pad = LANES - MLA_ROPE
    cosm = jnp.concatenate([cos32, jnp.ones((n_pos, pad), F32)], axis=1)
    sinm = jnp.concatenate([sin32, jnp.zeros((n_pos, pad), F32)], axis=1)
    return jnp.tile(cos32, (1, 4)), jnp.tile(sin32, (1, 4)), cosm, sinm


def _prep_inproj(w):
    o = np.cumsum([0, DA_WIDTH, DA_WIDTH, DA_WIDTH, MLA_Q_RANK, MLA_KV_RANK, MLA_ROPE, S5_WIDTH])
    wq, wk, wv, wcq, wckv, wkpe, wu = (w[..., o[i]:o[i + 1]] for i in range(7))
    wkpe = jnp.pad(wkpe, ((0, 0), (0, 0), (0, LANES - MLA_ROPE)))
    return jnp.concatenate([wq, wk, wv, wcq, wckv, wu, wkpe], axis=-1).astype(BF16)


def _prep_mla(q_norm_w, w_q_up, kv_norm_w, w_kv_up):
    pad = LANES - MLA_ROPE - MLA_NOPE
    wq = w_q_up.reshape(DEPTH, MLA_Q_RANK, MLA_HEADS, MLA_NOPE + MLA_ROPE)
    wqup = jnp.pad(jnp.concatenate([wq[..., MLA_NOPE:], wq[..., :MLA_NOPE]], axis=-1),
                   ((0, 0), (0, 0), (0, 0), (0, pad))).reshape(DEPTH, MLA_Q_RANK, MLA_HEADS * LANES)
    wkv = w_kv_up.reshape(DEPTH, MLA_KV_RANK, MLA_HEADS, MLA_NOPE + MLA_V)
    wk = jnp.pad(wkv[..., :MLA_NOPE], ((0, 0), (0, 0), (0, 0), (MLA_ROPE, pad)))
    wk = wk.reshape(DEPTH, MLA_KV_RANK, MLA_HEADS * LANES)
    wv = wkv[..., MLA_NOPE:].reshape(DEPTH, MLA_KV_RANK, MLA_WIDTH)
    return (q_norm_w.reshape(DEPTH, 1, MLA_Q_RANK), kv_norm_w.reshape(DEPTH, 1, MLA_KV_RANK),
            wqup.astype(BF16), wk.astype(BF16), wv.astype(BF16))


def _cached_slabs(cache, transposed):
    b, _, n = cache.shape[:3]
    a = cache.reshape(b, DEPTH, n, N_SLABS, LANES).astype(BF16)
    a = a.transpose(1, 3, 4, 0, 2) if transposed else a.transpose(1, 3, 0, 2, 4)
    return a.reshape(DEPTH, N_SLABS, LANES, b * n) if transposed else a.reshape(DEPTH, N_SLABS, b * n, LANES)


def kernel(x_prompt, x_sample, cache_diff_k, cache_diff_v, cache_mla_ckv, cache_mla_kpe, state_s5_re, state_s5_im, c, c_ctx, w_ada, b_ada, norm_w, ffn_w_in, ffn_w_out, w_in, diff_lambda, diff_subln_w, mla_q_norm_w, mla_w_q_up, mla_kv_norm_w, mla_w_kv_up, s5_a_re, s5_a_im, s5_log_step, s5_b_re, s5_b_im, s5_c_re, s5_c_im, s5_d, s5_w_glu, s5_b_glu, w_out, final_norm_w):
    n_ctx, seq, _ = x_prompt.shape
    n_dec, dec_seq, _ = x_sample.shape
    past = cache_diff_k.shape[2]

    cvec = jnp.concatenate(
        [c_ctx[None, :], c, jnp.zeros((MOD_ROWS - 1 - n_dec, D_MODEL), F32)], axis=0)
    mod = _adaln(cvec, w_ada, b_ada)
    rope_tabs = _rope_tables(dec_seq)

    groups = {
        'ctx': dict(x=x_prompt.reshape(n_ctx * seq, D_MODEL), seg=n_ctx * seq, row=0, nb=n_ctx),
        'dec': dict(x=x_sample.reshape(n_dec * dec_seq, D_MODEL), seg=dec_seq, row=1, nb=n_dec),
    }
    ffn_w = (ffn_w_in.astype(BF16), ffn_w_out.astype(BF16))
    norm_w4 = norm_w.reshape(DEPTH, 3, 1, D_MODEL)
    w_proj = _prep_inproj(w_in)
    mla_w = _prep_mla(mla_q_norm_w, mla_w_q_up, mla_kv_norm_w, mla_w_kv_up)
    mix_w = (s5_w_glu.astype(BF16), s5_b_glu.reshape(DEPTH, 1, S5_WIDTH), w_out.astype(BF16))
    s5_ops = jax.vmap(_s5_operators)(s5_a_re, s5_a_im, s5_log_step, s5_b_re, s5_b_im, s5_c_re, s5_c_im, s5_d)
    subw = jnp.tile(diff_subln_w, (1, 2)).reshape(DEPTH, 1, LANES)
    ctx_kt, ctx_v = _cached_slabs(cache_diff_k, True), _cached_slabs(cache_diff_v, False)
    ctx_kpe = jnp.pad(cache_mla_kpe, ((0, 0), (0, 0), (0, 0), (0, LANES - MLA_ROPE)))
    h0_dec = jnp.stack([state_s5_re[:, :, 0], state_s5_im[:, :, 0], state_s5_re[:, :, 1], state_s5_im[:, :, 1]], 0)
    h0_dec = jnp.pad(h0_dec.transpose(2, 3, 0, 1, 4), ((0, 0),) * 4 + ((0, LANES - S5_STATE),))
    h0_ctx = jnp.zeros((1, S5_GROUPS, 4, n_ctx, LANES), F32)
    caches = tuple(jnp.zeros((n_ctx, DEPTH, seq, width), F32)
                   for width in (DA_WIDTH, DA_WIDTH, MLA_KV_RANK, MLA_ROPE))
    s5_states = []
    for l in range(DEPTH):
        lam_init = 0.8 - 0.6 * math.exp(-0.3 * l)
        final_w = final_norm_w if l == DEPTH - 1 else None
        for name, g in groups.items():
            dec = name == 'dec'
            x = _ffn_call(g['x'], mod, g['seg'], g['row'], norm_w4, l, 0, ffn_w)
            proj = _inproj_call(x, mod, g['seg'], g['row'], norm_w4, l, w_proj, mla_w,
                                rope_tabs if dec else None, cache_out=None if dec else (seq, caches))
            q, kt, v, qm, ktm, vm, xg = proj[:7]
            parts = [(kt, v, ktm, vm, g['x'].shape[0] // g['nb'])]
            if dec:
                cktm, cvm = _mla_expand_call(cache_mla_ckv, ctx_kpe, mla_w[3], mla_w[4], l)
                parts.append((ctx_kt, ctx_v, cktm, cvm, past))
            o_a, o_b = _attn_call(q, qm, diff_lambda, subw, parts, g['nb'], l, lam_init)
            y_s5, fin = _s5_call(xg, *s5_ops, h0_dec if dec else h0_ctx, g['nb'], l, l if dec else 0)
            g['x'] = _ffn_call(x, mod, g['seg'], g['row'], norm_w4, l, 1, ffn_w,
                               mix_args=(o_a, o_b, y_s5), mix_w=mix_w, final_w=final_w)
            if not dec:
                caches = proj[7:]
                fin = fin[..., :S5_STATE].transpose(1, 2, 0, 3)
                s5_states.append((jnp.stack([fin[0], fin[2]], axis=1), jnp.stack([fin[1], fin[3]], axis=1)))

    y_prompt = groups['ctx']['x'].reshape(n_ctx, seq, D_MODEL)
    y_sample = groups['dec']['x'].reshape(n_dec, dec_seq, D_MODEL)
    new_k, new_v, new_ckv, new_kpe = caches
    new_s5_re = jnp.stack([t[0] for t in s5_states], axis=1)
    new_s5_im = jnp.stack([t[1] for t in s5_states], axis=1)
    return (y_prompt, y_sample,
            new_k.reshape(n_ctx, DEPTH, seq, DA_HEADS, 2 * DA_QK),
            new_v.reshape(n_ctx, DEPTH, seq, DA_HEADS, DA_V),
            new_ckv, new_kpe, new_s5_re, new_s5_im)
```

```python
import functools
import math

import jax
import jax.numpy as jnp
import numpy as np
from jax import lax
from jax.experimental import pallas as pl
from jax.experimental.pallas import tpu as pltpu

F32 = jnp.float32
BF16 = jnp.bfloat16

D_MODEL = 1024
DEPTH = 2
GRID_W = 64
ROPE_THETA = 10000.0
EPS = 1e-6
N_MOD = 9
D_FF = 2816
DA_HEADS = 6
DA_QK = 32
DA_V = 64
DA_WIDTH = DA_HEADS * DA_V
MLA_HEADS = 6
MLA_Q_RANK = 256
MLA_KV_RANK = 128
MLA_NOPE = 64
MLA_ROPE = 32
MLA_V = 64
MLA_WIDTH = MLA_HEADS * MLA_V
S5_WIDTH = D_MODEL - DA_WIDTH - MLA_WIDTH
S5_CH = 16
S5_GROUPS = S5_WIDTH // S5_CH
S5_STATE = 64

LANES = 128
FF_CHUNK = 256
N_FF_CHUNKS = D_FF // FF_CHUNK
TOKEN_TILE = 512
Q_TILE = 256
S5_CHUNK = 16
N_SLABS = DA_WIDTH // LANES
MOD_ROWS = 8
VMEM_LIMIT = 56 * 1024 * 1024


def _rms(x, w):
    return x * lax.rsqrt(jnp.mean(x * x, axis=-1, keepdims=True) + EPS) * w


def _silu(x):
    return x * jax.nn.sigmoid(x)


def _gelu_tanh(x):
    c = math.sqrt(2.0 / math.pi)
    return x * (0.5 * (1.0 + jnp.tanh(c * (x + 0.044715 * (x * x * x)))))


def _dot(a, b):
    return jnp.dot(a, b, preferred_element_type=F32)


def _resident(shape):
    nd = len(shape)
    return pl.BlockSpec(shape, lambda *_: (0,) * nd, pipeline_mode=pl.Buffered(1))


def _resident_at(shape, *lead):
    nd = len(shape)
    return pl.BlockSpec((None,) * len(lead) + tuple(shape), lambda *_: tuple(lead) + (0,) * nd,
                        pipeline_mode=pl.Buffered(1))


def _adaln_kernel(c_ref, w_ref, b_ref, o_ref):
    s = _silu(c_ref[...])
    o_ref[...] = jnp.dot(s, w_ref[...], preferred_element_type=F32,
                         precision=lax.Precision.HIGHEST) + b_ref[...]


def _adaln(cvec, w_ada, b_ada):
    n = N_MOD * D_MODEL
    tn = 1152
    out = pl.pallas_call(
        _adaln_kernel,
        grid=(DEPTH, n // tn),
        in_specs=[
            pl.BlockSpec((MOD_ROWS, D_MODEL), lambda l, j: (0, 0)),
            pl.BlockSpec((None, D_MODEL, tn), lambda l, j: (l, 0, j)),
            pl.BlockSpec((None, 1, tn), lambda l, j: (l, 0, j)),
        ],
        out_specs=pl.BlockSpec((None, MOD_ROWS, tn), lambda l, j: (l, 0, j)),
        out_shape=jax.ShapeDtypeStruct((DEPTH, MOD_ROWS, n), F32),
        compiler_params=pltpu.CompilerParams(
            dimension_semantics=("arbitrary", "arbitrary"), vmem_limit_bytes=VMEM_LIMIT),
        name="adaln",
    )(cvec, w_ada, b_ada.reshape(DEPTH, 1, n))
    return out.reshape(DEPTH, MOD_ROWS, N_MOD, D_MODEL)


_ROWS_PER_TILE = TOKEN_TILE // S5_CHUNK
_SLOTS = LANES // S5_CH


def _to_group_major(u_scr, xg_ref, row0, n_rows):
    c0, n_chunks = row0 // S5_CHUNK, n_rows // S5_CHUNK
    slot = lax.broadcasted_iota(jnp.int32, (n_chunks, LANES), 1) // S5_CH
    for g in range(S5_GROUPS):
        for tt in range(S5_CHUNK // _SLOTS):
            acc = None
            for j in range(_SLOTS):
                t = tt * _SLOTS + j
                src = u_scr[g // _SLOTS, pl.ds(row0 + t, n_chunks, stride=S5_CHUNK), :]
                shift = ((j - g % _SLOTS) % _SLOTS) * S5_CH
                piece = pltpu.roll(src, shift, 1) if shift else src
                acc = piece if acc is None else jnp.where(slot == j, piece, acc)
            xg_ref[g, c0:c0 + n_chunks, tt * LANES:(tt + 1) * LANES] = acc.astype(BF16)
        yield


def _from_group_major(yg_ref, y_scr):
    slot = lax.broadcasted_iota(jnp.int32, (_ROWS_PER_TILE, LANES), 1) // S5_CH
    for t in range(S5_CHUNK):
        for gc in range(S5_GROUPS // _SLOTS):
            acc = None
            for m in range(_SLOTS):
                src = yg_ref[gc * _SLOTS + m, :, (t // _SLOTS) * LANES:(t // _SLOTS + 1) * LANES]
                shift = ((m - t % _SLOTS) % _SLOTS) * S5_CH
                piece = pltpu.roll(src, shift, 1) if shift else src
                acc = piece if acc is None else jnp.where(slot == m, piece, acc)
            y_scr[gc, pl.ds(t, _ROWS_PER_TILE, stride=S5_CHUNK), :] = acc


def _ffn_kernel(mix, final, mod_base, *refs):
    it = iter(refs)
    x_ref, mod_ref, nw_ref, wi_ref, wo_ref = (next(it) for _ in range(5))
    if mix:
        oa_ref, ob_ref, ys_ref, wglu_ref, bglu_ref, wout_ref = (next(it) for _ in range(6))
    if final:
        fnw_ref = next(it)
    out_ref, h_scr, acc_scr = next(it), next(it), next(it)

    x = x_ref[...]
    if mix:
        y_scr = next(it)
        _from_group_major(ys_ref, y_scr)
        yc = _gelu_tanh(jnp.concatenate([y_scr[gc] for gc in range(S5_WIDTH // LANES)], axis=1))
        oc = yc * jax.nn.sigmoid(_dot(yc.astype(BF16), wglu_ref[...]) + bglu_ref[...])
        mixed = jnp.concatenate([oa_ref[s] for s in range(N_SLABS)] + [ob_ref[s] for s in range(N_SLABS)]
                                + [oc.astype(BF16)], axis=1)
        x = x + mod_ref[0, mod_base - 1:mod_base, :] * _dot(mixed, wout_ref[...])

    shift = mod_ref[0, mod_base:mod_base + 1, :]
    scale = mod_ref[0, mod_base + 1:mod_base + 2, :]
    gate = mod_ref[0, mod_base + 2:mod_base + 3, :]
    h = _rms(x, nw_ref[...]) * (1.0 + scale) + shift
    h_scr[...] = h.astype(BF16)
    acc_scr[...] = jnp.zeros_like(acc_scr)

    for j in range(N_FF_CHUNKS):
        hb = h_scr[...]
        col = j * FF_CHUNK
        a = _dot(hb, wi_ref[:, col:col + FF_CHUNK])
        g = _dot(hb, wi_ref[:, D_FF + col:D_FF + col + FF_CHUNK])
        acc_scr[...] += _dot((_silu(g) * a).astype(BF16), wo_ref[col:col + FF_CHUNK, :])
    y = x + 0.5 * gate * acc_scr[...]
    if final:
        y = _rms(y, fnw_ref[...])
    out_ref[...] = y


def _mod_spec(layer, seg_len, row_base):
    return pl.BlockSpec((None, 1, N_MOD, D_MODEL),
                        lambda i: (layer, row_base + (i * TOKEN_TILE) // seg_len, 0, 0))


def _ffn_call(x, mod, seg_len, row_base, norm_w4, layer, which, ffn_w, mix_args=None, mix_w=None, final_w=None):
    t = x.shape[0]
    tm = TOKEN_TILE
    mix = mix_args is not None
    final = final_w is not None
    w_in_all, w_out_all = ffn_w
    tok = lambda i: (i, 0)
    in_specs = [
        pl.BlockSpec((tm, D_MODEL), tok),
        _mod_spec(layer, seg_len, row_base),
        _resident_at((1, D_MODEL), layer, 2 * which),
        _resident_at((D_MODEL, 2 * D_FF), layer, which),
        _resident_at((D_FF, D_MODEL), layer, which),
    ]
    args = [x, mod, norm_w4, w_in_all, w_out_all]
    if mix:
        in_specs += [
            pl.BlockSpec((N_SLABS, tm, LANES), lambda i: (0, i, 0)),
            pl.BlockSpec((N_SLABS, tm, LANES), lambda i: (0, i, 0)),
            pl.BlockSpec((S5_GROUPS, _ROWS_PER_TILE, S5_CHUNK * S5_CH), lambda i: (0, i, 0)),
            _resident_at((S5_WIDTH, S5_WIDTH), layer),
            _resident_at((1, S5_WIDTH), layer),
            _resident_at((D_MODEL, D_MODEL), layer),
        ]
        args += [*mix_args, *mix_w]
    if final:
        in_specs.append(_resident((1, D_MODEL)))
        args.append(final_w.reshape(1, D_MODEL))
    return pl.pallas_call(
        functools.partial(_ffn_kernel, mix, final, 6 * which),
        grid=(t // tm,),
        in_specs=in_specs,
        out_specs=pl.BlockSpec((tm, D_MODEL), tok),
        out_shape=jax.ShapeDtypeStruct((t, D_MODEL), F32),
        scratch_shapes=[pltpu.VMEM((tm, D_MODEL), BF16), pltpu.VMEM((tm, D_MODEL), F32)]
        + ([pltpu.VMEM((S5_WIDTH // LANES, tm, LANES), F32)] if mix else []),
        compiler_params=pltpu.CompilerParams(
            dimension_semantics=("arbitrary",), vmem_limit_bytes=VMEM_LIMIT),
        name="mix_ffn" if mix else "ffn",
    )(*args)


_C_Q, _C_K, _C_V = 0, DA_WIDTH, 2 * DA_WIDTH
_C_CQ = 3 * DA_WIDTH
_C_CKV = _C_CQ + MLA_Q_RANK
_C_U = _C_CKV + MLA_KV_RANK
_C_KPE = _C_U + S5_WIDTH
_NCOL = _C_KPE + LANES

_QSCALE_DIFF = DA_QK ** -0.5 * math.log2(math.e)
_QSCALE_MLA = (MLA_NOPE + MLA_ROPE) ** -0.5 * math.log2(math.e)


_INPROJ_PARTS = 2
_INPROJ_STAGGER = 6


def _rotate_half(x):
    half = DA_QK // 4
    lane = lax.broadcasted_iota(jnp.int32, x.shape, 1)
    return jnp.where(lane % (2 * half) < half, pltpu.roll(x, LANES - half, 1), pltpu.roll(x, half, 1))


def _inproj_kernel(rope, n_aliased, *refs):
    it = iter(refs)
    x_ref, mod_ref, nw_ref, w_ref, qnw_ref, kvnw_ref, wqup_ref, wk_ref, wv_ref = (next(it) for _ in range(9))
    if rope:
        cos_ref, sin_ref, cosm_ref, sinm_ref = (next(it) for _ in range(4))
    for _ in range(n_aliased):
        next(it)
    q_ref, k_ref, vt_ref, qm_ref, km_ref, vmt_ref, xg_ref = (next(it) for _ in range(7))
    if not rope:
        ka_ref, va_ref, ckv_ref, kpe_ref = (next(it) for _ in range(4))
    u_scr = next(it)

    tm = x_ref.shape[0]
    n_part = _INPROJ_PARTS
    rows_part = tm // n_part

    def part(p):
        r0 = p * rows_part
        rows = slice(r0, r0 + rows_part)
        h = _rms(x_ref[rows, :], nw_ref[...]) * (1.0 + mod_ref[0, 4:5, :]) + mod_ref[0, 3:4, :]
        h = h.astype(BF16)
        yield
        qkv = _dot(h, w_ref[:, :_C_CQ])
        yield
        rest = _dot(h, w_ref[:, _C_CQ:])
        yield

        def cols(start, width):
            src, off = (qkv, 0) if start < _C_CQ else (rest, _C_CQ)
            return src[:, start - off:start - off + width]

        if rope:
            cos, sin, cosm, sinm = (r[rows, :] for r in (cos_ref, sin_ref, cosm_ref, sinm_ref))
        for s in range(N_SLABS):
            qs = cols(_C_Q + s * LANES, LANES)
            ks = cols(_C_K + s * LANES, LANES)
            if rope:
                qs = qs * cos + _rotate_half(qs) * sin
                ks = ks * cos + _rotate_half(ks) * sin
            q_ref[s, rows, :] = (qs * _QSCALE_DIFF).astype(BF16)
            k_ref[s, rows, :] = ks.astype(BF16)
            vt_ref[s, :, rows] = cols(_C_V + s * LANES, LANES).T.astype(BF16)
            yield
        for gc in range(S5_WIDTH // LANES):
            u_scr[gc, rows, :] = cols(_C_U + gc * LANES, LANES)
        yield from _to_group_major(u_scr, xg_ref, r0, rows_part)

        cqn = _rms(cols(_C_CQ, MLA_Q_RANK), qnw_ref[...]).astype(BF16)
        qm = _dot(cqn, wqup_ref[...])
        ckvn = _rms(cols(_C_CKV, MLA_KV_RANK), kvnw_ref[...])
        ckvb = ckvn.astype(BF16)
        kn = _dot(ckvb, wk_ref[...])
        vm = _dot(ckvb, wv_ref[...])
        yield
        kpe = cols(_C_KPE, LANES)
        if rope:
            kpe = kpe * cosm + _rotate_half(kpe) * sinm
        for hd in range(MLA_HEADS):
            qh = qm[:, hd * LANES:(hd + 1) * LANES]
            if rope:
                qh = qh * cosm + _rotate_half(qh) * sinm
            qm_ref[hd, rows, :] = (qh * _QSCALE_MLA).astype(BF16)
            km_ref[hd, rows, :] = (kn[:, hd * LANES:(hd + 1) * LANES] + kpe).astype(BF16)
            yield
        for s in range(N_SLABS):
            vmt_ref[s, :, rows] = vm[:, s * LANES:(s + 1) * LANES].T.astype(BF16)

        if not rope:
            n_seq, seq = ka_ref.shape[:2]
            for b in range(r0 // seq, (r0 + rows_part) // seq):
                mine = slice(b * seq - r0, (b + 1) * seq - r0)
                ka_ref[b] = cols(_C_K, DA_WIDTH)[mine]
                va_ref[b] = cols(_C_V, DA_WIDTH)[mine]
                ckv_ref[b] = ckvn[mine]
                kpe_ref[b] = cols(_C_KPE, MLA_ROPE)[mine]

    def delayed(gen, steps):
        for _ in range(steps):
            yield
        yield from gen

    _interleave(delayed(part(p), p * _INPROJ_STAGGER) for p in range(n_part))


def _inproj_call(x, mod, seg_len, row_base, norm_w4, layer, w, mla_w, rope_tabs, cache_out=None):
    t = x.shape[0]
    tm = TOKEN_TILE
    rope = rope_tabs is not None
    qnw, kvnw, wqup, wk, wv = mla_w
    tok = lambda i: (i, 0)
    slab = lambda i: (0, i, 0)
    slab_t = lambda i: (0, 0, i)
    in_specs = [
        pl.BlockSpec((tm, D_MODEL), tok),
        _mod_spec(layer, seg_len, row_base),
        _resident_at((1, D_MODEL), layer, 1),
        _resident_at((D_MODEL, _NCOL), layer),
        _resident_at((1, MLA_Q_RANK), layer),
        _resident_at((1, MLA_KV_RANK), layer),
        _resident_at((MLA_Q_RANK, MLA_HEADS * LANES), layer),
        _resident_at((MLA_KV_RANK, MLA_HEADS * LANES), layer),
        _resident_at((MLA_KV_RANK, MLA_WIDTH), layer),
    ]
    args = [x, mod, norm_w4, w, qnw, kvnw, wqup, wk, wv]
    if rope:
        n_pos = rope_tabs[0].shape[0]
        pos = lambda i: (i % (n_pos // tm), 0)
        in_specs += [pl.BlockSpec((tm, LANES), pos)] * 4
        args += list(rope_tabs)
    out_shape = [
        jax.ShapeDtypeStruct((N_SLABS, t, LANES), BF16),
        jax.ShapeDtypeStruct((N_SLABS, t, LANES), BF16),
        jax.ShapeDtypeStruct((N_SLABS, LANES, t), BF16),
        jax.ShapeDtypeStruct((MLA_HEADS, t, LANES), BF16),
        jax.ShapeDtypeStruct((MLA_HEADS, t, LANES), BF16),
        jax.ShapeDtypeStruct((N_SLABS, LANES, t), BF16),
        jax.ShapeDtypeStruct((S5_GROUPS, t // S5_CHUNK, S5_CHUNK * S5_CH), BF16),
    ]
    out_specs = [
        pl.BlockSpec((N_SLABS, tm, LANES), slab),
        pl.BlockSpec((N_SLABS, tm, LANES), slab),
        pl.BlockSpec((N_SLABS, LANES, tm), slab_t),
        pl.BlockSpec((MLA_HEADS, tm, LANES), slab),
        pl.BlockSpec((MLA_HEADS, tm, LANES), slab),
        pl.BlockSpec((N_SLABS, LANES, tm), slab_t),
        pl.BlockSpec((S5_GROUPS, _ROWS_PER_TILE, S5_CHUNK * S5_CH), slab),
    ]
    aliases = {}
    if not rope:
        seq, stacked = cache_out
        n_first = len(out_shape)
        for k, arr in enumerate(stacked):
            out_shape.append(jax.ShapeDtypeStruct(arr.shape, F32))
            out_specs.append(pl.BlockSpec((tm // seq, None, seq, arr.shape[-1]), lambda i: (i, layer, 0, 0)))
            aliases[len(args)] = n_first + k
            in_specs.append(pl.BlockSpec(memory_space=pl.ANY))
            args.append(arr)
    return pl.pallas_call(
        functools.partial(_inproj_kernel, rope, len(aliases)),
        grid=(t // tm,),
        in_specs=in_specs,
        out_specs=out_specs,
        out_shape=out_shape,
        input_output_aliases=aliases,
        scratch_shapes=[pltpu.VMEM((S5_WIDTH // LANES, tm, LANES), F32)],
        compiler_params=pltpu.CompilerParams(
            dimension_semantics=("arbitrary",), vmem_limit_bytes=VMEM_LIMIT),
        name="inproj_dec" if rope else "inproj_ctx",
    )(*args)


def _mla_expand_kernel(ckv_ref, kpe_ref, wk_ref, wv_ref, km_ref, vmt_ref):
    ckvb = ckv_ref[...].astype(BF16)
    kn = _dot(ckvb, wk_ref[...])
    vm = _dot(ckvb, wv_ref[...])
    kpe = kpe_ref[...]
    for hd in range(MLA_HEADS):
        km_ref[hd] = (kn[:, hd * LANES:(hd + 1) * LANES] + kpe).astype(BF16)
    for s in range(N_SLABS):
        vmt_ref[s] = vm[:, s * LANES:(s + 1) * LANES].T.astype(BF16)


def _mla_expand_call(ckv, kpe128, wk, wv, layer):
    n_batch, _, rows, _ = ckv.shape
    t = n_batch * rows
    return pl.pallas_call(
        _mla_expand_kernel,
        grid=(n_batch,),
        in_specs=[
            pl.BlockSpec((None, None, rows, MLA_KV_RANK), lambda i: (i, layer, 0, 0)),
            pl.BlockSpec((None, None, rows, LANES), lambda i: (i, layer, 0, 0)),
            _resident_at((MLA_KV_RANK, MLA_HEADS * LANES), layer),
            _resident_at((MLA_KV_RANK, MLA_WIDTH), layer),
        ],
        out_specs=[
            pl.BlockSpec((MLA_HEADS, rows, LANES), lambda i: (0, i, 0)),
            pl.BlockSpec((N_SLABS, LANES, rows), lambda i: (0, 0, i)),
        ],
        out_shape=[
            jax.ShapeDtypeStruct((MLA_HEADS, t, LANES), BF16),
            jax.ShapeDtypeStruct((N_SLABS, LANES, t), BF16),
        ],
        compiler_params=pltpu.CompilerParams(dimension_semantics=("arbitrary",)),
        name="mla_expand",
    )(ckv, kpe128, wk, wv)


KEY_TILE = 256
N_SCORE_BUFS = 4


def _interleave(stages):
    stages = list(stages)
    while stages:
        for g in list(stages):
            try:
                next(g)
            except StopIteration:
                stages.remove(g)


def _attn_kernel(n_parts, lam_init, *refs):
    q_ref, qm_ref, lamp_ref, subw_ref = refs[:4]
    parts = [refs[4 + 4 * i:8 + 4 * i] for i in range(n_parts)]
    oa_ref, ob_ref, sbuf, qt_scr = refs[4 + 4 * n_parts:]

    lp = lamp_ref[...]
    lam = (jnp.exp(jnp.sum(lp[0:1] * lp[1:2], keepdims=True))
           - jnp.exp(jnp.sum(lp[2:3] * lp[3:4], keepdims=True)) + lam_init)
    tq = q_ref.shape[1]
    low_half = lax.broadcasted_iota(jnp.int32, (tq, LANES), 1) < DA_V
    channel = lax.broadcasted_iota(jnp.int32, (LANES, tq), 0)
    subw = subw_ref[...] * (1.0 - lam_init)
    sublanes = 8

    tiles, row = [], 0
    for p, (k, _, _, _) in enumerate(parts):
        for k0 in range(0, k.shape[1], KEY_TILE):
            tiles.append((p, k0, row))
            row += KEY_TILE

    def row_fold(op, acc, x):
        for j in range(x.shape[0] // sublanes):
            piece = x[j * sublanes:(j + 1) * sublanes, :]
            acc = piece if acc is None else op(acc, piece)
        return acc

    for s in range(N_SLABS):
        qt_scr[s] = q_ref[s].astype(F32).T.astype(BF16)
    for hd in range(MLA_HEADS):
        qt_scr[N_SLABS + hd] = qm_ref[hd].astype(F32).T.astype(BF16)

    def scores(unit):
        buf = unit['buf']
        if unit['mla']:
            rhs, kref, kidx = qt_scr[N_SLABS + unit['head']], 2, unit['head']
        else:
            qt = qt_scr[unit['slab']]
            rhs, kref, kidx = jnp.where(channel // DA_QK == unit['sub'], qt, jnp.zeros_like(qt)), 0, unit['slab']
        mx = None
        for p, k0, r0 in tiles:
            s = _dot(parts[p][kref][kidx, k0:k0 + KEY_TILE, :], rhs)
            sbuf[buf, r0:r0 + KEY_TILE, :] = s
            mx = row_fold(jnp.maximum, mx, s)
            yield
        unit['max'] = jnp.max(mx, axis=0, keepdims=True)

    def expo(unit):
        buf, tot = unit['buf'], None
        for _, _, r0 in tiles:
            e = jnp.exp2(sbuf[buf, r0:r0 + KEY_TILE, :] - unit['max'])
            sbuf[buf, r0:r0 + KEY_TILE, :] = e
            tot = row_fold(jnp.add, tot, e)
            yield
        unit['sum'] = jnp.sum(tot, axis=0, keepdims=True)

    def pv(head):
        units, slab = head['units'], head['slab']
        vref = 3 if units[0]['mla'] else 1
        l0 = units[0]['sum']
        if len(units) == 2:
            r = lam * l0 / units[1]['sum']
        acc = None
        for p, k0, r0 in tiles:
            w = sbuf[units[0]['buf'], r0:r0 + KEY_TILE, :]
            if len(units) == 2:
                w = w - r * sbuf[units[1]['buf'], r0:r0 + KEY_TILE, :]
            o = _dot(parts[p][vref][slab, :, k0:k0 + KEY_TILE], w.astype(BF16))
            acc = o if acc is None else acc + o
            yield
        head['out'] = (acc * (1.0 / l0)).T

    def finish(slab, mla, lo, hi):
        o = jnp.where(low_half, lo['out'], hi['out'])
        if mla:
            ob_ref[slab] = o.astype(BF16)
            return
        o2 = o * o
        ss_lo = jnp.sum(jnp.where(low_half, o2, 0.0), axis=1, keepdims=True)
        ss_hi = jnp.sum(jnp.where(low_half, 0.0, o2), axis=1, keepdims=True)
        ms = jnp.where(low_half, ss_lo, ss_hi) * (1.0 / DA_V)
        oa_ref[slab] = (o * lax.rsqrt(ms + EPS) * subw).astype(BF16)

    units, heads = [], []
    for slab in range(N_SLABS):
        for mla in (False, True):
            for half in range(2):
                mine = [dict(mla=mla, slab=slab, head=2 * slab + half, sub=2 * half + c)
                        for c in range(1 if mla else 2)]
                heads.append(dict(units=mine, slab=slab, mla=mla, half=half, last=len(units) + len(mine) - 1))
                units += mine
    for k, unit in enumerate(units):
        unit['buf'] = k % N_SCORE_BUFS

    pending = None
    for k in range(len(units) + 2):
        stages = []
        if k < len(units):
            stages.append(scores(units[k]))
        if 1 <= k <= len(units):
            stages.append(expo(units[k - 1]))
        ready = [h for h in heads if h['last'] == k - 2]
        if ready:
            stages.append(pv(ready[0]))
        _interleave(stages)
        if ready and ready[0]['half'] == 1:
            finish(ready[0]['slab'], ready[0]['mla'], pending[(ready[0]['slab'], ready[0]['mla'])], ready[0])
        elif ready:
            pending = dict(pending or {})
            pending[(ready[0]['slab'], ready[0]['mla'])] = ready[0]


def _attn_call(q, qm, lamp, subw, parts, n_batch, layer, lam_init):
    t = q.shape[1]
    seq = t // n_batch
    tq = min(Q_TILE, seq)
    nq = seq // tq
    qmap = lambda b, i: (0, b * nq + i, 0)
    in_specs = [
        pl.BlockSpec((N_SLABS, tq, LANES), qmap),
        pl.BlockSpec((MLA_HEADS, tq, LANES), qmap),
        pl.BlockSpec((None, 4, DA_QK), lambda b, i: (layer, 0, 0)),
        pl.BlockSpec((None, 1, LANES), lambda b, i: (layer, 0, 0)),
    ]
    args = [q, qm, lamp, subw]

    def keys_spec(arr, n_keys, transposed):
        block = (arr.shape[-3], LANES, n_keys) if transposed else (arr.shape[-3], n_keys, LANES)
        at = (lambda b: (0, 0, b)) if transposed else (lambda b: (0, b, 0))
        if arr.ndim == 4:
            return pl.BlockSpec((None,) + block, lambda b, i: (layer,) + at(b))
        return pl.BlockSpec(block, lambda b, i: at(b))

    for k, vt, km, vmt, n_keys in parts:
        in_specs += [keys_spec(k, n_keys, False), keys_spec(vt, n_keys, True),
                     keys_spec(km, n_keys, False), keys_spec(vmt, n_keys, True)]
        args += [k, vt, km, vmt]
    return pl.pallas_call(
        functools.partial(_attn_kernel, len(parts), lam_init),
        grid=(n_batch, nq),
        in_specs=in_specs,
        out_specs=[pl.BlockSpec((N_SLABS, tq, LANES), qmap)] * 2,
        out_shape=[jax.ShapeDtypeStruct((N_SLABS, t, LANES), BF16)] * 2,
        scratch_shapes=[pltpu.VMEM((N_SCORE_BUFS, sum(p[4] for p in parts), tq), F32),
                        pltpu.VMEM((N_SLABS + MLA_HEADS, LANES, tq), BF16)],
        compiler_params=pltpu.CompilerParams(
            dimension_semantics=("arbitrary", "arbitrary"), vmem_limit_bytes=VMEM_LIMIT),
        name="attn_dec" if len(parts) > 1 else "attn_ctx",
    )(*args)


def _s5_kernel(n_batch, n_chunks, x_ref, kmat_ref, bst_ref, cst_ref, lamt_ref, h0_ref, y_ref, fin_ref,
               s_scr, hin_scr):
    x = x_ref[...]
    y = _dot(x, kmat_ref[...])
    for k in range(4):
        s_scr[k] = _dot(x, bst_ref[k])

    a_fr, a_fi, a_br, a_bi = (lamt_ref[k] for k in range(4))

    def chunk_rows(j):
        return pl.ds(j, n_batch, stride=n_chunks)

    def step(j, carry):
        fr, fi, br, bi = carry
        rf, rb = chunk_rows(j), chunk_rows(n_chunks - 1 - j)
        hin_scr[0, rf, :] = fr
        hin_scr[1, rf, :] = fi
        hin_scr[2, rb, :] = br
        hin_scr[3, rb, :] = bi
        nfr = a_fr * fr - a_fi * fi + s_scr[0, rf, :]
        nfi = a_fr * fi + a_fi * fr + s_scr[1, rf, :]
        nbr = a_br * br - a_bi * bi + s_scr[2, rb, :]
        nbi = a_br * bi + a_bi * br + s_scr[3, rb, :]
        return nfr, nfi, nbr, nbi

    fin = lax.fori_loop(0, n_chunks, step, tuple(h0_ref[k] for k in range(4)))
    for k in range(4):
        fin_ref[k] = fin[k]
        y += _dot(hin_scr[k].astype(BF16), cst_ref[k])
    y_ref[...] = y


def _s5_call(xg, kmat, bst, cst, lamt, h0, n_batch, layer, h0_layer):
    rows = xg.shape[1]
    n_chunks = rows // n_batch
    wide = S5_CHUNK * S5_CH
    grp = lambda g: (g, 0, 0)
    grp4 = lambda g: (g, 0, 0, 0)
    return pl.pallas_call(
        functools.partial(_s5_kernel, n_batch, n_chunks),
        grid=(S5_GROUPS,),
        in_specs=[
            pl.BlockSpec((None, rows, wide), grp),
            pl.BlockSpec((None, None, wide, wide), lambda g: (layer, g, 0, 0)),
            pl.BlockSpec((None, None, 4, wide, LANES), lambda g: (layer, g, 0, 0, 0)),
            pl.BlockSpec((None, None, 4, LANES, wide), lambda g: (layer, g, 0, 0, 0)),
            pl.BlockSpec((None, None, 4, 1, LANES), lambda g: (layer, g, 0, 0, 0)),
            pl.BlockSpec((None, None, 4, n_batch, LANES), lambda g: (h0_layer, g, 0, 0, 0)),
        ],
        out_specs=[
            pl.BlockSpec((None, rows, wide), grp),
            pl.BlockSpec((None, 4, n_batch, LANES), grp4),
        ],
        out_shape=[
            jax.ShapeDtypeStruct((S5_GROUPS, rows, wide), F32),
            jax.ShapeDtypeStruct((S5_GROUPS, 4, n_batch, LANES), F32),
        ],
        scratch_shapes=[pltpu.VMEM((4, rows, LANES), F32), pltpu.VMEM((4, rows, LANES), F32)],
        compiler_params=pltpu.CompilerParams(dimension_semantics=("arbitrary",)),
        name="s5",
    )(xg, kmat, bst, cst, lamt, h0)


def _s5_operators(a_re, a_im, log_step, b_re, b_im, c_re, c_im, d):
    hi = lax.Precision.HIGHEST
    n = S5_CHUNK
    wide = n * S5_CH
    ar, ai = a_re.astype(F32), a_im.astype(F32)
    step = jnp.exp(log_step.astype(F32))[..., None]
    zr, zi = ar * step, ai * step
    k = jnp.arange(n + 1, dtype=F32)[:, None, None, None]
    mag = jnp.exp(k * zr[None])
    pr, pi = mag * jnp.cos(k * zi[None]), mag * jnp.sin(k * zi[None])

    xr, xi = pr[1] - 1.0, pi[1]
    den = ar * ar + ai * ai
    fr, fi = ((xr * ar + xi * ai) / den)[:, :, None, :], ((xi * ar - xr * ai) / den)[:, :, None, :]
    br, bi = (jnp.swapaxes(b.astype(F32), 2, 3) for b in (b_re, b_im))
    bbr, bbi = fr * br - fi * bi, fr * bi + fi * br
    cr, ci = c_re.astype(F32), c_im.astype(F32)

    tt = np.arange(n)

    def state_readout(powers, dirn):
        qr = pr[powers, dirn].transpose(1, 2, 0)[:, :, :, None]
        qi = pi[powers, dirn].transpose(1, 2, 0)[:, :, :, None]
        ctr = jnp.swapaxes(cr[dirn], 1, 2)[:, :, None, :]
        cti = jnp.swapaxes(ci[dirn], 1, 2)[:, :, None, :]
        rr = ctr * qr - cti * qi
        ri = -(ctr * qi + cti * qr)
        return rr.reshape(S5_GROUPS, S5_STATE, wide), ri.reshape(S5_GROUPS, S5_STATE, wide)

    def lag_response(powers, dirn):
        rr, ri = state_readout(powers, dirn)
        return (jnp.einsum('gkp,gpn->gkn', bbr[dirn], rr, precision=hi)
                + jnp.einsum('gkp,gpn->gkn', bbi[dirn], ri, precision=hi))

    fwd, bwd = lag_response(tt, 0), lag_response(n - 1 - tt, 1)
    skip = d.astype(F32)[:, None, :] * jnp.eye(S5_CH, dtype=F32)[None]
    strip = jnp.concatenate([bwd[:, :, :wide - S5_CH], bwd[:, :, wide - S5_CH:] + fwd[:, :, :S5_CH] + skip,
                             fwd[:, :, S5_CH:]], axis=-1)
    kmat = jnp.stack([strip[:, :, (n - 1 - s) * S5_CH:(n - 1 - s) * S5_CH + wide] for s in range(n)], axis=1)
    kmat = kmat.reshape(S5_GROUPS, wide, wide)

    def edge_states(powers, dirn):
        qr = jnp.swapaxes(pr[powers, dirn], 0, 1)[:, :, None, :]
        qi = jnp.swapaxes(pi[powers, dirn], 0, 1)[:, :, None, :]
        er = qr * bbr[dirn][:, None] - qi * bbi[dirn][:, None]
        ei = qr * bbi[dirn][:, None] + qi * bbr[dirn][:, None]
        return er.reshape(S5_GROUPS, wide, S5_STATE), ei.reshape(S5_GROUPS, wide, S5_STATE)

    bst = jnp.stack([*edge_states(n - 1 - tt, 0), *edge_states(tt, 1)], axis=1)
    cst = jnp.stack([*state_readout(tt + 1, 0), *state_readout(n - tt, 1)], axis=1)
    lamt = jnp.stack([pr[n, 0], pi[n, 0], pr[n, 1], pi[n, 1]], axis=1)
    zpad = LANES - S5_STATE
    bst = jnp.pad(bst, ((0, 0), (0, 0), (0, 0), (0, zpad)))
    cst = jnp.pad(cst, ((0, 0), (0, 0), (0, zpad), (0, 0)))
    lamt = jnp.pad(lamt, ((0, 0), (0, 0), (0, zpad)))
    return kmat.astype(BF16), bst.astype(BF16), cst.astype(BF16), lamt[:, :, None, :]


def _rope_tables(n_pos):
    n = DA_QK // 4
    inv = ROPE_THETA ** (-jnp.arange(n, dtype=F32) / n)
    pos = jnp.arange(n_pos, dtype=jnp.int32)
    ang_r = (pos // GRID_W).astype(F32)[:, None] * inv[None, :]
    ang_c = (pos % GRID_W).astype(F32)[:, None] * inv[None, :]
    cos32 = jnp.concatenate([jnp.cos(ang_r)] * 2 + [jnp.cos(ang_c)] * 2, axis=1)
    sin32 = jnp.concatenate([-jnp.sin(ang_r), jnp.sin(ang_r), -jnp.sin(ang_c), jnp.sin(ang_c)], axis=1)
    pad = LANES - MLA_ROPE
    cosm = jnp.concatenate([cos32, jnp.ones((n_pos, pad), F32)], axis=1)
    sinm = jnp.concatenate([sin32, jnp.zeros((n_pos, pad), F32)], axis=1)
    return jnp.tile(cos32, (1, 4)), jnp.tile(sin32, (1, 4)), cosm, sinm


def _prep_inproj(w):
    o = np.cumsum([0, DA_WIDTH, DA_WIDTH, DA_WIDTH, MLA_Q_RANK, MLA_KV_RANK, MLA_ROPE, S5_WIDTH])
    wq, wk, wv, wcq, wckv, wkpe, wu = (w[..., o[i]:o[i + 1]] for i in range(7))
    wkpe = jnp.pad(wkpe, ((0, 0), (0, 0), (0, LANES - MLA_ROPE)))
    return jnp.concatenate([wq, wk, wv, wcq, wckv, wu, wkpe], axis=-1).astype(BF16)


def _prep_mla(q_norm_w, w_q_up, kv_norm_w, w_kv_up):
    pad = LANES - MLA_ROPE - MLA_NOPE
    wq = w_q_up.reshape(DEPTH, MLA_Q_RANK, MLA_HEADS, MLA_NOPE + MLA_ROPE)
    wqup = jnp.pad(jnp.concatenate([wq[..., MLA_NOPE:], wq[..., :MLA_NOPE]], axis=-1),
                   ((0, 0), (0, 0), (0, 0), (0, pad))).reshape(DEPTH, MLA_Q_RANK, MLA_HEADS * LANES)
    wkv = w_kv_up.reshape(DEPTH, MLA_KV_RANK, MLA_HEADS, MLA_NOPE + MLA_V)
    wk = jnp.pad(wkv[..., :MLA_NOPE], ((0, 0), (0, 0), (0, 0), (MLA_ROPE, pad)))
    wk = wk.reshape(DEPTH, MLA_KV_RANK, MLA_HEADS * LANES)
    wv = wkv[..., MLA_NOPE:].reshape(DEPTH, MLA_KV_RANK, MLA_WIDTH)
    return (q_norm_w.reshape(DEPTH, 1, MLA_Q_RANK), kv_norm_w.reshape(DEPTH, 1, MLA_KV_RANK),
            wqup.astype(BF16), wk.astype(BF16), wv.astype(BF16))


def _cached_slabs(cache, transposed):
    b, _, n = cache.shape[:3]
    a = cache.reshape(b, DEPTH, n, N_SLABS, LANES).astype(BF16)
    a = a.transpose(1, 3, 4, 0, 2) if transposed else a.transpose(1, 3, 0, 2, 4)
    return a.reshape(DEPTH, N_SLABS, LANES, b * n) if transposed else a.reshape(DEPTH, N_SLABS, b * n, LANES)


def kernel(x_prompt, x_sample, cache_diff_k, cache_diff_v, cache_mla_ckv, cache_mla_kpe, state_s5_re, state_s5_im, c, c_ctx, w_ada, b_ada, norm_w, ffn_w_in, ffn_w_out, w_in, diff_lambda, diff_subln_w, mla_q_norm_w, mla_w_q_up, mla_kv_norm_w, mla_w_kv_up, s5_a_re, s5_a_im, s5_log_step, s5_b_re, s5_b_im, s5_c_re, s5_c_im, s5_d, s5_w_glu, s5_b_glu, w_out, final_norm_w):
    n_ctx, seq, _ = x_prompt.shape
    n_dec, dec_seq, _ = x_sample.shape
    past = cache_diff_k.shape[2]

    cvec = jnp.concatenate(
        [c_ctx[None, :], c, jnp.zeros((MOD_ROWS - 1 - n_dec, D_MODEL), F32)], axis=0)
    mod = _adaln(cvec, w_ada, b_ada)
    rope_tabs = _rope_tables(dec_seq)

    groups = {
        'ctx': dict(x=x_prompt.reshape(n_ctx * seq, D_MODEL), seg=n_ctx * seq, row=0, nb=n_ctx),
        'dec': dict(x=x_sample.reshape(n_dec * dec_seq, D_MODEL), seg=dec_seq, row=1, nb=n_dec),
    }
    ffn_w = (ffn_w_in.astype(BF16), ffn_w_out.astype(BF16))
    norm_w4 = norm_w.reshape(DEPTH, 3, 1, D_MODEL)
    w_proj = _prep_inproj(w_in)
    mla_w = _prep_mla(mla_q_norm_w, mla_w_q_up, mla_kv_norm_w, mla_w_kv_up)
    mix_w = (s5_w_glu.astype(BF16), s5_b_glu.reshape(DEPTH, 1, S5_WIDTH), w_out.astype(BF16))
    s5_ops = jax.vmap(_s5_operators)(s5_a_re, s5_a_im, s5_log_step, s5_b_re, s5_b_im, s5_c_re, s5_c_im, s5_d)
    subw = jnp.tile(diff_subln_w, (1, 2)).reshape(DEPTH, 1, LANES)
    ctx_k, ctx_vt = _cached_slabs(cache_diff_k, False), _cached_slabs(cache_diff_v, True)
    ctx_kpe = jnp.pad(cache_mla_kpe, ((0, 0), (0, 0), (0, 0), (0, LANES - MLA_ROPE)))
    h0_dec = jnp.stack([state_s5_re[:, :, 0], state_s5_im[:, :, 0], state_s5_re[:, :, 1], state_s5_im[:, :, 1]], 0)
    h0_dec = jnp.pad(h0_dec.transpose(2, 3, 0, 1, 4), ((0, 0),) * 4 + ((0, LANES - S5_STATE),))
    h0_ctx = jnp.zeros((1, S5_GROUPS, 4, n_ctx, LANES), F32)
    caches = tuple(jnp.zeros((n_ctx, DEPTH, seq, width), F32)
                   for width in (DA_WIDTH, DA_WIDTH, MLA_KV_RANK, MLA_ROPE))
    s5_states = []
    for l in range(DEPTH):
        lam_init = 0.8 - 0.6 * math.exp(-0.3 * l)
        final_w = final_norm_w if l == DEPTH - 1 else None
        for name, g in groups.items():
            dec = name == 'dec'
            x = _ffn_call(g['x'], mod, g['seg'], g['row'], norm_w4, l, 0, ffn_w)
            proj = _inproj_call(x, mod, g['seg'], g['row'], norm_w4, l, w_proj, mla_w,
                                rope_tabs if dec else None, cache_out=None if dec else (seq, caches))
            q, k, vt, qm, km, vmt, xg = proj[:7]
            parts = [(k, vt, km, vmt, g['x'].shape[0] // g['nb'])]
            if dec:
                ckm, cvmt = _mla_expand_call(cache_mla_ckv, ctx_kpe, mla_w[3], mla_w[4], l)
                parts.append((ctx_k, ctx_vt, ckm, cvmt, past))
            o_a, o_b = _attn_call(q, qm, diff_lambda, subw, parts, g['nb'], l, lam_init)
            y_s5, fin = _s5_call(xg, *s5_ops, h0_dec if dec else h0_ctx, g['nb'], l, l if dec else 0)
            g['x'] = _ffn_call(x, mod, g['seg'], g['row'], norm_w4, l, 1, ffn_w,
                               mix_args=(o_a, o_b, y_s5), mix_w=mix_w, final_w=final_w)
            if not dec:
                caches = proj[7:]
                fin = fin[..., :S5_STATE].transpose(1, 2, 0, 3)
                s5_states.append((jnp.stack([fin[0], fin[2]], axis=1), jnp.stack([fin[1], fin[3]], axis=1)))

    y_prompt = groups['ctx']['x'].reshape(n_ctx, seq, D_MODEL)
    y_sample = groups['dec']['x'].reshape(n_dec, dec_seq, D_MODEL)
    new_k, new_v, new_ckv, new_kpe = caches
    new_s5_re = jnp.stack([t[0] for t in s5_states], axis=1)
    new_s5_im = jnp.stack([t[1] for t in s5_states], axis=1)
    return (y_prompt, y_sample,
            new_k.reshape(n_ctx, DEPTH, seq, DA_HEADS, 2 * DA_QK),
            new_v.reshape(n_ctx, DEPTH, seq, DA_HEADS, DA_V),
            new_ckv, new_kpe, new_s5_re, new_s5_im)
```

```python
import functools
import math

import jax
import jax.numpy as jnp
import numpy as np
from jax import lax
from jax.experimental import pallas as pl
from jax.experimental.pallas import tpu as pltpu

F32 = jnp.float32
BF16 = jnp.bfloat16

D_MODEL = 1024
DEPTH = 2
GRID_W = 64
ROPE_THETA = 10000.0
EPS = 1e-6
N_MOD = 9
D_FF = 2816
DA_HEADS = 6
DA_QK = 32
DA_V = 64
DA_WIDTH = DA_HEADS * DA_V
MLA_HEADS = 6
MLA_Q_RANK = 256
MLA_KV_RANK = 128
MLA_NOPE = 64
MLA_ROPE = 32
MLA_V = 64
MLA_WIDTH = MLA_HEADS * MLA_V
S5_WIDTH = D_MODEL - DA_WIDTH - MLA_WIDTH
S5_CH = 16
S5_GROUPS = S5_WIDTH // S5_CH
S5_STATE = 64

LANES = 128
FF_CHUNK = 256
N_FF_CHUNKS = D_FF // FF_CHUNK
TOKEN_TILE = 512
Q_TILE = 256
S5_CHUNK = 16
N_SLABS = DA_WIDTH // LANES
MOD_ROWS = 8
VMEM_LIMIT = 56 * 1024 * 1024


def _rms(x, w):
    return x * lax.rsqrt(jnp.mean(x * x, axis=-1, keepdims=True) + EPS) * w


def _silu(x):
    return x * jax.nn.sigmoid(x)


def _gelu_tanh(x):
    c = math.sqrt(2.0 / math.pi)
    return x * (0.5 * (1.0 + jnp.tanh(c * (x + 0.044715 * (x * x * x)))))


def _dot(a, b):
    return jnp.dot(a, b, preferred_element_type=F32)


def _resident(shape):
    nd = len(shape)
    return pl.BlockSpec(shape, lambda *_: (0,) * nd, pipeline_mode=pl.Buffered(1))


def _resident_at(shape, *lead):
    nd = len(shape)
    return pl.BlockSpec((None,) * len(lead) + tuple(shape), lambda *_: tuple(lead) + (0,) * nd,
                        pipeline_mode=pl.Buffered(1))


def _adaln_kernel(c_ref, w_ref, b_ref, o_ref):
    s = _silu(c_ref[...])
    o_ref[...] = jnp.dot(s, w_ref[...], preferred_element_type=F32,
                         precision=lax.Precision.HIGHEST) + b_ref[...]


def _adaln(cvec, w_ada, b_ada):
    n = N_MOD * D_MODEL
    tn = 1152
    out = pl.pallas_call(
        _adaln_kernel,
        grid=(DEPTH, n // tn),
        in_specs=[
            pl.BlockSpec((MOD_ROWS, D_MODEL), lambda l, j: (0, 0)),
            pl.BlockSpec((None, D_MODEL, tn), lambda l, j: (l, 0, j)),
            pl.BlockSpec((None, 1, tn), lambda l, j: (l, 0, j)),
        ],
        out_specs=pl.BlockSpec((None, MOD_ROWS, tn), lambda l, j: (l, 0, j)),
        out_shape=jax.ShapeDtypeStruct((DEPTH, MOD_ROWS, n), F32),
        compiler_params=pltpu.CompilerParams(
            dimension_semantics=("arbitrary", "arbitrary"), vmem_limit_bytes=VMEM_LIMIT),
        name="adaln",
    )(cvec, w_ada, b_ada.reshape(DEPTH, 1, n))
    return out.reshape(DEPTH, MOD_ROWS, N_MOD, D_MODEL)


_ROWS_PER_TILE = TOKEN_TILE // S5_CHUNK
_SLOTS = LANES // S5_CH


def _to_group_major(u_scr, xg_ref, row0, n_rows):
    c0, n_chunks = row0 // S5_CHUNK, n_rows // S5_CHUNK
    slot = lax.broadcasted_iota(jnp.int32, (n_chunks, LANES), 1) // S5_CH
    for g in range(S5_GROUPS):
        for tt in range(S5_CHUNK // _SLOTS):
            acc = None
            for j in range(_SLOTS):
                t = tt * _SLOTS + j
                src = u_scr[g // _SLOTS, pl.ds(row0 + t, n_chunks, stride=S5_CHUNK), :]
                shift = ((j - g % _SLOTS) % _SLOTS) * S5_CH
                piece = pltpu.roll(src, shift, 1) if shift else src
                acc = piece if acc is None else jnp.where(slot == j, piece, acc)
            xg_ref[g, c0:c0 + n_chunks, tt * LANES:(tt + 1) * LANES] = acc.astype(BF16)
        yield


def _from_group_major(yg_ref, y_scr, row0, n_rows):
    c0, n_chunks = row0 // S5_CHUNK, n_rows // S5_CHUNK
    slot = lax.broadcasted_iota(jnp.int32, (n_chunks, LANES), 1) // S5_CH
    for t in range(S5_CHUNK):
        for gc in range(S5_GROUPS // _SLOTS):
            acc = None
            for m in range(_SLOTS):
                src = yg_ref[gc * _SLOTS + m, c0:c0 + n_chunks, (t // _SLOTS) * LANES:(t // _SLOTS + 1) * LANES]
                shift = ((m - t % _SLOTS) % _SLOTS) * S5_CH
                piece = pltpu.roll(src, shift, 1) if shift else src
                acc = piece if acc is None else jnp.where(slot == m, piece, acc)
            y_scr[gc, pl.ds(row0 + t, n_chunks, stride=S5_CHUNK), :] = acc
        yield


def _interleave(stages):
    stages = list(stages)
    while stages:
        for g in list(stages):
            try:
                next(g)
            except StopIteration:
                stages.remove(g)


def _delayed(gen, steps):
    for _ in range(steps):
        yield
    yield from gen


_MIX_PARTS = 2
_MIX_STAGGER = 8


def _ffn_kernel(mix, final, mod_base, *refs):
    it = iter(refs)
    x_ref, mod_ref, nw_ref, wi_ref, wo_ref = (next(it) for _ in range(5))
    if mix:
        oa_ref, ob_ref, ys_ref, wglu_ref, bglu_ref, wout_ref = (next(it) for _ in range(6))
    if final:
        fnw_ref = next(it)
    out_ref, h_scr, acc_scr = next(it), next(it), next(it)

    if mix:
        y_scr = next(it)
    shift = mod_ref[0, mod_base:mod_base + 1, :]
    scale = mod_ref[0, mod_base + 1:mod_base + 2, :]
    gate = mod_ref[0, mod_base + 2:mod_base + 3, :]
    tm = x_ref.shape[0]
    n_part = _MIX_PARTS if mix else 1
    rows_part = tm // n_part

    def prologue(p):
        r0 = p * rows_part
        rows = slice(r0, r0 + rows_part)
        x = x_ref[rows, :]
        if mix:
            yield from _from_group_major(ys_ref, y_scr, r0, rows_part)
            yc = _gelu_tanh(jnp.concatenate([y_scr[gc, rows, :] for gc in range(S5_WIDTH // LANES)], axis=1))
            oc = yc * jax.nn.sigmoid(_dot(yc.astype(BF16), wglu_ref[...]) + bglu_ref[...])
            yield
            mixed = jnp.concatenate([oa_ref[s, rows, :] for s in range(N_SLABS)]
                                    + [ob_ref[s, rows, :] for s in range(N_SLABS)]
                                    + [oc.astype(BF16)], axis=1)
            x = x + mod_ref[0, mod_base - 1:mod_base, :] * _dot(mixed, wout_ref[...])
            out_ref[rows, :] = x
            yield
        h = _rms(x, nw_ref[...]) * (1.0 + scale) + shift
        h_scr[rows, :] = h.astype(BF16)
        yield

    _interleave(_delayed(prologue(p), p * _MIX_STAGGER) for p in range(n_part))
    acc_scr[...] = jnp.zeros_like(acc_scr)

    for j in range(N_FF_CHUNKS):
        hb = h_scr[...]
        col = j * FF_CHUNK
        a = _dot(hb, wi_ref[:, col:col + FF_CHUNK])
        g = _dot(hb, wi_ref[:, D_FF + col:D_FF + col + FF_CHUNK])
        acc_scr[...] += _dot((_silu(g) * a).astype(BF16), wo_ref[col:col + FF_CHUNK, :])
    y = (out_ref[...] if mix else x_ref[...]) + 0.5 * gate * acc_scr[...]
    if final:
        y = _rms(y, fnw_ref[...])
    out_ref[...] = y


def _mod_spec(layer, seg_len, row_base):
    return pl.BlockSpec((None, 1, N_MOD, D_MODEL),
                        lambda i: (layer, row_base + (i * TOKEN_TILE) // seg_len, 0, 0))


def _ffn_call(x, mod, seg_len, row_base, norm_w4, layer, which, ffn_w, mix_args=None, mix_w=None, final_w=None):
    t = x.shape[0]
    tm = TOKEN_TILE
    mix = mix_args is not None
    final = final_w is not None
    w_in_all, w_out_all = ffn_w
    tok = lambda i: (i, 0)
    in_specs = [
        pl.BlockSpec((tm, D_MODEL), tok),
        _mod_spec(layer, seg_len, row_base),
        _resident_at((1, D_MODEL), layer, 2 * which),
        _resident_at((D_MODEL, 2 * D_FF), layer, which),
        _resident_at((D_FF, D_MODEL), layer, which),
    ]
    args = [x, mod, norm_w4, w_in_all, w_out_all]
    if mix:
        in_specs += [
            pl.BlockSpec((N_SLABS, tm, LANES), lambda i: (0, i, 0)),
            pl.BlockSpec((N_SLABS, tm, LANES), lambda i: (0, i, 0)),
            pl.BlockSpec((S5_GROUPS, _ROWS_PER_TILE, S5_CHUNK * S5_CH), lambda i: (0, i, 0)),
            _resident_at((S5_WIDTH, S5_WIDTH), layer),
            _resident_at((1, S5_WIDTH), layer),
            _resident_at((D_MODEL, D_MODEL), layer),
        ]
        args += [*mix_args, *mix_w]
    if final:
        in_specs.append(_resident((1, D_MODEL)))
        args.append(final_w.reshape(1, D_MODEL))
    return pl.pallas_call(
        functools.partial(_ffn_kernel, mix, final, 6 * which),
        grid=(t // tm,),
        in_specs=in_specs,
        out_specs=pl.BlockSpec((tm, D_MODEL), tok),
        out_shape=jax.ShapeDtypeStruct((t, D_MODEL), F32),
        scratch_shapes=[pltpu.VMEM((tm, D_MODEL), BF16), pltpu.VMEM((tm, D_MODEL), F32)]
        + ([pltpu.VMEM((S5_WIDTH // LANES, tm, LANES), F32)] if mix else []),
        compiler_params=pltpu.CompilerParams(
            dimension_semantics=("arbitrary",), vmem_limit_bytes=VMEM_LIMIT),
        name="mix_ffn" if mix else "ffn",
    )(*args)


_C_Q, _C_K, _C_V = 0, DA_WIDTH, 2 * DA_WIDTH
_C_CQ = 3 * DA_WIDTH
_C_CKV = _C_CQ + MLA_Q_RANK
_C_U = _C_CKV + MLA_KV_RANK
_C_KPE = _C_U + S5_WIDTH
_NCOL = _C_KPE + LANES

_QSCALE_DIFF = DA_QK ** -0.5 * math.log2(math.e)
_QSCALE_MLA = (MLA_NOPE + MLA_ROPE) ** -0.5 * math.log2(math.e)


_INPROJ_PARTS = 2
_INPROJ_STAGGER = 6


def _rotate_half(x):
    half = DA_QK // 4
    lane = lax.broadcasted_iota(jnp.int32, x.shape, 1)
    return jnp.where(lane % (2 * half) < half, pltpu.roll(x, LANES - half, 1), pltpu.roll(x, half, 1))


def _inproj_kernel(rope, n_aliased, *refs):
    it = iter(refs)
    x_ref, mod_ref, nw_ref, w_ref, qnw_ref, kvnw_ref, wqup_ref, wk_ref, wv_ref = (next(it) for _ in range(9))
    if rope:
        cos_ref, sin_ref, cosm_ref, sinm_ref = (next(it) for _ in range(4))
    for _ in range(n_aliased):
        next(it)
    q_ref, k_ref, vt_ref, qm_ref, km_ref, vmt_ref, xg_ref = (next(it) for _ in range(7))
    if not rope:
        ka_ref, va_ref, ckv_ref, kpe_ref = (next(it) for _ in range(4))
    u_scr = next(it)

    tm = x_ref.shape[0]
    n_part = _INPROJ_PARTS
    rows_part = tm // n_part

    def part(p):
        r0 = p * rows_part
        rows = slice(r0, r0 + rows_part)
        h = _rms(x_ref[rows, :], nw_ref[...]) * (1.0 + mod_ref[0, 4:5, :]) + mod_ref[0, 3:4, :]
        h = h.astype(BF16)
        yield
        qkv = _dot(h, w_ref[:, :_C_CQ])
        yield
        rest = _dot(h, w_ref[:, _C_CQ:])
        yield

        def cols(start, width):
            src, off = (qkv, 0) if start < _C_CQ else (rest, _C_CQ)
            return src[:, start - off:start - off + width]

        if rope:
            cos, sin, cosm, sinm = (r[rows, :] for r in (cos_ref, sin_ref, cosm_ref, sinm_ref))
        for s in range(N_SLABS):
            qs = cols(_C_Q + s * LANES, LANES)
            ks = cols(_C_K + s * LANES, LANES)
            if rope:
                qs = qs * cos + _rotate_half(qs) * sin
                ks = ks * cos + _rotate_half(ks) * sin
            q_ref[s, rows, :] = (qs * _QSCALE_DIFF).astype(BF16)
            k_ref[s, rows, :] = ks.astype(BF16)
            vt_ref[s, :, rows] = cols(_C_V + s * LANES, LANES).T.astype(BF16)
            yield
        for gc in range(S5_WIDTH // LANES):
            u_scr[gc, rows, :] = cols(_C_U + gc * LANES, LANES)
        yield from _to_group_major(u_scr, xg_ref, r0, rows_part)

        cqn = _rms(cols(_C_CQ, MLA_Q_RANK), qnw_ref[...]).astype(BF16)
        qm = _dot(cqn, wqup_ref[...])
        ckvn = _rms(cols(_C_CKV, MLA_KV_RANK), kvnw_ref[...])
        ckvb = ckvn.astype(BF16)
        kn = _dot(ckvb, wk_ref[...])
        vm = _dot(ckvb, wv_ref[...])
        yield
        kpe = cols(_C_KPE, LANES)
        if rope:
            kpe = kpe * cosm + _rotate_half(kpe) * sinm
        for hd in range(MLA_HEADS):
            qh = qm[:, hd * LANES:(hd + 1) * LANES]
            if rope:
                qh = qh * cosm + _rotate_half(qh) * sinm
            qm_ref[hd, rows, :] = (qh * _QSCALE_MLA).astype(BF16)
            km_ref[hd, rows, :] = (kn[:, hd * LANES:(hd + 1) * LANES] + kpe).astype(BF16)
            yield
        for s in range(N_SLABS):
            vmt_ref[s, :, rows] = vm[:, s * LANES:(s + 1) * LANES].T.astype(BF16)

        if not rope:
            seq = ka_ref.shape[-2]
            new = (cols(_C_K, DA_WIDTH), cols(_C_V, DA_WIDTH), ckvn, cols(_C_KPE, MLA_ROPE))
            for b in range(r0 // seq, (r0 + rows_part) // seq):
                mine = slice(b * seq - r0, (b + 1) * seq - r0)
                for ref, val in zip((ka_ref, va_ref, ckv_ref, kpe_ref), new):
                    if len(ref.shape) == 3:
                        ref[b] = val[mine]
                    else:
                        ref[b, 0] = val[mine]
                        for later in range(1, ref.shape[1]):
                            ref[b, later] = jnp.zeros_like(val[mine])

    _interleave(_delayed(part(p), p * _INPROJ_STAGGER) for p in range(n_part))


def _inproj_call(x, mod, seg_len, row_base, norm_w4, layer, w, mla_w, rope_tabs, cache_out=None):
    t = x.shape[0]
    tm = TOKEN_TILE
    rope = rope_tabs is not None
    qnw, kvnw, wqup, wk, wv = mla_w
    tok = lambda i: (i, 0)
    slab = lambda i: (0, i, 0)
    slab_t = lambda i: (0, 0, i)
    in_specs = [
        pl.BlockSpec((tm, D_MODEL), tok),
        _mod_spec(layer, seg_len, row_base),
        _resident_at((1, D_MODEL), layer, 1),
        _resident_at((D_MODEL, _NCOL), layer),
        _resident_at((1, MLA_Q_RANK), layer),
        _resident_at((1, MLA_KV_RANK), layer),
        _resident_at((MLA_Q_RANK, MLA_HEADS * LANES), layer),
        _resident_at((MLA_KV_RANK, MLA_HEADS * LANES), layer),
        _resident_at((MLA_KV_RANK, MLA_WIDTH), layer),
    ]
    args = [x, mod, norm_w4, w, qnw, kvnw, wqup, wk, wv]
    if rope:
        n_pos = rope_tabs[0].shape[0]
        pos = lambda i: (i % (n_pos // tm), 0)
        in_specs += [pl.BlockSpec((tm, LANES), pos)] * 4
        args += list(rope_tabs)
    out_shape = [
        jax.ShapeDtypeStruct((N_SLABS, t, LANES), BF16),
        jax.ShapeDtypeStruct((N_SLABS, t, LANES), BF16),
        jax.ShapeDtypeStruct((N_SLABS, LANES, t), BF16),
        jax.ShapeDtypeStruct((MLA_HEADS, t, LANES), BF16),
        jax.ShapeDtypeStruct((MLA_HEADS, t, LANES), BF16),
        jax.ShapeDtypeStruct((N_SLABS, LANES, t), BF16),
        jax.ShapeDtypeStruct((S5_GROUPS, t // S5_CHUNK, S5_CHUNK * S5_CH), BF16),
    ]
    out_specs = [
        pl.BlockSpec((N_SLABS, tm, LANES), slab),
        pl.BlockSpec((N_SLABS, tm, LANES), slab),
        pl.BlockSpec((N_SLABS, LANES, tm), slab_t),
        pl.BlockSpec((MLA_HEADS, tm, LANES), slab),
        pl.BlockSpec((MLA_HEADS, tm, LANES), slab),
        pl.BlockSpec((N_SLABS, LANES, tm), slab_t),
        pl.BlockSpec((S5_GROUPS, _ROWS_PER_TILE, S5_CHUNK * S5_CH), slab),
    ]
    aliases = {}
    if not rope:
        seq, stacked = cache_out
        n_first = len(out_shape)
        for k, width in enumerate((DA_WIDTH, DA_WIDTH, MLA_KV_RANK, MLA_ROPE)):
            out_shape.append(jax.ShapeDtypeStruct((t // seq, DEPTH, seq, width), F32))
            if stacked is None:
                out_specs.append(pl.BlockSpec((tm // seq, DEPTH, seq, width), lambda i: (i, 0, 0, 0)))
            else:
                out_specs.append(pl.BlockSpec((tm // seq, None, seq, width), lambda i: (i, layer, 0, 0)))
                aliases[len(args)] = n_first + k
                in_specs.append(pl.BlockSpec(memory_space=pl.ANY))
                args.append(stacked[k])
    return pl.pallas_call(
        functools.partial(_inproj_kernel, rope, len(aliases)),
        grid=(t // tm,),
        in_specs=in_specs,
        out_specs=out_specs,
        out_shape=out_shape,
        input_output_aliases=aliases,
        scratch_shapes=[pltpu.VMEM((S5_WIDTH // LANES, tm, LANES), F32)],
        compiler_params=pltpu.CompilerParams(
            dimension_semantics=("arbitrary",), vmem_limit_bytes=VMEM_LIMIT),
        name="inproj_dec" if rope else "inproj_ctx",
    )(*args)


def _mla_expand_kernel(ckv_ref, kpe_ref, wk_ref, wv_ref, km_ref, vmt_ref):
    ckvb = ckv_ref[...].astype(BF16)
    kn = _dot(ckvb, wk_ref[...])
    vm = _dot(ckvb, wv_ref[...])
    kpe = kpe_ref[...]
    for hd in range(MLA_HEADS):
        km_ref[hd] = (kn[:, hd * LANES:(hd + 1) * LANES] + kpe).astype(BF16)
    for s in range(N_SLABS):
        vmt_ref[s] = vm[:, s * LANES:(s + 1) * LANES].T.astype(BF16)


def _mla_expand_call(ckv, kpe128, wk, wv, layer):
    n_batch, _, rows, _ = ckv.shape
    t = n_batch * rows
    return pl.pallas_call(
        _mla_expand_kernel,
        grid=(n_batch,),
        in_specs=[
            pl.BlockSpec((None, None, rows, MLA_KV_RANK), lambda i: (i, layer, 0, 0)),
            pl.BlockSpec((None, None, rows, LANES), lambda i: (i, layer, 0, 0)),
            _resident_at((MLA_KV_RANK, MLA_HEADS * LANES), layer),
            _resident_at((MLA_KV_RANK, MLA_WIDTH), layer),
        ],
        out_specs=[
            pl.BlockSpec((MLA_HEADS, rows, LANES), lambda i: (0, i, 0)),
            pl.BlockSpec((N_SLABS, LANES, rows), lambda i: (0, 0, i)),
        ],
        out_shape=[
            jax.ShapeDtypeStruct((MLA_HEADS, t, LANES), BF16),
            jax.ShapeDtypeStruct((N_SLABS, LANES, t), BF16),
        ],
        compiler_params=pltpu.CompilerParams(dimension_semantics=("arbitrary",)),
        name="mla_expand",
    )(ckv, kpe128, wk, wv)


KEY_TILE = 256
N_SCORE_BUFS = 4


def _attn_kernel(n_parts, lam_init, *refs):
    q_ref, qm_ref, lamp_ref, subw_ref = refs[:4]
    parts = [refs[4 + 4 * i:8 + 4 * i] for i in range(n_parts)]
    oa_ref, ob_ref, sbuf, qt_scr = refs[4 + 4 * n_parts:]

    lp = lamp_ref[...]
    lam = (jnp.exp(jnp.sum(lp[0:1] * lp[1:2], keepdims=True))
           - jnp.exp(jnp.sum(lp[2:3] * lp[3:4], keepdims=True)) + lam_init)
    tq = q_ref.shape[1]
    low_half = lax.broadcasted_iota(jnp.int32, (tq, LANES), 1) < DA_V
    channel = lax.broadcasted_iota(jnp.int32, (LANES, tq), 0)
    subw = subw_ref[...] * (1.0 - lam_init)
    sublanes = 8

    tiles, row = [], 0
    for p, (k, _, _, _) in enumerate(parts):
        for k0 in range(0, k.shape[1], KEY_TILE):
            tiles.append((p, k0, row))
            row += KEY_TILE

    def row_fold(op, acc, x):
        for j in range(x.shape[0] // sublanes):
            piece = x[j * sublanes:(j + 1) * sublanes, :]
            acc = piece if acc is None else op(acc, piece)
        return acc

    for s in range(N_SLABS):
        qt_scr[s] = q_ref[s].astype(F32).T.astype(BF16)
    for hd in range(MLA_HEADS):
        qt_scr[N_SLABS + hd] = qm_ref[hd].astype(F32).T.astype(BF16)

    def scores(unit):
        buf = unit['buf']
        if unit['mla']:
            rhs, kref, kidx = qt_scr[N_SLABS + unit['head']], 2, unit['head']
        else:
            qt = qt_scr[unit['slab']]
            rhs, kref, kidx = jnp.where(channel // DA_QK == unit['sub'], qt, jnp.zeros_like(qt)), 0, unit['slab']
        mx = None
        for p, k0, r0 in tiles:
            s = _dot(parts[p][kref][kidx, k0:k0 + KEY_TILE, :], rhs)
            sbuf[buf, r0:r0 + KEY_TILE, :] = s
            mx = row_fold(jnp.maximum, mx, s)
            yield
        unit['max'] = jnp.max(mx, axis=0, keepdims=True)

    def expo(unit):
        buf, tot = unit['buf'], None
        for _, _, r0 in tiles:
            e = jnp.exp2(sbuf[buf, r0:r0 + KEY_TILE, :] - unit['max'])
            sbuf[buf, r0:r0 + KEY_TILE, :] = e
            tot = row_fold(jnp.add, tot, e)
            yield
        unit['sum'] = jnp.sum(tot, axis=0, keepdims=True)

    def pv(head):
        units, slab = head['units'], head['slab']
        vref = 3 if units[0]['mla'] else 1
        l0 = units[0]['sum']
        if len(units) == 2:
            r = lam * l0 / units[1]['sum']
        acc = None
        for p, k0, r0 in tiles:
            w = sbuf[units[0]['buf'], r0:r0 + KEY_TILE, :]
            if len(units) == 2:
                w = w - r * sbuf[units[1]['buf'], r0:r0 + KEY_TILE, :]
            o = _dot(parts[p][vref][slab, :, k0:k0 + KEY_TILE], w.astype(BF16))
            acc = o if acc is None else acc + o
            yield
        head['out'] = (acc * (1.0 / l0)).T

    def finish(slab, mla, lo, hi):
        o = jnp.where(low_half, lo['out'], hi['out'])
        if mla:
            ob_ref[slab] = o.astype(BF16)
            return
        o2 = o * o
        ss_lo = jnp.sum(jnp.where(low_half, o2, 0.0), axis=1, keepdims=True)
        ss_hi = jnp.sum(jnp.where(low_half, 0.0, o2), axis=1, keepdims=True)
        ms = jnp.where(low_half, ss_lo, ss_hi) * (1.0 / DA_V)
        oa_ref[slab] = (o * lax.rsqrt(ms + EPS) * subw).astype(BF16)

    units, heads = [], []
    for slab in range(N_SLABS):
        for mla in (False, True):
            for half in range(2):
                mine = [dict(mla=mla, slab=slab, head=2 * slab + half, sub=2 * half + c)
                        for c in range(1 if mla else 2)]
                heads.append(dict(units=mine, slab=slab, mla=mla, half=half, last=len(units) + len(mine) - 1))
                units += mine
    for k, unit in enumerate(units):
        unit['buf'] = k % N_SCORE_BUFS

    pending = None
    for k in range(len(units) + 2):
        stages = []
        if k < len(units):
            stages.append(scores(units[k]))
        if 1 <= k <= len(units):
            stages.append(expo(units[k - 1]))
        ready = [h for h in heads if h['last'] == k - 2]
        if ready:
            stages.append(pv(ready[0]))
        _interleave(stages)
        if ready and ready[0]['half'] == 1:
            finish(ready[0]['slab'], ready[0]['mla'], pending[(ready[0]['slab'], ready[0]['mla'])], ready[0])
        elif ready:
            pending = dict(pending or {})
            pending[(ready[0]['slab'], ready[0]['mla'])] = ready[0]


def _attn_call(q, qm, lamp, subw, parts, n_batch, layer, lam_init):
    t = q.shape[1]
    seq = t // n_batch
    tq = min(Q_TILE, seq)
    nq = seq // tq
    qmap = lambda b, i: (0, b * nq + i, 0)
    in_specs = [
        pl.BlockSpec((N_SLABS, tq, LANES), qmap),
        pl.BlockSpec((MLA_HEADS, tq, LANES), qmap),
        pl.BlockSpec((None, 4, DA_QK), lambda b, i: (layer, 0, 0)),
        pl.BlockSpec((None, 1, LANES), lambda b, i: (layer, 0, 0)),
    ]
    args = [q, qm, lamp, subw]

    def keys_spec(arr, n_keys, transposed):
        block = (arr.shape[-3], LANES, n_keys) if transposed else (arr.shape[-3], n_keys, LANES)
        at = (lambda b: (0, 0, b)) if transposed else (lambda b: (0, b, 0))
        if arr.ndim == 4:
            return pl.BlockSpec((None,) + block, lambda b, i: (layer,) + at(b))
        return pl.BlockSpec(block, lambda b, i: at(b))

    for k, vt, km, vmt, n_keys in parts:
        in_specs += [keys_spec(k, n_keys, False), keys_spec(vt, n_keys, True),
                     keys_spec(km, n_keys, False), keys_spec(vmt, n_keys, True)]
        args += [k, vt, km, vmt]
    return pl.pallas_call(
        functools.partial(_attn_kernel, len(parts), lam_init),
        grid=(n_batch, nq),
        in_specs=in_specs,
        out_specs=[pl.BlockSpec((N_SLABS, tq, LANES), qmap)] * 2,
        out_shape=[jax.ShapeDtypeStruct((N_SLABS, t, LANES), BF16)] * 2,
        scratch_shapes=[pltpu.VMEM((N_SCORE_BUFS, sum(p[4] for p in parts), tq), F32),
                        pltpu.VMEM((N_SLABS + MLA_HEADS, LANES, tq), BF16)],
        compiler_params=pltpu.CompilerParams(
            dimension_semantics=("arbitrary", "arbitrary"), vmem_limit_bytes=VMEM_LIMIT),
        name="attn_dec" if len(parts) > 1 else "attn_ctx",
    )(*args)


def _s5_kernel(n_batch, n_chunks, x_ref, kmat_ref, bst_ref, cst_ref, lamt_ref, h0_ref, y_ref, fin_ref,
               s_scr, hin_scr):
    x = x_ref[...]
    y = _dot(x, kmat_ref[...])
    for k in range(2):
        s_scr[k] = _dot(x, bst_ref[k])

    a_re, a_im = lamt_ref[0], lamt_ref[1]
    fwd_lanes = lax.broadcasted_iota(jnp.int32, (n_batch, LANES), 1) < S5_STATE

    def chunk_rows(j):
        return pl.ds(j, n_batch, stride=n_chunks)

    def step(j, carry):
        h_re, h_im = carry
        rf, rb = chunk_rows(j), chunk_rows(n_chunks - 1 - j)
        hin_scr[0, rf, :] = h_re
        hin_scr[1, rf, :] = h_im
        hin_scr[2, rb, :] = h_re
        hin_scr[3, rb, :] = h_im
        s_re = jnp.where(fwd_lanes, s_scr[0, rf, :], s_scr[0, rb, :])
        s_im = jnp.where(fwd_lanes, s_scr[1, rf, :], s_scr[1, rb, :])
        return a_re * h_re - a_im * h_im + s_re, a_re * h_im + a_im * h_re + s_im

    fin = lax.fori_loop(0, n_chunks, step, (h0_ref[0], h0_ref[1]))
    fwd_rows = lax.broadcasted_iota(jnp.int32, hin_scr.shape[1:], 1) < S5_STATE
    for k in range(2):
        fin_ref[k] = fin[k]
        hin = jnp.where(fwd_rows, hin_scr[k], hin_scr[2 + k])
        y += _dot(hin.astype(BF16), cst_ref[k])
    y_ref[...] = y


def _s5_call(xg, kmat, bst, cst, lamt, h0, n_batch, layer, h0_layer):
    rows = xg.shape[1]
    n_chunks = rows // n_batch
    wide = S5_CHUNK * S5_CH
    grp = lambda g: (g, 0, 0)
    grp4 = lambda g: (g, 0, 0, 0)
    return pl.pallas_call(
        functools.partial(_s5_kernel, n_batch, n_chunks),
        grid=(S5_GROUPS,),
        in_specs=[
            pl.BlockSpec((None, rows, wide), grp),
            pl.BlockSpec((None, None, wide, wide), lambda g: (layer, g, 0, 0)),
            pl.BlockSpec((None, None, 2, wide, LANES), lambda g: (layer, g, 0, 0, 0)),
            pl.BlockSpec((None, None, 2, LANES, wide), lambda g: (layer, g, 0, 0, 0)),
            pl.BlockSpec((None, None, 2, 1, LANES), lambda g: (layer, g, 0, 0, 0)),
            pl.BlockSpec((None, None, 2, n_batch, LANES), lambda g: (h0_layer, g, 0, 0, 0)),
        ],
        out_specs=[
            pl.BlockSpec((None, rows, wide), grp),
            pl.BlockSpec((None, 2, n_batch, LANES), grp4),
        ],
        out_shape=[
            jax.ShapeDtypeStruct((S5_GROUPS, rows, wide), F32),
            jax.ShapeDtypeStruct((S5_GROUPS, 2, n_batch, LANES), F32),
        ],
        scratch_shapes=[pltpu.VMEM((2, rows, LANES), F32), pltpu.VMEM((4, rows, LANES), F32)],
        compiler_params=pltpu.CompilerParams(dimension_semantics=("arbitrary",)),
        name="s5",
    )(xg, kmat, bst, cst, lamt, h0)


def _s5_operators(a_re, a_im, log_step, b_re, b_im, c_re, c_im, d):
    hi = lax.Precision.HIGHEST
    n = S5_CHUNK
    wide = n * S5_CH
    ar, ai = a_re.astype(F32), a_im.astype(F32)
    step = jnp.exp(log_step.astype(F32))[..., None]
    zr, zi = ar * step, ai * step
    k = jnp.arange(n + 1, dtype=F32)[:, None, None, None]
    mag = jnp.exp(k * zr[None])
    pr, pi = mag * jnp.cos(k * zi[None]), mag * jnp.sin(k * zi[None])

    xr, xi = pr[1] - 1.0, pi[1]
    den = ar * ar + ai * ai
    fr, fi = ((xr * ar + xi * ai) / den)[:, :, None, :], ((xi * ar - xr * ai) / den)[:, :, None, :]
    br, bi = (jnp.swapaxes(b.astype(F32), 2, 3) for b in (b_re, b_im))
    bbr, bbi = fr * br - fi * bi, fr * bi + fi * br
    cr, ci = c_re.astype(F32), c_im.astype(F32)

    tt = np.arange(n)

    def state_readout(powers, dirn):
        qr = pr[powers, dirn].transpose(1, 2, 0)[:, :, :, None]
        qi = pi[powers, dirn].transpose(1, 2, 0)[:, :, :, None]
        ctr = jnp.swapaxes(cr[dirn], 1, 2)[:, :, None, :]
        cti = jnp.swapaxes(ci[dirn], 1, 2)[:, :, None, :]
        rr = ctr * qr - cti * qi
        ri = -(ctr * qi + cti * qr)
        return rr.reshape(S5_GROUPS, S5_STATE, wide), ri.reshape(S5_GROUPS, S5_STATE, wide)

    def lag_response(powers, dirn):
        rr, ri = state_readout(powers, dirn)
        return (jnp.einsum('gkp,gpn->gkn', bbr[dirn], rr, precision=hi)
                + jnp.einsum('gkp,gpn->gkn', bbi[dirn], ri, precision=hi))

    fwd, bwd = lag_response(tt, 0), lag_response(n - 1 - tt, 1)
    skip = d.astype(F32)[:, None, :] * jnp.eye(S5_CH, dtype=F32)[None]
    strip = jnp.concatenate([bwd[:, :, :wide - S5_CH], bwd[:, :, wide - S5_CH:] + fwd[:, :, :S5_CH] + skip,
                             fwd[:, :, S5_CH:]], axis=-1)
    kmat = jnp.stack([strip[:, :, (n - 1 - s) * S5_CH:(n - 1 - s) * S5_CH + wide] for s in range(n)], axis=1)
    kmat = kmat.reshape(S5_GROUPS, wide, wide)

    def edge_states(powers, dirn):
        qr = jnp.swapaxes(pr[powers, dirn], 0, 1)[:, :, None, :]
        qi = jnp.swapaxes(pi[powers, dirn], 0, 1)[:, :, None, :]
        er = qr * bbr[dirn][:, None] - qi * bbi[dirn][:, None]
        ei = qr * bbi[dirn][:, None] + qi * bbr[dirn][:, None]
        return er.reshape(S5_GROUPS, wide, S5_STATE), ei.reshape(S5_GROUPS, wide, S5_STATE)

    (bf_r, bf_i), (bb_r, bb_i) = edge_states(n - 1 - tt, 0), edge_states(tt, 1)
    bst = jnp.stack([jnp.concatenate([bf_r, bb_r], axis=2), jnp.concatenate([bf_i, bb_i], axis=2)], axis=1)
    (cf_r, cf_i), (cb_r, cb_i) = state_readout(tt + 1, 0), state_readout(n - tt, 1)
    cst = jnp.stack([jnp.concatenate([cf_r, cb_r], axis=1), jnp.concatenate([cf_i, cb_i], axis=1)], axis=1)
    lamt = jnp.stack([jnp.concatenate([pr[n, 0], pr[n, 1]], axis=1),
                      jnp.concatenate([pi[n, 0], pi[n, 1]], axis=1)], axis=1)
    return kmat.astype(BF16), bst.astype(BF16), cst.astype(BF16), lamt[:, :, None, :]


def _rope_tables(n_pos):
    n = DA_QK // 4
    inv = ROPE_THETA ** (-jnp.arange(n, dtype=F32) / n)
    pos = jnp.arange(n_pos, dtype=jnp.int32)
    ang_r = (pos // GRID_W).astype(F32)[:, None] * inv[None, :]
    ang_c = (pos % GRID_W).astype(F32)[:, None] * inv[None, :]
    cos32 = jnp.concatenate([jnp.cos(ang_r)] * 2 + [jnp.cos(ang_c)] * 2, axis=1)
    sin32 = jnp.concatenate([-jnp.sin(ang_r), jnp.sin(ang_r), -jnp.sin(ang_c), jnp.sin(ang_c)], axis=1)
    pad = LANES - MLA_ROPE
    cosm = jnp.concatenate([cos32, jnp.ones((n_pos, pad), F32)], axis=1)
    sinm = jnp.concatenate([sin32, jnp.zeros((n_pos, pad), F32)], axis=1)
    return jnp.tile(cos32, (1, 4)), jnp.tile(sin32, (1, 4)), cosm, sinm


def _prep_inproj(w):
    o = np.cumsum([0, DA_WIDTH, DA_WIDTH, DA_WIDTH, MLA_Q_RANK, MLA_KV_RANK, MLA_ROPE, S5_WIDTH])
    wq, wk, wv, wcq, wckv, wkpe, wu = (w[..., o[i]:o[i + 1]] for i in range(7))
    wkpe = jnp.pad(wkpe, ((0, 0), (0, 0), (0, LANES - MLA_ROPE)))
    return jnp.concatenate([wq, wk, wv, wcq, wckv, wu, wkpe], axis=-1).astype(BF16)


def _prep_mla(q_norm_w, w_q_up, kv_norm_w, w_kv_up):
    pad = LANES - MLA_ROPE - MLA_NOPE
    wq = w_q_up.reshape(DEPTH, MLA_Q_RANK, MLA_HEADS, MLA_NOPE + MLA_ROPE)
    wqup = jnp.pad(jnp.concatenate([wq[..., MLA_NOPE:], wq[..., :MLA_NOPE]], axis=-1),
                   ((0, 0), (0, 0), (0, 0), (0, pad))).reshape(DEPTH, MLA_Q_RANK, MLA_HEADS * LANES)
    wkv = w_kv_up.reshape(DEPTH, MLA_KV_RANK, MLA_HEADS, MLA_NOPE + MLA_V)
    wk = jnp.pad(wkv[..., :MLA_NOPE], ((0, 0), (0, 0), (0, 0), (MLA_ROPE, pad)))
    wk = wk.reshape(DEPTH, MLA_KV_RANK, MLA_HEADS * LANES)
    wv = wkv[..., MLA_NOPE:].reshape(DEPTH, MLA_KV_RANK, MLA_WIDTH)
    return (q_norm_w.reshape(DEPTH, 1, MLA_Q_RANK), kv_norm_w.reshape(DEPTH, 1, MLA_KV_RANK),
            wqup.astype(BF16), wk.astype(BF16), wv.astype(BF16))


def _cached_slabs(cache, transposed):
    b, _, n = cache.shape[:3]
    a = cache.reshape(b, DEPTH, n, N_SLABS, LANES).astype(BF16)
    a = a.transpose(1, 3, 4, 0, 2) if transposed else a.transpose(1, 3, 0, 2, 4)
    return a.reshape(DEPTH, N_SLABS, LANES, b * n) if transposed else a.reshape(DEPTH, N_SLABS, b * n, LANES)


def kernel(x_prompt, x_sample, cache_diff_k, cache_diff_v, cache_mla_ckv, cache_mla_kpe, state_s5_re, state_s5_im, c, c_ctx, w_ada, b_ada, norm_w, ffn_w_in, ffn_w_out, w_in, diff_lambda, diff_subln_w, mla_q_norm_w, mla_w_q_up, mla_kv_norm_w, mla_w_kv_up, s5_a_re, s5_a_im, s5_log_step, s5_b_re, s5_b_im, s5_c_re, s5_c_im, s5_d, s5_w_glu, s5_b_glu, w_out, final_norm_w):
    n_ctx, seq, _ = x_prompt.shape
    n_dec, dec_seq, _ = x_sample.shape
    past = cache_diff_k.shape[2]

    cvec = jnp.concatenate(
        [c_ctx[None, :], c, jnp.zeros((MOD_ROWS - 1 - n_dec, D_MODEL), F32)], axis=0)
    mod = _adaln(cvec, w_ada, b_ada)
    rope_tabs = _rope_tables(dec_seq)

    groups = {
        'ctx': dict(x=x_prompt.reshape(n_ctx * seq, D_MODEL), seg=n_ctx * seq, row=0, nb=n_ctx),
        'dec': dict(x=x_sample.reshape(n_dec * dec_seq, D_MODEL), seg=dec_seq, row=1, nb=n_dec),
    }
    ffn_w = (ffn_w_in.astype(BF16), ffn_w_out.astype(BF16))
    norm_w4 = norm_w.reshape(DEPTH, 3, 1, D_MODEL)
    w_proj = _prep_inproj(w_in)
    mla_w = _prep_mla(mla_q_norm_w, mla_w_q_up, mla_kv_norm_w, mla_w_kv_up)
    mix_w = (s5_w_glu.astype(BF16), s5_b_glu.reshape(DEPTH, 1, S5_WIDTH), w_out.astype(BF16))
    s5_ops = jax.vmap(_s5_operators)(s5_a_re, s5_a_im, s5_log_step, s5_b_re, s5_b_im, s5_c_re, s5_c_im, s5_d)
    subw = jnp.tile(diff_subln_w, (1, 2)).reshape(DEPTH, 1, LANES)
    ctx_k, ctx_vt = _cached_slabs(cache_diff_k, False), _cached_slabs(cache_diff_v, True)
    ctx_kpe = jnp.pad(cache_mla_kpe, ((0, 0), (0, 0), (0, 0), (0, LANES - MLA_ROPE)))
    pack_dirs = lambda s: jnp.concatenate([s[:, :, 0], s[:, :, 1]], axis=-1)
    h0_dec = jnp.stack([pack_dirs(state_s5_re), pack_dirs(state_s5_im)], 0).transpose(2, 3, 0, 1, 4)
    h0_ctx = jnp.zeros((1, S5_GROUPS, 2, n_ctx, LANES), F32)
    caches = None
    s5_states = []
    for l in range(DEPTH):
        lam_init = 0.8 - 0.6 * math.exp(-0.3 * l)
        final_w = final_norm_w if l == DEPTH - 1 else None
        for name, g in groups.items():
            dec = name == 'dec'
            x = _ffn_call(g['x'], mod, g['seg'], g['row'], norm_w4, l, 0, ffn_w)
            proj = _inproj_call(x, mod, g['seg'], g['row'], norm_w4, l, w_proj, mla_w,
                                rope_tabs if dec else None, cache_out=None if dec else (seq, caches))
            q, k, vt, qm, km, vmt, xg = proj[:7]
            parts = [(k, vt, km, vmt, g['x'].shape[0] // g['nb'])]
            if dec:
                ckm, cvmt = _mla_expand_call(cache_mla_ckv, ctx_kpe, mla_w[3], mla_w[4], l)
                parts.append((ctx_k, ctx_vt, ckm, cvmt, past))
            o_a, o_b = _attn_call(q, qm, diff_lambda, subw, parts, g['nb'], l, lam_init)
            y_s5, fin = _s5_call(xg, *s5_ops, h0_dec if dec else h0_ctx, g['nb'], l, l if dec else 0)
            g['x'] = _ffn_call(x, mod, g['seg'], g['row'], norm_w4, l, 1, ffn_w,
                               mix_args=(o_a, o_b, y_s5), mix_w=mix_w, final_w=final_w)
            if not dec:
                caches = proj[7:]
                fin = fin.reshape(S5_GROUPS, 2, n_ctx, 2, S5_STATE).transpose(1, 2, 3, 0, 4)
                s5_states.append((fin[0], fin[1]))

    y_prompt = groups['ctx']['x'].reshape(n_ctx, seq, D_MODEL)
    y_sample = groups['dec']['x'].reshape(n_dec, dec_seq, D_MODEL)
    new_k, new_v, new_ckv, new_kpe = caches
    new_s5_re = jnp.stack([t[0] for t in s5_states], axis=1)
    new_s5_im = jnp.stack([t[1] for t in s5_states], axis=1)
    return (y_prompt, y_sample,
            new_k.reshape(n_ctx, DEPTH, seq, DA_HEADS, 2 * DA_QK),
            new_v.reshape(n_ctx, DEPTH, seq, DA_HEADS, DA_V),
            new_ckv, new_kpe, new_s5_re, new_s5_im)
```

```python
import functools
import math

import jax
import jax.numpy as jnp
import numpy as np
from jax import lax
from jax.experimental import pallas as pl
from jax.experimental.pallas import tpu as pltpu

F32 = jnp.float32
BF16 = jnp.bfloat16

D_MODEL = 1024
DEPTH = 2
GRID_W = 64
ROPE_THETA = 10000.0
EPS = 1e-6
N_MOD = 9
D_FF = 2816
DA_HEADS = 6
DA_QK = 32
DA_V = 64
DA_WIDTH = DA_HEADS * DA_V
MLA_HEADS = 6
MLA_Q_RANK = 256
MLA_KV_RANK = 128
MLA_NOPE = 64
MLA_ROPE = 32
MLA_V = 64
MLA_WIDTH = MLA_HEADS * MLA_V
S5_WIDTH = D_MODEL - DA_WIDTH - MLA_WIDTH
S5_CH = 16
S5_GROUPS = S5_WIDTH // S5_CH
S5_STATE = 64

LANES = 128
FF_CHUNK = 256
N_FF_CHUNKS = D_FF // FF_CHUNK
TOKEN_TILE = 512
Q_TILE = 256
S5_CHUNK = 16
N_SLABS = DA_WIDTH // LANES
MOD_ROWS = 8
VMEM_LIMIT = 56 * 1024 * 1024


def _rms(x, w):
    return x * lax.rsqrt(jnp.mean(x * x, axis=-1, keepdims=True) + EPS) * w


def _silu(x):
    return x * jax.nn.sigmoid(x)


def _gelu_tanh(x):
    c = math.sqrt(2.0 / math.pi)
    return x * (0.5 * (1.0 + jnp.tanh(c * (x + 0.044715 * (x * x * x)))))


def _dot(a, b):
    return jnp.dot(a, b, preferred_element_type=F32)


def _resident(shape):
    nd = len(shape)
    return pl.BlockSpec(shape, lambda *_: (0,) * nd, pipeline_mode=pl.Buffered(1))


def _resident_at(shape, *lead):
    nd = len(shape)
    return pl.BlockSpec((None,) * len(lead) + tuple(shape), lambda *_: tuple(lead) + (0,) * nd,
                        pipeline_mode=pl.Buffered(1))


def _adaln_kernel(c_ref, w_ref, b_ref, o_ref):
    s = _silu(c_ref[...])
    o_ref[...] = jnp.dot(s, w_ref[...], preferred_element_type=F32,
                         precision=lax.Precision.HIGHEST) + b_ref[...]


def _adaln(cvec, w_ada, b_ada):
    n = N_MOD * D_MODEL
    tn = 1152
    out = pl.pallas_call(
        _adaln_kernel,
        grid=(DEPTH, n // tn),
        in_specs=[
            pl.BlockSpec((MOD_ROWS, D_MODEL), lambda l, j: (0, 0)),
            pl.BlockSpec((None, D_MODEL, tn), lambda l, j: (l, 0, j)),
            pl.BlockSpec((None, 1, tn), lambda l, j: (l, 0, j)),
        ],
        out_specs=pl.BlockSpec((None, MOD_ROWS, tn), lambda l, j: (l, 0, j)),
        out_shape=jax.ShapeDtypeStruct((DEPTH, MOD_ROWS, n), F32),
        compiler_params=pltpu.CompilerParams(
            dimension_semantics=("arbitrary", "arbitrary"), vmem_limit_bytes=VMEM_LIMIT),
        name="adaln",
    )(cvec, w_ada, b_ada.reshape(DEPTH, 1, n))
    return out.reshape(DEPTH, MOD_ROWS, N_MOD, D_MODEL)


_ROWS_PER_TILE = TOKEN_TILE // S5_CHUNK
_SLOTS = LANES // S5_CH


def _to_group_major(u_scr, xg_ref, row0, n_rows):
    c0, n_chunks = row0 // S5_CHUNK, n_rows // S5_CHUNK
    slot = lax.broadcasted_iota(jnp.int32, (n_chunks, LANES), 1) // S5_CH
    for g in range(S5_GROUPS):
        for tt in range(S5_CHUNK // _SLOTS):
            acc = None
            for j in range(_SLOTS):
                t = tt * _SLOTS + j
                src = u_scr[g // _SLOTS, pl.ds(row0 + t, n_chunks, stride=S5_CHUNK), :]
                shift = ((j - g % _SLOTS) % _SLOTS) * S5_CH
                piece = pltpu.roll(src, shift, 1) if shift else src
                acc = piece if acc is None else jnp.where(slot == j, piece, acc)
            xg_ref[g, c0:c0 + n_chunks, tt * LANES:(tt + 1) * LANES] = acc.astype(BF16)
        yield


def _from_group_major(yg_ref, y_scr, row0, n_rows):
    c0, n_chunks = row0 // S5_CHUNK, n_rows // S5_CHUNK
    slot = lax.broadcasted_iota(jnp.int32, (n_chunks, LANES), 1) // S5_CH
    for t in range(S5_CHUNK):
        for gc in range(S5_GROUPS // _SLOTS):
            acc = None
            for m in range(_SLOTS):
                src = yg_ref[gc * _SLOTS + m, c0:c0 + n_chunks, (t // _SLOTS) * LANES:(t // _SLOTS + 1) * LANES]
                shift = ((m - t % _SLOTS) % _SLOTS) * S5_CH
                piece = pltpu.roll(src, shift, 1) if shift else src
                acc = piece if acc is None else jnp.where(slot == m, piece, acc)
            y_scr[gc, pl.ds(row0 + t, n_chunks, stride=S5_CHUNK), :] = acc
        yield


def _interleave(stages):
    stages = list(stages)
    while stages:
        for g in list(stages):
            try:
                next(g)
            except StopIteration:
                stages.remove(g)


def _delayed(gen, steps):
    for _ in range(steps):
        yield
    yield from gen


_MIX_PARTS = 2
_MIX_STAGGER = 8


def _ffn_kernel(mix, final, mod_base, *refs):
    it = iter(refs)
    x_ref, mod_ref, nw_ref, wi_ref, wo_ref = (next(it) for _ in range(5))
    if mix:
        oa_ref, ob_ref, ys_ref, wglu_ref, bglu_ref, wout_ref = (next(it) for _ in range(6))
    if final:
        fnw_ref = next(it)
    out_ref, h_scr, acc_scr = next(it), next(it), next(it)

    if mix:
        y_scr = next(it)
    shift = mod_ref[0, mod_base:mod_base + 1, :]
    scale = mod_ref[0, mod_base + 1:mod_base + 2, :]
    gate = mod_ref[0, mod_base + 2:mod_base + 3, :]
    tm = x_ref.shape[0]
    n_part = _MIX_PARTS if mix else 1
    rows_part = tm // n_part

    def chunk_cols(j):
        col = j * FF_CHUNK
        return slice(col, col + FF_CHUNK), slice(D_FF + col, D_FF + col + FF_CHUNK)

    def prologue(p):
        r0 = p * rows_part
        rows = slice(r0, r0 + rows_part)
        x = x_ref[rows, :]
        if mix:
            yield from _from_group_major(ys_ref, y_scr, r0, rows_part)
            yc = _gelu_tanh(jnp.concatenate([y_scr[gc, rows, :] for gc in range(S5_WIDTH // LANES)], axis=1))
            oc = yc * jax.nn.sigmoid(_dot(yc.astype(BF16), wglu_ref[...]) + bglu_ref[...])
            yield
            mixed = jnp.concatenate([oa_ref[s, rows, :] for s in range(N_SLABS)]
                                    + [ob_ref[s, rows, :] for s in range(N_SLABS)]
                                    + [oc.astype(BF16)], axis=1)
            x = x + mod_ref[0, mod_base - 1:mod_base, :] * _dot(mixed, wout_ref[...])
            out_ref[rows, :] = x
            yield
        h = _rms(x, nw_ref[...]) * (1.0 + scale) + shift
        h_scr[rows, :] = h.astype(BF16)
        yield

    _interleave(_delayed(prologue(p), p * _MIX_STAGGER) for p in range(n_part))
    acc_scr[...] = jnp.zeros_like(acc_scr)

    for j in range(N_FF_CHUNKS):
        hb = h_scr[...]
        ca, cg = chunk_cols(j)
        a = _dot(hb, wi_ref[:, ca])
        g = _dot(hb, wi_ref[:, cg])
        acc_scr[...] += _dot((_silu(g) * a).astype(BF16), wo_ref[ca, :])
    y = (out_ref[...] if mix else x_ref[...]) + 0.5 * gate * acc_scr[...]
    if final:
        y = _rms(y, fnw_ref[...])
    out_ref[...] = y


def _mod_spec(layer, seg_len, row_base):
    return pl.BlockSpec((None, 1, N_MOD, D_MODEL),
                        lambda i: (layer, row_base + (i * TOKEN_TILE) // seg_len, 0, 0))


def _ffn_call(x, mod, seg_len, row_base, norm_w4, layer, which, ffn_w, mix_args=None, mix_w=None, final_w=None):
    t = x.shape[0]
    tm = TOKEN_TILE
    mix = mix_args is not None
    final = final_w is not None
    w_in_all, w_out_all = ffn_w
    tok = lambda i: (i, 0)
    in_specs = [
        pl.BlockSpec((tm, D_MODEL), tok),
        _mod_spec(layer, seg_len, row_base),
        _resident_at((1, D_MODEL), layer, 2 * which),
        _resident_at((D_MODEL, 2 * D_FF), layer, which),
        _resident_at((D_FF, D_MODEL), layer, which),
    ]
    args = [x, mod, norm_w4, w_in_all, w_out_all]
    if mix:
        in_specs += [
            pl.BlockSpec((N_SLABS, tm, LANES), lambda i: (0, i, 0)),
            pl.BlockSpec((N_SLABS, tm, LANES), lambda i: (0, i, 0)),
            pl.BlockSpec((S5_GROUPS, _ROWS_PER_TILE, S5_CHUNK * S5_CH), lambda i: (0, i, 0)),
            _resident_at((S5_WIDTH, S5_WIDTH), layer),
            _resident_at((1, S5_WIDTH), layer),
            _resident_at((D_MODEL, D_MODEL), layer),
        ]
        args += [*mix_args, *mix_w]
    if final:
        in_specs.append(_resident((1, D_MODEL)))
        args.append(final_w.reshape(1, D_MODEL))
    return pl.pallas_call(
        functools.partial(_ffn_kernel, mix, final, 6 * which),
        grid=(t // tm,),
        in_specs=in_specs,
        out_specs=pl.BlockSpec((tm, D_MODEL), tok),
        out_shape=jax.ShapeDtypeStruct((t, D_MODEL), F32),
        scratch_shapes=[pltpu.VMEM((tm, D_MODEL), BF16), pltpu.VMEM((tm, D_MODEL), F32)]
        + ([pltpu.VMEM((S5_WIDTH // LANES, tm, LANES), F32)] if mix else []),
        compiler_params=pltpu.CompilerParams(
            dimension_semantics=("arbitrary",), vmem_limit_bytes=VMEM_LIMIT),
        name="mix_ffn" if mix else "ffn",
    )(*args)


_C_Q, _C_K, _C_V = 0, DA_WIDTH, 2 * DA_WIDTH
_C_CQ = 3 * DA_WIDTH
_C_CKV = _C_CQ + MLA_Q_RANK
_C_U = _C_CKV + MLA_KV_RANK
_C_KPE = _C_U + S5_WIDTH
_NCOL = _C_KPE + LANES

_QSCALE_DIFF = DA_QK ** -0.5 * math.log2(math.e)
_QSCALE_MLA = (MLA_NOPE + MLA_ROPE) ** -0.5 * math.log2(math.e)


_INPROJ_PARTS = 2
_INPROJ_STAGGER = 6


def _rotate_half(x):
    half = DA_QK // 4
    lane = lax.broadcasted_iota(jnp.int32, x.shape, 1)
    return jnp.where(lane % (2 * half) < half, pltpu.roll(x, LANES - half, 1), pltpu.roll(x, half, 1))


def _inproj_kernel(rope, n_aliased, *refs):
    it = iter(refs)
    x_ref, mod_ref, nw_ref, w_ref, qnw_ref, kvnw_ref, wqup_ref, wk_ref, wv_ref = (next(it) for _ in range(9))
    if rope:
        cos_ref, sin_ref, cosm_ref, sinm_ref = (next(it) for _ in range(4))
    for _ in range(n_aliased):
        next(it)
    q_ref, k_ref, vt_ref, qm_ref, km_ref, vmt_ref, xg_ref = (next(it) for _ in range(7))
    if not rope:
        ka_ref, va_ref, ckv_ref, kpe_ref = (next(it) for _ in range(4))
    u_scr = next(it)

    tm = x_ref.shape[0]
    n_part = _INPROJ_PARTS
    rows_part = tm // n_part

    def part(p):
        r0 = p * rows_part
        rows = slice(r0, r0 + rows_part)
        h = _rms(x_ref[rows, :], nw_ref[...]) * (1.0 + mod_ref[0, 4:5, :]) + mod_ref[0, 3:4, :]
        h = h.astype(BF16)
        yield
        qkv = _dot(h, w_ref[:, :_C_CQ])
        yield
        rest = _dot(h, w_ref[:, _C_CQ:])
        yield

        def cols(start, width):
            src, off = (qkv, 0) if start < _C_CQ else (rest, _C_CQ)
            return src[:, start - off:start - off + width]

        if rope:
            cos, sin, cosm, sinm = (r[rows, :] for r in (cos_ref, sin_ref, cosm_ref, sinm_ref))
        for s in range(N_SLABS):
            qs = cols(_C_Q + s * LANES, LANES)
            ks = cols(_C_K + s * LANES, LANES)
            if rope:
                qs = qs * cos + _rotate_half(qs) * sin
                ks = ks * cos + _rotate_half(ks) * sin
            q_ref[s, rows, :] = (qs * _QSCALE_DIFF).astype(BF16)
            k_ref[s, rows, :] = ks.astype(BF16)
            vt_ref[s, :, rows] = cols(_C_V + s * LANES, LANES).T.astype(BF16)
            yield
        for gc in range(S5_WIDTH // LANES):
            u_scr[gc, rows, :] = cols(_C_U + gc * LANES, LANES)
        yield from _to_group_major(u_scr, xg_ref, r0, rows_part)

        cqn = _rms(cols(_C_CQ, MLA_Q_RANK), qnw_ref[...]).astype(BF16)
        qm = _dot(cqn, wqup_ref[...])
        ckvn = _rms(cols(_C_CKV, MLA_KV_RANK), kvnw_ref[...])
        ckvb = ckvn.astype(BF16)
        kn = _dot(ckvb, wk_ref[...])
        vm = _dot(ckvb, wv_ref[...])
        yield
        kpe = cols(_C_KPE, LANES)
        if rope:
            kpe = kpe * cosm + _rotate_half(kpe) * sinm
        for hd in range(MLA_HEADS):
            qh = qm[:, hd * LANES:(hd + 1) * LANES]
            if rope:
                qh = qh * cosm + _rotate_half(qh) * sinm
            qm_ref[hd, rows, :] = (qh * _QSCALE_MLA).astype(BF16)
            km_ref[hd, rows, :] = (kn[:, hd * LANES:(hd + 1) * LANES] + kpe).astype(BF16)
            yield
        for s in range(N_SLABS):
            vmt_ref[s, :, rows] = vm[:, s * LANES:(s + 1) * LANES].T.astype(BF16)

        if not rope:
            seq = ka_ref.shape[-2]
            new = (cols(_C_K, DA_WIDTH), cols(_C_V, DA_WIDTH), ckvn, cols(_C_KPE, MLA_ROPE))
            for b in range(r0 // seq, (r0 + rows_part) // seq):
                mine = slice(b * seq - r0, (b + 1) * seq - r0)
                for ref, val in zip((ka_ref, va_ref, ckv_ref, kpe_ref), new):
                    if len(ref.shape) == 3:
                        ref[b] = val[mine]
                    else:
                        ref[b, 0] = val[mine]
                        for later in range(1, ref.shape[1]):
                            ref[b, later] = jnp.zeros_like(val[mine])

    _interleave(_delayed(part(p), p * _INPROJ_STAGGER) for p in range(n_part))


def _inproj_call(x, mod, seg_len, row_base, norm_w4, layer, w, mla_w, rope_tabs, cache_out=None):
    t = x.shape[0]
    tm = TOKEN_TILE
    rope = rope_tabs is not None
    qnw, kvnw, wqup, wk, wv = mla_w
    tok = lambda i: (i, 0)
    slab = lambda i: (0, i, 0)
    slab_t = lambda i: (0, 0, i)
    in_specs = [
        pl.BlockSpec((tm, D_MODEL), tok),
        _mod_spec(layer, seg_len, row_base),
        _resident_at((1, D_MODEL), layer, 1),
        _resident_at((D_MODEL, _NCOL), layer),
        _resident_at((1, MLA_Q_RANK), layer),
        _resident_at((1, MLA_KV_RANK), layer),
        _resident_at((MLA_Q_RANK, MLA_HEADS * LANES), layer),
        _resident_at((MLA_KV_RANK, MLA_HEADS * LANES), layer),
        _resident_at((MLA_KV_RANK, MLA_WIDTH), layer),
    ]
    args = [x, mod, norm_w4, w, qnw, kvnw, wqup, wk, wv]
    if rope:
        n_pos = rope_tabs[0].shape[0]
        pos = lambda i: (i % (n_pos // tm), 0)
        in_specs += [pl.BlockSpec((tm, LANES), pos)] * 4
        args += list(rope_tabs)
    out_shape = [
        jax.ShapeDtypeStruct((N_SLABS, t, LANES), BF16),
        jax.ShapeDtypeStruct((N_SLABS, t, LANES), BF16),
        jax.ShapeDtypeStruct((N_SLABS, LANES, t), BF16),
        jax.ShapeDtypeStruct((MLA_HEADS, t, LANES), BF16),
        jax.ShapeDtypeStruct((MLA_HEADS, t, LANES), BF16),
        jax.ShapeDtypeStruct((N_SLABS, LANES, t), BF16),
        jax.ShapeDtypeStruct((S5_GROUPS, t // S5_CHUNK, S5_CHUNK * S5_CH), BF16),
    ]
    out_specs = [
        pl.BlockSpec((N_SLABS, tm, LANES), slab),
        pl.BlockSpec((N_SLABS, tm, LANES), slab),
        pl.BlockSpec((N_SLABS, LANES, tm), slab_t),
        pl.BlockSpec((MLA_HEADS, tm, LANES), slab),
        pl.BlockSpec((MLA_HEADS, tm, LANES), slab),
        pl.BlockSpec((N_SLABS, LANES, tm), slab_t),
        pl.BlockSpec((S5_GROUPS, _ROWS_PER_TILE, S5_CHUNK * S5_CH), slab),
    ]
    aliases = {}
    if not rope:
        seq, stacked = cache_out
        n_first = len(out_shape)
        for k, width in enumerate((DA_WIDTH, DA_WIDTH, MLA_KV_RANK, MLA_ROPE)):
            out_shape.append(jax.ShapeDtypeStruct((t // seq, DEPTH, seq, width), F32))
            if stacked is None:
                out_specs.append(pl.BlockSpec((tm // seq, DEPTH, seq, width), lambda i: (i, 0, 0, 0)))
            else:
                out_specs.append(pl.BlockSpec((tm // seq, None, seq, width), lambda i: (i, layer, 0, 0)))
                aliases[len(args)] = n_first + k
                in_specs.append(pl.BlockSpec(memory_space=pl.ANY))
                args.append(stacked[k])
    return pl.pallas_call(
        functools.partial(_inproj_kernel, rope, len(aliases)),
        grid=(t // tm,),
        in_specs=in_specs,
        out_specs=out_specs,
        out_shape=out_shape,
        input_output_aliases=aliases,
        scratch_shapes=[pltpu.VMEM((S5_WIDTH // LANES, tm, LANES), F32)],
        compiler_params=pltpu.CompilerParams(
            dimension_semantics=("arbitrary",), vmem_limit_bytes=VMEM_LIMIT),
        name="inproj_dec" if rope else "inproj_ctx",
    )(*args)


def _mla_expand_kernel(ckv_ref, kpe_ref, wk_ref, wv_ref, km_ref, vmt_ref):
    ckvb = ckv_ref[...].astype(BF16)
    kn = _dot(ckvb, wk_ref[...])
    vm = _dot(ckvb, wv_ref[...])
    kpe = kpe_ref[...]
    for hd in range(MLA_HEADS):
        km_ref[hd] = (kn[:, hd * LANES:(hd + 1) * LANES] + kpe).astype(BF16)
    for s in range(N_SLABS):
        vmt_ref[s] = vm[:, s * LANES:(s + 1) * LANES].T.astype(BF16)


def _mla_expand_call(ckv, kpe128, wk, wv, layer):
    n_batch, _, rows, _ = ckv.shape
    t = n_batch * rows
    return pl.pallas_call(
        _mla_expand_kernel,
        grid=(n_batch,),
        in_specs=[
            pl.BlockSpec((None, None, rows, MLA_KV_RANK), lambda i: (i, layer, 0, 0)),
            pl.BlockSpec((None, None, rows, LANES), lambda i: (i, layer, 0, 0)),
            _resident_at((MLA_KV_RANK, MLA_HEADS * LANES), layer),
            _resident_at((MLA_KV_RANK, MLA_WIDTH), layer),
        ],
        out_specs=[
            pl.BlockSpec((MLA_HEADS, rows, LANES), lambda i: (0, i, 0)),
            pl.BlockSpec((N_SLABS, LANES, rows), lambda i: (0, 0, i)),
        ],
        out_shape=[
            jax.ShapeDtypeStruct((MLA_HEADS, t, LANES), BF16),
            jax.ShapeDtypeStruct((N_SLABS, LANES, t), BF16),
        ],
        compiler_params=pltpu.CompilerParams(dimension_semantics=("arbitrary",)),
        name="mla_expand",
    )(ckv, kpe128, wk, wv)


KEY_TILE = 256
SHORT_SEQS_PER_STEP = 2
N_SCORE_BUFS = 4


def _attn_kernel(n_parts, lam_init, n_seq, *refs):
    q_ref, qm_ref, lamp_ref, subw_ref = refs[:4]
    parts = [refs[4 + 4 * i:8 + 4 * i] for i in range(n_parts)]
    oa_ref, ob_ref, sbuf, qt_scr = refs[4 + 4 * n_parts:]

    lp = lamp_ref[...]
    lam = (jnp.exp(jnp.sum(lp[0:1] * lp[1:2], keepdims=True))
           - jnp.exp(jnp.sum(lp[2:3] * lp[3:4], keepdims=True)) + lam_init)
    tq = q_ref.shape[1] // n_seq
    low_half = lax.broadcasted_iota(jnp.int32, (tq, LANES), 1) < DA_V
    channel = lax.broadcasted_iota(jnp.int32, (LANES, tq), 0)
    subw = subw_ref[...] * (1.0 - lam_init)
    sublanes = 8
    n_qt = N_SLABS + MLA_HEADS

    tiles, row = [], 0
    for p, (k, _, _, _) in enumerate(parts):
        per_seq = k.shape[1] // n_seq
        for k0 in range(0, per_seq, KEY_TILE):
            n = min(KEY_TILE, per_seq - k0)
            tiles.append((p, k0, n, slice(row, row + n)))
            row += n

    def key_slice(unit, p, k0, n):
        first = unit['seq'] * (parts[p][0].shape[1] // n_seq) + k0
        return slice(first, first + n)

    def row_fold(op, acc, x):
        for j in range(x.shape[0] // sublanes):
            piece = x[j * sublanes:(j + 1) * sublanes, :]
            acc = piece if acc is None else op(acc, piece)
        return acc

    for j in range(n_seq):
        mine = slice(j * tq, (j + 1) * tq)
        for s in range(N_SLABS):
            qt_scr[j * n_qt + s] = q_ref[s, mine, :].astype(F32).T.astype(BF16)
        for hd in range(MLA_HEADS):
            qt_scr[j * n_qt + N_SLABS + hd] = qm_ref[hd, mine, :].astype(F32).T.astype(BF16)

    def scores(unit):
        buf = unit['buf']
        if unit['mla']:
            rhs, kref, kidx = qt_scr[unit['seq'] * n_qt + N_SLABS + unit['head']], 2, unit['head']
        else:
            qt = qt_scr[unit['seq'] * n_qt + unit['slab']]
            rhs, kref, kidx = jnp.where(channel // DA_QK == unit['sub'], qt, jnp.zeros_like(qt)), 0, unit['slab']
        mx = None
        for p, k0, n, rows in tiles:
            s = _dot(parts[p][kref][kidx, key_slice(unit, p, k0, n), :], rhs)
            sbuf[buf, rows, :] = s
            mx = row_fold(jnp.maximum, mx, s)
            yield
        unit['max'] = jnp.max(mx, axis=0, keepdims=True)

    def expo(unit):
        buf, tot = unit['buf'], None
        for _, _, _, rows in tiles:
            e = jnp.exp2(sbuf[buf, rows, :] - unit['max'])
            sbuf[buf, rows, :] = e
            tot = row_fold(jnp.add, tot, e)
            yield
        unit['sum'] = jnp.sum(tot, axis=0, keepdims=True)

    def pv(head):
        units, slab = head['units'], head['slab']
        vref = 3 if units[0]['mla'] else 1
        l0 = units[0]['sum']
        if len(units) == 2:
            r = lam * l0 / units[1]['sum']
        acc = None
        for p, k0, n, rows in tiles:
            w = sbuf[units[0]['buf'], rows, :]
            if len(units) == 2:
                w = w - r * sbuf[units[1]['buf'], rows, :]
            o = _dot(parts[p][vref][slab, :, key_slice(units[0], p, k0, n)], w.astype(BF16))
            acc = o if acc is None else acc + o
            yield
        head['out'] = (acc * (1.0 / l0)).T

    def finish(lo, hi):
        slab, mine = hi['slab'], slice(hi['seq'] * tq, (hi['seq'] + 1) * tq)
        o = jnp.where(low_half, lo['out'], hi['out'])
        if hi['mla']:
            ob_ref[slab, mine, :] = o.astype(BF16)
            return
        o2 = o * o
        ss_lo = jnp.sum(jnp.where(low_half, o2, 0.0), axis=1, keepdims=True)
        ss_hi = jnp.sum(jnp.where(low_half, 0.0, o2), axis=1, keepdims=True)
        ms = jnp.where(low_half, ss_lo, ss_hi) * (1.0 / DA_V)
        oa_ref[slab, mine, :] = (o * lax.rsqrt(ms + EPS) * subw).astype(BF16)

    def stream(seq):
        units, heads = [], []
        for slab in range(N_SLABS):
            for mla in (False, True):
                for half in range(2):
                    mine = [dict(seq=seq, mla=mla, slab=slab, head=2 * slab + half, sub=2 * half + c)
                            for c in range(1 if mla else 2)]
                    heads.append(dict(seq=seq, units=mine, slab=slab, mla=mla, half=half,
                                      last=len(units) + len(mine) - 1))
                    units += mine
        for k, unit in enumerate(units):
            unit['buf'] = seq * N_SCORE_BUFS + k % N_SCORE_BUFS
        return units, heads

    streams = [stream(j) for j in range(n_seq)]
    n_units = len(streams[0][0])
    pending = {}
    for k in range(n_units + 2):
        stages, ready = [], []
        for units, heads in streams:
            if k < n_units:
                stages.append(scores(units[k]))
            if 1 <= k <= n_units:
                stages.append(expo(units[k - 1]))
            for h in heads:
                if h['last'] == k - 2:
                    stages.append(pv(h))
                    ready.append(h)
        _interleave(stages)
        for h in ready:
            key = (h['seq'], h['slab'], h['mla'])
            if h['half'] == 1:
                finish(pending.pop(key), h)
            else:
                pending[key] = h


def _attn_call(q, qm, lamp, subw, parts, n_batch, layer, lam_init):
    t = q.shape[1]
    seq = t // n_batch
    tq = min(Q_TILE, seq)
    nq = seq // tq
    n_seq = SHORT_SEQS_PER_STEP if nq == 1 and n_batch % SHORT_SEQS_PER_STEP == 0 else 1
    qmap = lambda b, i: (0, b * nq + i, 0)
    in_specs = [
        pl.BlockSpec((N_SLABS, n_seq * tq, LANES), qmap),
        pl.BlockSpec((MLA_HEADS, n_seq * tq, LANES), qmap),
        pl.BlockSpec((None, 4, DA_QK), lambda b, i: (layer, 0, 0)),
        pl.BlockSpec((None, 1, LANES), lambda b, i: (layer, 0, 0)),
    ]
    args = [q, qm, lamp, subw]

    def keys_spec(arr, n_keys, transposed):
        n_keys = n_seq * n_keys
        block = (arr.shape[-3], LANES, n_keys) if transposed else (arr.shape[-3], n_keys, LANES)
        at = (lambda b: (0, 0, b)) if transposed else (lambda b: (0, b, 0))
        if arr.ndim == 4:
            return pl.BlockSpec((None,) + block, lambda b, i: (layer,) + at(b))
        return pl.BlockSpec(block, lambda b, i: at(b))

    for k, vt, km, vmt, n_keys in parts:
        in_specs += [keys_spec(k, n_keys, False), keys_spec(vt, n_keys, True),
                     keys_spec(km, n_keys, False), keys_spec(vmt, n_keys, True)]
        args += [k, vt, km, vmt]
    return pl.pallas_call(
        functools.partial(_attn_kernel, len(parts), lam_init, n_seq),
        grid=(n_batch // n_seq, nq),
        in_specs=in_specs,
        out_specs=[pl.BlockSpec((N_SLABS, n_seq * tq, LANES), qmap)] * 2,
        out_shape=[jax.ShapeDtypeStruct((N_SLABS, t, LANES), BF16)] * 2,
        scratch_shapes=[pltpu.VMEM((n_seq * N_SCORE_BUFS, sum(p[4] for p in parts), tq), F32),
                        pltpu.VMEM((n_seq * (N_SLABS + MLA_HEADS), LANES, tq), BF16)],
        compiler_params=pltpu.CompilerParams(
            dimension_semantics=("arbitrary", "arbitrary"), vmem_limit_bytes=VMEM_LIMIT),
        name="attn_dec" if len(parts) > 1 else "attn_ctx",
    )(*args)


def _s5_kernel(n_batch, n_chunks, x_ref, kmat_ref, bst_ref, cst_ref, lamt_ref, h0_ref, y_ref, fin_ref,
               s_scr, hin_scr):
    x = x_ref[...]
    y = _dot(x, kmat_ref[...])
    for k in range(2):
        s_scr[k] = _dot(x, bst_ref[k])

    a_re, a_im = lamt_ref[0], lamt_ref[1]
    fwd_lanes = lax.broadcasted_iota(jnp.int32, (n_batch, LANES), 1) < S5_STATE

    def chunk_rows(j):
        return pl.ds(j, n_batch, stride=n_chunks)

    def step(j, carry):
        h_re, h_im = carry
        rf, rb = chunk_rows(j), chunk_rows(n_chunks - 1 - j)
        hin_scr[0, rf, :] = h_re
        hin_scr[1, rf, :] = h_im
        hin_scr[2, rb, :] = h_re
        hin_scr[3, rb, :] = h_im
        s_re = jnp.where(fwd_lanes, s_scr[0, rf, :], s_scr[0, rb, :])
        s_im = jnp.where(fwd_lanes, s_scr[1, rf, :], s_scr[1, rb, :])
        return a_re * h_re - a_im * h_im + s_re, a_re * h_im + a_im * h_re + s_im

    fin = lax.fori_loop(0, n_chunks, step, (h0_ref[0], h0_ref[1]))
    fwd_rows = lax.broadcasted_iota(jnp.int32, hin_scr.shape[1:], 1) < S5_STATE
    for k in range(2):
        fin_ref[k] = fin[k]
        hin = jnp.where(fwd_rows, hin_scr[k], hin_scr[2 + k])
        y += _dot(hin.astype(BF16), cst_ref[k])
    y_ref[...] = y


def _s5_call(xg, kmat, bst, cst, lamt, h0, n_batch, layer, h0_layer):
    rows = xg.shape[1]
    n_chunks = rows // n_batch
    wide = S5_CHUNK * S5_CH
    grp = lambda g: (g, 0, 0)
    grp4 = lambda g: (g, 0, 0, 0)
    return pl.pallas_call(
        functools.partial(_s5_kernel, n_batch, n_chunks),
        grid=(S5_GROUPS,),
        in_specs=[
            pl.BlockSpec((None, rows, wide), grp),
            pl.BlockSpec((None, None, wide, wide), lambda g: (layer, g, 0, 0)),
            pl.BlockSpec((None, None, 2, wide, LANES), lambda g: (layer, g, 0, 0, 0)),
            pl.BlockSpec((None, None, 2, LANES, wide), lambda g: (layer, g, 0, 0, 0)),
            pl.BlockSpec((None, None, 2, 1, LANES), lambda g: (layer, g, 0, 0, 0)),
            pl.BlockSpec((None, None, 2, n_batch, LANES), lambda g: (h0_layer, g, 0, 0, 0)),
        ],
        out_specs=[
            pl.BlockSpec((None, rows, wide), grp),
            pl.BlockSpec((None, 2, n_batch, LANES), grp4),
        ],
        out_shape=[
            jax.ShapeDtypeStruct((S5_GROUPS, rows, wide), F32),
            jax.ShapeDtypeStruct((S5_GROUPS, 2, n_batch, LANES), F32),
        ],
        scratch_shapes=[pltpu.VMEM((2, rows, LANES), F32), pltpu.VMEM((4, rows, LANES), F32)],
        compiler_params=pltpu.CompilerParams(dimension_semantics=("arbitrary",)),
        name="s5",
    )(xg, kmat, bst, cst, lamt, h0)


def _s5_operators(a_re, a_im, log_step, b_re, b_im, c_re, c_im, d):
    hi = lax.Precision.HIGHEST
    n = S5_CHUNK
    wide = n * S5_CH
    ar, ai = a_re.astype(F32), a_im.astype(F32)
    step = jnp.exp(log_step.astype(F32))[..., None]
    zr, zi = ar * step, ai * step
    k = jnp.arange(n + 1, dtype=F32)[:, None, None, None]
    mag = jnp.exp(k * zr[None])
    pr, pi = mag * jnp.cos(k * zi[None]), mag * jnp.sin(k * zi[None])

    xr, xi = pr[1] - 1.0, pi[1]
    den = ar * ar + ai * ai
    fr, fi = ((xr * ar + xi * ai) / den)[:, :, None, :], ((xi * ar - xr * ai) / den)[:, :, None, :]
    br, bi = (jnp.swapaxes(b.astype(F32), 2, 3) for b in (b_re, b_im))
    bbr, bbi = fr * br - fi * bi, fr * bi + fi * br
    cr, ci = c_re.astype(F32), c_im.astype(F32)

    tt = np.arange(n)

    def state_readout(powers, dirn):
        qr = pr[powers, dirn].transpose(1, 2, 0)[:, :, :, None]
        qi = pi[powers, dirn].transpose(1, 2, 0)[:, :, :, None]
        ctr = jnp.swapaxes(cr[dirn], 1, 2)[:, :, None, :]
        cti = jnp.swapaxes(ci[dirn], 1, 2)[:, :, None, :]
        rr = ctr * qr - cti * qi
        ri = -(ctr * qi + cti * qr)
        return rr.reshape(S5_GROUPS, S5_STATE, wide), ri.reshape(S5_GROUPS, S5_STATE, wide)

    def lag_response(powers, dirn):
        rr, ri = state_readout(powers, dirn)
        return (jnp.einsum('gkp,gpn->gkn', bbr[dirn], rr, precision=hi)
                + jnp.einsum('gkp,gpn->gkn', bbi[dirn], ri, precision=hi))

    fwd, bwd = lag_response(tt, 0), lag_response(n - 1 - tt, 1)
    skip = d.astype(F32)[:, None, :] * jnp.eye(S5_CH, dtype=F32)[None]
    strip = jnp.concatenate([bwd[:, :, :wide - S5_CH], bwd[:, :, wide - S5_CH:] + fwd[:, :, :S5_CH] + skip,
                             fwd[:, :, S5_CH:]], axis=-1)
    kmat = jnp.stack([strip[:, :, (n - 1 - s) * S5_CH:(n - 1 - s) * S5_CH + wide] for s in range(n)], axis=1)
    kmat = kmat.reshape(S5_GROUPS, wide, wide)

    def edge_states(powers, dirn):
        qr = jnp.swapaxes(pr[powers, dirn], 0, 1)[:, :, None, :]
        qi = jnp.swapaxes(pi[powers, dirn], 0, 1)[:, :, None, :]
        er = qr * bbr[dirn][:, None] - qi * bbi[dirn][:, None]
        ei = qr * bbi[dirn][:, None] + qi * bbr[dirn][:, None]
        return er.reshape(S5_GROUPS, wide, S5_STATE), ei.reshape(S5_GROUPS, wide, S5_STATE)

    (bf_r, bf_i), (bb_r, bb_i) = edge_states(n - 1 - tt, 0), edge_states(tt, 1)
    bst = jnp.stack([jnp.concatenate([bf_r, bb_r], axis=2), jnp.concatenate([bf_i, bb_i], axis=2)], axis=1)
    (cf_r, cf_i), (cb_r, cb_i) = state_readout(tt + 1, 0), state_readout(n - tt, 1)
    cst = jnp.stack([jnp.concatenate([cf_r, cb_r], axis=1), jnp.concatenate([cf_i, cb_i], axis=1)], axis=1)
    lamt = jnp.stack([jnp.concatenate([pr[n, 0], pr[n, 1]], axis=1),
                      jnp.concatenate([pi[n, 0], pi[n, 1]], axis=1)], axis=1)
    return kmat.astype(BF16), bst.astype(BF16), cst.astype(BF16), lamt[:, :, None, :]


def _rope_tables(n_pos):
    n = DA_QK // 4
    inv = ROPE_THETA ** (-jnp.arange(n, dtype=F32) / n)
    pos = jnp.arange(n_pos, dtype=jnp.int32)
    ang_r = (pos // GRID_W).astype(F32)[:, None] * inv[None, :]
    ang_c = (pos % GRID_W).astype(F32)[:, None] * inv[None, :]
    cos32 = jnp.concatenate([jnp.cos(ang_r)] * 2 + [jnp.cos(ang_c)] * 2, axis=1)
    sin32 = jnp.concatenate([-jnp.sin(ang_r), jnp.sin(ang_r), -jnp.sin(ang_c), jnp.sin(ang_c)], axis=1)
    pad = LANES - MLA_ROPE
    cosm = jnp.concatenate([cos32, jnp.ones((n_pos, pad), F32)], axis=1)
    sinm = jnp.concatenate([sin32, jnp.zeros((n_pos, pad), F32)], axis=1)
    return jnp.tile(cos32, (1, 4)), jnp.tile(sin32, (1, 4)), cosm, sinm


def _prep_inproj(w):
    o = np.cumsum([0, DA_WIDTH, DA_WIDTH, DA_WIDTH, MLA_Q_RANK, MLA_KV_RANK, MLA_ROPE, S5_WIDTH])
    wq, wk, wv, wcq, wckv, wkpe, wu = (w[..., o[i]:o[i + 1]] for i in range(7))
    wkpe = jnp.pad(wkpe, ((0, 0), (0, 0), (0, LANES - MLA_ROPE)))
    return jnp.concatenate([wq, wk, wv, wcq, wckv, wu, wkpe], axis=-1).astype(BF16)


def _prep_mla(q_norm_w, w_q_up, kv_norm_w, w_kv_up):
    pad = LANES - MLA_ROPE - MLA_NOPE
    wq = w_q_up.reshape(DEPTH, MLA_Q_RANK, MLA_HEADS, MLA_NOPE + MLA_ROPE)
    wqup = jnp.pad(jnp.concatenate([wq[..., MLA_NOPE:], wq[..., :MLA_NOPE]], axis=-1),
                   ((0, 0), (0, 0), (0, 0), (0, pad))).reshape(DEPTH, MLA_Q_RANK, MLA_HEADS * LANES)
    wkv = w_kv_up.reshape(DEPTH, MLA_KV_RANK, MLA_HEADS, MLA_NOPE + MLA_V)
    wk = jnp.pad(wkv[..., :MLA_NOPE], ((0, 0), (0, 0), (0, 0), (MLA_ROPE, pad)))
    wk = wk.reshape(DEPTH, MLA_KV_RANK, MLA_HEADS * LANES)
    wv = wkv[..., MLA_NOPE:].reshape(DEPTH, MLA_KV_RANK, MLA_WIDTH)
    return (q_norm_w.reshape(DEPTH, 1, MLA_Q_RANK), kv_norm_w.reshape(DEPTH, 1, MLA_KV_RANK),
            wqup.astype(BF16), wk.astype(BF16), wv.astype(BF16))


def _cached_slabs(cache, transposed):
    b, _, n = cache.shape[:3]
    a = cache.reshape(b, DEPTH, n, N_SLABS, LANES).astype(BF16)
    a = a.transpose(1, 3, 4, 0, 2) if transposed else a.transpose(1, 3, 0, 2, 4)
    return a.reshape(DEPTH, N_SLABS, LANES, b * n) if transposed else a.reshape(DEPTH, N_SLABS, b * n, LANES)


def kernel(x_prompt, x_sample, cache_diff_k, cache_diff_v, cache_mla_ckv, cache_mla_kpe, state_s5_re, state_s5_im, c, c_ctx, w_ada, b_ada, norm_w, ffn_w_in, ffn_w_out, w_in, diff_lambda, diff_subln_w, mla_q_norm_w, mla_w_q_up, mla_kv_norm_w, mla_w_kv_up, s5_a_re, s5_a_im, s5_log_step, s5_b_re, s5_b_im, s5_c_re, s5_c_im, s5_d, s5_w_glu, s5_b_glu, w_out, final_norm_w):
    n_ctx, seq, _ = x_prompt.shape
    n_dec, dec_seq, _ = x_sample.shape
    past = cache_diff_k.shape[2]

    cvec = jnp.concatenate(
        [c_ctx[None, :], c, jnp.zeros((MOD_ROWS - 1 - n_dec, D_MODEL), F32)], axis=0)
    mod = _adaln(cvec, w_ada, b_ada)
    rope_tabs = _rope_tables(dec_seq)

    groups = {
        'ctx': dict(x=x_prompt.reshape(n_ctx * seq, D_MODEL), seg=n_ctx * seq, row=0, nb=n_ctx),
        'dec': dict(x=x_sample.reshape(n_dec * dec_seq, D_MODEL), seg=dec_seq, row=1, nb=n_dec),
    }
    ffn_w = (ffn_w_in.astype(BF16), ffn_w_out.astype(BF16))
    norm_w4 = norm_w.reshape(DEPTH, 3, 1, D_MODEL)
    w_proj = _prep_inproj(w_in)
    mla_w = _prep_mla(mla_q_norm_w, mla_w_q_up, mla_kv_norm_w, mla_w_kv_up)
    mix_w = (s5_w_glu.astype(BF16), s5_b_glu.reshape(DEPTH, 1, S5_WIDTH), w_out.astype(BF16))
    s5_ops = jax.vmap(_s5_operators)(s5_a_re, s5_a_im, s5_log_step, s5_b_re, s5_b_im, s5_c_re, s5_c_im, s5_d)
    subw = jnp.tile(diff_subln_w, (1, 2)).reshape(DEPTH, 1, LANES)
    ctx_k, ctx_vt = _cached_slabs(cache_diff_k, False), _cached_slabs(cache_diff_v, True)
    ctx_kpe = jnp.pad(cache_mla_kpe, ((0, 0), (0, 0), (0, 0), (0, LANES - MLA_ROPE)))
    pack_dirs = lambda s: jnp.concatenate([s[:, :, 0], s[:, :, 1]], axis=-1)
    h0_dec = jnp.stack([pack_dirs(state_s5_re), pack_dirs(state_s5_im)], 0).transpose(2, 3, 0, 1, 4)
    h0_ctx = jnp.zeros((1, S5_GROUPS, 2, n_ctx, LANES), F32)
    caches = None
    s5_states = []
    for l in range(DEPTH):
        lam_init = 0.8 - 0.6 * math.exp(-0.3 * l)
        final_w = final_norm_w if l == DEPTH - 1 else None
        for name, g in groups.items():
            dec = name == 'dec'
            x = _ffn_call(g['x'], mod, g['seg'], g['row'], norm_w4, l, 0, ffn_w)
            proj = _inproj_call(x, mod, g['seg'], g['row'], norm_w4, l, w_proj, mla_w,
                                rope_tabs if dec else None, cache_out=None if dec else (seq, caches))
            q, k, vt, qm, km, vmt, xg = proj[:7]
            parts = [(k, vt, km, vmt, g['x'].shape[0] // g['nb'])]
            if dec:
                ckm, cvmt = _mla_expand_call(cache_mla_ckv, ctx_kpe, mla_w[3], mla_w[4], l)
                parts.append((ctx_k, ctx_vt, ckm, cvmt, past))
            o_a, o_b = _attn_call(q, qm, diff_lambda, subw, parts, g['nb'], l, lam_init)
            y_s5, fin = _s5_call(xg, *s5_ops, h0_dec if dec else h0_ctx, g['nb'], l, l if dec else 0)
            g['x'] = _ffn_call(x, mod, g['seg'], g['row'], norm_w4, l, 1, ffn_w,
                               mix_args=(o_a, o_b, y_s5), mix_w=mix_w, final_w=final_w)
            if not dec:
                caches = proj[7:]
                fin = fin.reshape(S5_GROUPS, 2, n_ctx, 2, S5_STATE).transpose(1, 2, 3, 0, 4)
                s5_states.append((fin[0], fin[1]))

    y_prompt = groups['ctx']['x'].reshape(n_ctx, seq, D_MODEL)
    y_sample = groups['dec']['x'].reshape(n_dec, dec_seq, D_MODEL)
    new_k, new_v, new_ckv, new_kpe = caches
    new_s5_re = jnp.stack([t[0] for t in s5_states], axis=1)
    new_s5_im = jnp.stack([t[1] for t in s5_states], axis=1)
    return (y_prompt, y_sample,
            new_k.reshape(n_ctx, DEPTH, seq, DA_HEADS, 2 * DA_QK),
            new_v.reshape(n_ctx, DEPTH, seq, DA_HEADS, DA_V),
            new_ckv, new_kpe, new_s5_re, new_s5_im)
```

```python
import functools
import math

import jax
import jax.numpy as jnp
import numpy as np
from jax import lax
from jax.experimental import pallas as pl
from jax.experimental.pallas import tpu as pltpu

F32 = jnp.float32
BF16 = jnp.bfloat16

D_MODEL = 1024
DEPTH = 2
GRID_W = 64
ROPE_THETA = 10000.0
EPS = 1e-6
N_MOD = 9
D_FF = 2816
DA_HEADS = 6
DA_QK = 32
DA_V = 64
DA_WIDTH = DA_HEADS * DA_V
MLA_HEADS = 6
MLA_Q_RANK = 256
MLA_KV_RANK = 128
MLA_NOPE = 64
MLA_ROPE = 32
MLA_V = 64
MLA_WIDTH = MLA_HEADS * MLA_V
S5_WIDTH = D_MODEL - DA_WIDTH - MLA_WIDTH
S5_CH = 16
S5_GROUPS = S5_WIDTH // S5_CH
S5_STATE = 64

LANES = 128
FF_CHUNK = 256
N_FF_CHUNKS = D_FF // FF_CHUNK
TOKEN_TILE = 512
Q_TILE = 256
S5_CHUNK = 16
N_SLABS = DA_WIDTH // LANES
MOD_ROWS = 8
VMEM_LIMIT = 56 * 1024 * 1024


def _rms(x, w):
    return x * lax.rsqrt(jnp.mean(x * x, axis=-1, keepdims=True) + EPS) * w


def _silu(x):
    return x * jax.nn.sigmoid(x)


def _gelu_tanh(x):
    c = math.sqrt(2.0 / math.pi)
    return x * (0.5 * (1.0 + jnp.tanh(c * (x + 0.044715 * (x * x * x)))))


def _dot(a, b):
    return jnp.dot(a, b, preferred_element_type=F32)


def _resident(shape):
    nd = len(shape)
    return pl.BlockSpec(shape, lambda *_: (0,) * nd, pipeline_mode=pl.Buffered(1))


def _resident_at(shape, *lead):
    nd = len(shape)
    return pl.BlockSpec((None,) * len(lead) + tuple(shape), lambda *_: tuple(lead) + (0,) * nd,
                        pipeline_mode=pl.Buffered(1))


def _adaln_kernel(c_ref, w_ref, b_ref, o_ref):
    s = _silu(c_ref[...])
    o_ref[...] = jnp.dot(s, w_ref[...], preferred_element_type=F32,
                         precision=lax.Precision.HIGHEST) + b_ref[...]


def _adaln(cvec, w_ada, b_ada):
    n = N_MOD * D_MODEL
    tn = 1152
    out = pl.pallas_call(
        _adaln_kernel,
        grid=(DEPTH, n // tn),
        in_specs=[
            pl.BlockSpec((MOD_ROWS, D_MODEL), lambda l, j: (0, 0)),
            pl.BlockSpec((None, D_MODEL, tn), lambda l, j: (l, 0, j)),
            pl.BlockSpec((None, 1, tn), lambda l, j: (l, 0, j)),
        ],
        out_specs=pl.BlockSpec((None, MOD_ROWS, tn), lambda l, j: (l, 0, j)),
        out_shape=jax.ShapeDtypeStruct((DEPTH, MOD_ROWS, n), F32),
        compiler_params=pltpu.CompilerParams(
            dimension_semantics=("arbitrary", "arbitrary"), vmem_limit_bytes=VMEM_LIMIT),
        name="adaln",
    )(cvec, w_ada, b_ada.reshape(DEPTH, 1, n))
    return out.reshape(DEPTH, MOD_ROWS, N_MOD, D_MODEL)


_ROWS_PER_TILE = TOKEN_TILE // S5_CHUNK
_SLOTS = LANES // S5_CH


def _to_group_major(u_scr, xg_ref, row0, n_rows):
    c0, n_chunks = row0 // S5_CHUNK, n_rows // S5_CHUNK
    slot = lax.broadcasted_iota(jnp.int32, (n_chunks, LANES), 1) // S5_CH
    for g in range(S5_GROUPS):
        for tt in range(S5_CHUNK // _SLOTS):
            acc = None
            for j in range(_SLOTS):
                t = tt * _SLOTS + j
                src = u_scr[g // _SLOTS, pl.ds(row0 + t, n_chunks, stride=S5_CHUNK), :]
                shift = ((j - g % _SLOTS) % _SLOTS) * S5_CH
                piece = pltpu.roll(src, shift, 1) if shift else src
                acc = piece if acc is None else jnp.where(slot == j, piece, acc)
            xg_ref[g, c0:c0 + n_chunks, tt * LANES:(tt + 1) * LANES] = acc.astype(BF16)
        yield


def _from_group_major(yg_ref, y_scr, row0, n_rows):
    c0, n_chunks = row0 // S5_CHUNK, n_rows // S5_CHUNK
    slot = lax.broadcasted_iota(jnp.int32, (n_chunks, LANES), 1) // S5_CH
    for t in range(S5_CHUNK):
        for gc in range(S5_GROUPS // _SLOTS):
            acc = None
            for m in range(_SLOTS):
                src = yg_ref[gc * _SLOTS + m, c0:c0 + n_chunks, (t // _SLOTS) * LANES:(t // _SLOTS + 1) * LANES]
                shift = ((m - t % _SLOTS) % _SLOTS) * S5_CH
                piece = pltpu.roll(src, shift, 1) if shift else src
                acc = piece if acc is None else jnp.where(slot == m, piece, acc)
            y_scr[gc, pl.ds(row0 + t, n_chunks, stride=S5_CHUNK), :] = acc
        yield


def _interleave(stages):
    stages = list(stages)
    while stages:
        for g in list(stages):
            try:
                next(g)
            except StopIteration:
                stages.remove(g)


def _delayed(gen, steps):
    for _ in range(steps):
        yield
    yield from gen


_MIX_PARTS = 2
_MIX_STAGGER = 8


def _ffn_kernel(mix, final, mod_base, *refs):
    it = iter(refs)
    x_ref, mod_ref, nw_ref, wi_ref, wo_ref = (next(it) for _ in range(5))
    if mix:
        oa_ref, ob_ref, ys_ref, wglu_ref, bglu_ref, wout_ref = (next(it) for _ in range(6))
    if final:
        fnw_ref = next(it)
    out_ref, h_scr, acc_scr = next(it), next(it), next(it)

    if mix:
        y_scr = next(it)
    shift = mod_ref[0, mod_base:mod_base + 1, :]
    scale = mod_ref[0, mod_base + 1:mod_base + 2, :]
    gate = mod_ref[0, mod_base + 2:mod_base + 3, :]
    tm = x_ref.shape[0]
    n_part = _MIX_PARTS if mix else 1
    rows_part = tm // n_part

    def chunk_cols(j):
        col = j * FF_CHUNK
        return slice(col, col + FF_CHUNK), slice(D_FF + col, D_FF + col + FF_CHUNK)

    def prologue(p):
        r0 = p * rows_part
        rows = slice(r0, r0 + rows_part)
        x = x_ref[rows, :]
        if mix:
            yield from _from_group_major(ys_ref, y_scr, r0, rows_part)
            yc = _gelu_tanh(jnp.concatenate([y_scr[gc, rows, :] for gc in range(S5_WIDTH // LANES)], axis=1))
            oc = yc * jax.nn.sigmoid(_dot(yc.astype(BF16), wglu_ref[...]) + bglu_ref[...])
            yield
            mixed = jnp.concatenate([oa_ref[s, rows, :] for s in range(N_SLABS)]
                                    + [ob_ref[s, rows, :] for s in range(N_SLABS)]
                                    + [oc.astype(BF16)], axis=1)
            x = x + mod_ref[0, mod_base - 1:mod_base, :] * _dot(mixed, wout_ref[...])
            out_ref[rows, :] = x
            yield
        h = _rms(x, nw_ref[...]) * (1.0 + scale) + shift
        h_scr[rows, :] = h.astype(BF16)
        yield

    _interleave(_delayed(prologue(p), p * _MIX_STAGGER) for p in range(n_part))
    acc_scr[...] = jnp.zeros_like(acc_scr)

    for j in range(N_FF_CHUNKS):
        hb = h_scr[...]
        ca, cg = chunk_cols(j)
        a = _dot(hb, wi_ref[:, ca].astype(BF16))
        g = _dot(hb, wi_ref[:, cg].astype(BF16))
        acc_scr[...] += _dot((_silu(g) * a).astype(BF16), wo_ref[ca, :].astype(BF16))
    y = (out_ref[...] if mix else x_ref[...]) + 0.5 * gate * acc_scr[...]
    if final:
        y = _rms(y, fnw_ref[...])
    out_ref[...] = y


def _mod_spec(layer, seg_len, row_base):
    return pl.BlockSpec((None, 1, N_MOD, D_MODEL),
                        lambda i: (layer, row_base + (i * TOKEN_TILE) // seg_len, 0, 0))


def _ffn_call(x, mod, seg_len, row_base, norm_w4, layer, which, ffn_w, mix_args=None, mix_w=None, final_w=None):
    t = x.shape[0]
    tm = TOKEN_TILE
    mix = mix_args is not None
    final = final_w is not None
    w_in_all, w_out_all = ffn_w
    tok = lambda i: (i, 0)
    in_specs = [
        pl.BlockSpec((tm, D_MODEL), tok),
        _mod_spec(layer, seg_len, row_base),
        _resident_at((1, D_MODEL), layer, 2 * which),
        _resident_at((D_MODEL, 2 * D_FF), layer, which),
        _resident_at((D_FF, D_MODEL), layer, which),
    ]
    args = [x, mod, norm_w4, w_in_all, w_out_all]
    if mix:
        in_specs += [
            pl.BlockSpec((N_SLABS, tm, LANES), lambda i: (0, i, 0)),
            pl.BlockSpec((N_SLABS, tm, LANES), lambda i: (0, i, 0)),
            pl.BlockSpec((S5_GROUPS, _ROWS_PER_TILE, S5_CHUNK * S5_CH), lambda i: (0, i, 0)),
            _resident_at((S5_WIDTH, S5_WIDTH), layer),
            _resident_at((1, S5_WIDTH), layer),
            _resident_at((D_MODEL, D_MODEL), layer),
        ]
        args += [*mix_args, *mix_w]
    if final:
        in_specs.append(_resident((1, D_MODEL)))
        args.append(final_w.reshape(1, D_MODEL))
    return pl.pallas_call(
        functools.partial(_ffn_kernel, mix, final, 6 * which),
        grid=(t // tm,),
        in_specs=in_specs,
        out_specs=pl.BlockSpec((tm, D_MODEL), tok),
        out_shape=jax.ShapeDtypeStruct((t, D_MODEL), F32),
        scratch_shapes=[pltpu.VMEM((tm, D_MODEL), BF16), pltpu.VMEM((tm, D_MODEL), F32)]
        + ([pltpu.VMEM((S5_WIDTH // LANES, tm, LANES), F32)] if mix else []),
        compiler_params=pltpu.CompilerParams(
            dimension_semantics=("arbitrary",), vmem_limit_bytes=VMEM_LIMIT),
        name="mix_ffn" if mix else "ffn",
    )(*args)


_C_Q, _C_K, _C_V = 0, DA_WIDTH, 2 * DA_WIDTH
_C_CQ = 3 * DA_WIDTH
_C_CKV = _C_CQ + MLA_Q_RANK
_C_U = _C_CKV + MLA_KV_RANK
_C_KPE = _C_U + S5_WIDTH
_NCOL = _C_KPE + LANES

_QSCALE_DIFF = DA_QK ** -0.5 * math.log2(math.e)
_QSCALE_MLA = (MLA_NOPE + MLA_ROPE) ** -0.5 * math.log2(math.e)


_INPROJ_PARTS = 2
_INPROJ_STAGGER = 6


def _rotate_half(x):
    half = DA_QK // 4
    lane = lax.broadcasted_iota(jnp.int32, x.shape, 1)
    return jnp.where(lane % (2 * half) < half, pltpu.roll(x, LANES - half, 1), pltpu.roll(x, half, 1))


def _inproj_kernel(rope, n_aliased, *refs):
    it = iter(refs)
    x_ref, mod_ref, nw_ref, w_ref, qnw_ref, kvnw_ref, wqup_ref, wk_ref, wv_ref = (next(it) for _ in range(9))
    if rope:
        cos_ref, sin_ref, cosm_ref, sinm_ref = (next(it) for _ in range(4))
    for _ in range(n_aliased):
        next(it)
    q_ref, k_ref, vt_ref, qm_ref, km_ref, vmt_ref, xg_ref = (next(it) for _ in range(7))
    if not rope:
        ka_ref, va_ref, ckv_ref, kpe_ref = (next(it) for _ in range(4))
    u_scr = next(it)

    tm = x_ref.shape[0]
    n_part = _INPROJ_PARTS
    rows_part = tm // n_part

    def part(p):
        r0 = p * rows_part
        rows = slice(r0, r0 + rows_part)
        h = _rms(x_ref[rows, :], nw_ref[...]) * (1.0 + mod_ref[0, 4:5, :]) + mod_ref[0, 3:4, :]
        h = h.astype(BF16)
        yield
        qkv = _dot(h, w_ref[:, :_C_CQ])
        yield
        rest = _dot(h, w_ref[:, _C_CQ:])
        yield

        def cols(start, width):
            src, off = (qkv, 0) if start < _C_CQ else (rest, _C_CQ)
            return src[:, start - off:start - off + width]

        if rope:
            cos, sin, cosm, sinm = (r[rows, :] for r in (cos_ref, sin_ref, cosm_ref, sinm_ref))
        for s in range(N_SLABS):
            qs = cols(_C_Q + s * LANES, LANES)
            ks = cols(_C_K + s * LANES, LANES)
            if rope:
                qs = qs * cos + _rotate_half(qs) * sin
                ks = ks * cos + _rotate_half(ks) * sin
            q_ref[s, rows, :] = (qs * _QSCALE_DIFF).astype(BF16)
            k_ref[s, rows, :] = ks.astype(BF16)
            vt_ref[s, :, rows] = cols(_C_V + s * LANES, LANES).T.astype(BF16)
            yield
        for gc in range(S5_WIDTH // LANES):
            u_scr[gc, rows, :] = cols(_C_U + gc * LANES, LANES)
        yield from _to_group_major(u_scr, xg_ref, r0, rows_part)

        cqn = _rms(cols(_C_CQ, MLA_Q_RANK), qnw_ref[...]).astype(BF16)
        qm = _dot(cqn, wqup_ref[...])
        ckvn = _rms(cols(_C_CKV, MLA_KV_RANK), kvnw_ref[...])
        ckvb = ckvn.astype(BF16)
        kn = _dot(ckvb, wk_ref[...])
        vm = _dot(ckvb, wv_ref[...])
        yield
        kpe = cols(_C_KPE, LANES)
        if rope:
            kpe = kpe * cosm + _rotate_half(kpe) * sinm
        for hd in range(MLA_HEADS):
            qh = qm[:, hd * LANES:(hd + 1) * LANES]
            if rope:
                qh = qh * cosm + _rotate_half(qh) * sinm
            qm_ref[hd, rows, :] = (qh * _QSCALE_MLA).astype(BF16)
            km_ref[hd, rows, :] = (kn[:, hd * LANES:(hd + 1) * LANES] + kpe).astype(BF16)
            yield
        for s in range(N_SLABS):
            vmt_ref[s, :, rows] = vm[:, s * LANES:(s + 1) * LANES].T.astype(BF16)

        if not rope:
            seq = ka_ref.shape[-2]
            new = (cols(_C_K, DA_WIDTH), cols(_C_V, DA_WIDTH), ckvn, cols(_C_KPE, MLA_ROPE))
            for b in range(r0 // seq, (r0 + rows_part) // seq):
                mine = slice(b * seq - r0, (b + 1) * seq - r0)
                for ref, val in zip((ka_ref, va_ref, ckv_ref, kpe_ref), new):
                    if len(ref.shape) == 3:
                        ref[b] = val[mine]
                    else:
                        ref[b, 0] = val[mine]
                        for later in range(1, ref.shape[1]):
                            ref[b, later] = jnp.zeros_like(val[mine])

    _interleave(_delayed(part(p), p * _INPROJ_STAGGER) for p in range(n_part))


def _inproj_call(x, mod, seg_len, row_base, norm_w4, layer, w, mla_w, rope_tabs, cache_out=None):
    t = x.shape[0]
    tm = TOKEN_TILE
    rope = rope_tabs is not None
    qnw, kvnw, wqup, wk, wv = mla_w
    tok = lambda i: (i, 0)
    slab = lambda i: (0, i, 0)
    slab_t = lambda i: (0, 0, i)
    in_specs = [
        pl.BlockSpec((tm, D_MODEL), tok),
        _mod_spec(layer, seg_len, row_base),
        _resident_at((1, D_MODEL), layer, 1),
        _resident_at((D_MODEL, _NCOL), layer),
        _resident_at((1, MLA_Q_RANK), layer),
        _resident_at((1, MLA_KV_RANK), layer),
        _resident_at((MLA_Q_RANK, MLA_HEADS * LANES), layer),
        _resident_at((MLA_KV_RANK, MLA_HEADS * LANES), layer),
        _resident_at((MLA_KV_RANK, MLA_WIDTH), layer),
    ]
    args = [x, mod, norm_w4, w, qnw, kvnw, wqup, wk, wv]
    if rope:
        n_pos = rope_tabs[0].shape[0]
        pos = lambda i: (i % (n_pos // tm), 0)
        in_specs += [pl.BlockSpec((tm, LANES), pos)] * 4
        args += list(rope_tabs)
    out_shape = [
        jax.ShapeDtypeStruct((N_SLABS, t, LANES), BF16),
        jax.ShapeDtypeStruct((N_SLABS, t, LANES), BF16),
        jax.ShapeDtypeStruct((N_SLABS, LANES, t), BF16),
        jax.ShapeDtypeStruct((MLA_HEADS, t, LANES), BF16),
        jax.ShapeDtypeStruct((MLA_HEADS, t, LANES), BF16),
        jax.ShapeDtypeStruct((N_SLABS, LANES, t), BF16),
        jax.ShapeDtypeStruct((S5_GROUPS, t // S5_CHUNK, S5_CHUNK * S5_CH), BF16),
    ]
    out_specs = [
        pl.BlockSpec((N_SLABS, tm, LANES), slab),
        pl.BlockSpec((N_SLABS, tm, LANES), slab),
        pl.BlockSpec((N_SLABS, LANES, tm), slab_t),
        pl.BlockSpec((MLA_HEADS, tm, LANES), slab),
        pl.BlockSpec((MLA_HEADS, tm, LANES), slab),
        pl.BlockSpec((N_SLABS, LANES, tm), slab_t),
        pl.BlockSpec((S5_GROUPS, _ROWS_PER_TILE, S5_CHUNK * S5_CH), slab),
    ]
    aliases = {}
    if not rope:
        seq, stacked = cache_out
        n_first = len(out_shape)
        for k, width in enumerate((DA_WIDTH, DA_WIDTH, MLA_KV_RANK, MLA_ROPE)):
            out_shape.append(jax.ShapeDtypeStruct((t // seq, DEPTH, seq, width), F32))
            if stacked is None:
                out_specs.append(pl.BlockSpec((tm // seq, DEPTH, seq, width), lambda i: (i, 0, 0, 0)))
            else:
                out_specs.append(pl.BlockSpec((tm // seq, None, seq, width), lambda i: (i, layer, 0, 0)))
                aliases[len(args)] = n_first + k
                in_specs.append(pl.BlockSpec(memory_space=pl.ANY))
                args.append(stacked[k])
    return pl.pallas_call(
        functools.partial(_inproj_kernel, rope, len(aliases)),
        grid=(t // tm,),
        in_specs=in_specs,
        out_specs=out_specs,
        out_shape=out_shape,
        input_output_aliases=aliases,
        scratch_shapes=[pltpu.VMEM((S5_WIDTH // LANES, tm, LANES), F32)],
        compiler_params=pltpu.CompilerParams(
            dimension_semantics=("arbitrary",), vmem_limit_bytes=VMEM_LIMIT),
        name="inproj_dec" if rope else "inproj_ctx",
    )(*args)


def _mla_expand_kernel(ckv_ref, kpe_ref, wk_ref, wv_ref, km_ref, vmt_ref):
    ckvb = ckv_ref[...].astype(BF16)
    kn = _dot(ckvb, wk_ref[...])
    vm = _dot(ckvb, wv_ref[...])
    kpe = kpe_ref[...]
    for hd in range(MLA_HEADS):
        km_ref[hd] = (kn[:, hd * LANES:(hd + 1) * LANES] + kpe).astype(BF16)
    for s in range(N_SLABS):
        vmt_ref[s] = vm[:, s * LANES:(s + 1) * LANES].T.astype(BF16)


def _mla_expand_call(ckv, kpe128, wk, wv, layer):
    n_batch, _, rows, _ = ckv.shape
    t = n_batch * rows
    return pl.pallas_call(
        _mla_expand_kernel,
        grid=(n_batch,),
        in_specs=[
            pl.BlockSpec((None, None, rows, MLA_KV_RANK), lambda i: (i, layer, 0, 0)),
            pl.BlockSpec((None, None, rows, LANES), lambda i: (i, layer, 0, 0)),
            _resident_at((MLA_KV_RANK, MLA_HEADS * LANES), layer),
            _resident_at((MLA_KV_RANK, MLA_WIDTH), layer),
        ],
        out_specs=[
            pl.BlockSpec((MLA_HEADS, rows, LANES), lambda i: (0, i, 0)),
            pl.BlockSpec((N_SLABS, LANES, rows), lambda i: (0, 0, i)),
        ],
        out_shape=[
            jax.ShapeDtypeStruct((MLA_HEADS, t, LANES), BF16),
            jax.ShapeDtypeStruct((N_SLABS, LANES, t), BF16),
        ],
        compiler_params=pltpu.CompilerParams(dimension_semantics=("arbitrary",)),
        name="mla_expand",
    )(ckv, kpe128, wk, wv)


KEY_TILE = 256
SHORT_SEQS_PER_STEP = 2
N_SCORE_BUFS = 4


def _attn_kernel(n_parts, lam_init, n_seq, *refs):
    q_ref, qm_ref, lamp_ref, subw_ref = refs[:4]
    parts = [refs[4 + 4 * i:8 + 4 * i] for i in range(n_parts)]
    oa_ref, ob_ref, sbuf, qt_scr = refs[4 + 4 * n_parts:]

    lp = lamp_ref[...]
    lam = (jnp.exp(jnp.sum(lp[0:1] * lp[1:2], keepdims=True))
           - jnp.exp(jnp.sum(lp[2:3] * lp[3:4], keepdims=True)) + lam_init)
    tq = q_ref.shape[1] // n_seq
    low_half = lax.broadcasted_iota(jnp.int32, (tq, LANES), 1) < DA_V
    channel = lax.broadcasted_iota(jnp.int32, (LANES, tq), 0)
    subw = subw_ref[...] * (1.0 - lam_init)
    sublanes = 8
    n_qt = N_SLABS + MLA_HEADS

    tiles, row = [], 0
    for p, (k, _, _, _) in enumerate(parts):
        per_seq = k.shape[1] // n_seq
        for k0 in range(0, per_seq, KEY_TILE):
            n = min(KEY_TILE, per_seq - k0)
            tiles.append((p, k0, n, slice(row, row + n)))
            row += n

    def key_slice(unit, p, k0, n):
        first = unit['seq'] * (parts[p][0].shape[1] // n_seq) + k0
        return slice(first, first + n)

    def row_fold(op, acc, x):
        for j in range(x.shape[0] // sublanes):
            piece = x[j * sublanes:(j + 1) * sublanes, :]
            acc = piece if acc is None else op(acc, piece)
        return acc

    for j in range(n_seq):
        mine = slice(j * tq, (j + 1) * tq)
        for s in range(N_SLABS):
            qt_scr[j * n_qt + s] = q_ref[s, mine, :].astype(F32).T.astype(BF16)
        for hd in range(MLA_HEADS):
            qt_scr[j * n_qt + N_SLABS + hd] = qm_ref[hd, mine, :].astype(F32).T.astype(BF16)

    def scores(unit):
        buf = unit['buf']
        if unit['mla']:
            rhs, kref, kidx = qt_scr[unit['seq'] * n_qt + N_SLABS + unit['head']], 2, unit['head']
        else:
            qt = qt_scr[unit['seq'] * n_qt + unit['slab']]
            rhs, kref, kidx = jnp.where(channel // DA_QK == unit['sub'], qt, jnp.zeros_like(qt)), 0, unit['slab']
        mx = None
        for p, k0, n, rows in tiles:
            s = _dot(parts[p][kref][kidx, key_slice(unit, p, k0, n), :], rhs)
            sbuf[buf, rows, :] = s
            mx = row_fold(jnp.maximum, mx, s)
            yield
        unit['max'] = jnp.max(mx, axis=0, keepdims=True)

    def expo(unit):
        buf, tot = unit['buf'], None
        for _, _, _, rows in tiles:
            e = jnp.exp2(sbuf[buf, rows, :] - unit['max'])
            sbuf[buf, rows, :] = e
            tot = row_fold(jnp.add, tot, e)
            yield
        unit['sum'] = jnp.sum(tot, axis=0, keepdims=True)

    def pv(head):
        units, slab = head['units'], head['slab']
        vref = 3 if units[0]['mla'] else 1
        l0 = units[0]['sum']
        if len(units) == 2:
            r = lam * l0 / units[1]['sum']
        acc = None
        for p, k0, n, rows in tiles:
            w = sbuf[units[0]['buf'], rows, :]
            if len(units) == 2:
                w = w - r * sbuf[units[1]['buf'], rows, :]
            o = _dot(parts[p][vref][slab, :, key_slice(units[0], p, k0, n)], w.astype(BF16))
            acc = o if acc is None else acc + o
            yield
        head['out'] = (acc * (1.0 / l0)).T

    def finish(lo, hi):
        slab, mine = hi['slab'], slice(hi['seq'] * tq, (hi['seq'] + 1) * tq)
        o = jnp.where(low_half, lo['out'], hi['out'])
        if hi['mla']:
            ob_ref[slab, mine, :] = o.astype(BF16)
            return
        o2 = o * o
        ss_lo = jnp.sum(jnp.where(low_half, o2, 0.0), axis=1, keepdims=True)
        ss_hi = jnp.sum(jnp.where(low_half, 0.0, o2), axis=1, keepdims=True)
        ms = jnp.where(low_half, ss_lo, ss_hi) * (1.0 / DA_V)
        oa_ref[slab, mine, :] = (o * lax.rsqrt(ms + EPS) * subw).astype(BF16)

    def stream(seq):
        units, heads = [], []
        for slab in range(N_SLABS):
            for mla in (False, True):
                for half in range(2):
                    mine = [dict(seq=seq, mla=mla, slab=slab, head=2 * slab + half, sub=2 * half + c)
                            for c in range(1 if mla else 2)]
                    heads.append(dict(seq=seq, units=mine, slab=slab, mla=mla, half=half,
                                      last=len(units) + len(mine) - 1))
                    units += mine
        for k, unit in enumerate(units):
            unit['buf'] = seq * N_SCORE_BUFS + k % N_SCORE_BUFS
        return units, heads

    streams = [stream(j) for j in range(n_seq)]
    n_units = len(streams[0][0])
    pending = {}
    for k in range(n_units + 2):
        stages, ready = [], []
        for units, heads in streams:
            if k < n_units:
                stages.append(scores(units[k]))
            if 1 <= k <= n_units:
                stages.append(expo(units[k - 1]))
            for h in heads:
                if h['last'] == k - 2:
                    stages.append(pv(h))
                    ready.append(h)
        _interleave(stages)
        for h in ready:
            key = (h['seq'], h['slab'], h['mla'])
            if h['half'] == 1:
                finish(pending.pop(key), h)
            else:
                pending[key] = h


def _attn_call(q, qm, lamp, subw, parts, n_batch, layer, lam_init):
    t = q.shape[1]
    seq = t // n_batch
    tq = min(Q_TILE, seq)
    nq = seq // tq
    n_seq = SHORT_SEQS_PER_STEP if nq == 1 and n_batch % SHORT_SEQS_PER_STEP == 0 else 1
    qmap = lambda b, i: (0, b * nq + i, 0)
    in_specs = [
        pl.BlockSpec((N_SLABS, n_seq * tq, LANES), qmap),
        pl.BlockSpec((MLA_HEADS, n_seq * tq, LANES), qmap),
        pl.BlockSpec((None, 4, DA_QK), lambda b, i: (layer, 0, 0)),
        pl.BlockSpec((None, 1, LANES), lambda b, i: (layer, 0, 0)),
    ]
    args = [q, qm, lamp, subw]

    def keys_spec(arr, n_keys, transposed):
        n_keys = n_seq * n_keys
        block = (arr.shape[-3], LANES, n_keys) if transposed else (arr.shape[-3], n_keys, LANES)
        at = (lambda b: (0, 0, b)) if transposed else (lambda b: (0, b, 0))
        if arr.ndim == 4:
            return pl.BlockSpec((None,) + block, lambda b, i: (layer,) + at(b))
        return pl.BlockSpec(block, lambda b, i: at(b))

    for k, vt, km, vmt, n_keys in parts:
        in_specs += [keys_spec(k, n_keys, False), keys_spec(vt, n_keys, True),
                     keys_spec(km, n_keys, False), keys_spec(vmt, n_keys, True)]
        args += [k, vt, km, vmt]
    return pl.pallas_call(
        functools.partial(_attn_kernel, len(parts), lam_init, n_seq),
        grid=(n_batch // n_seq, nq),
        in_specs=in_specs,
        out_specs=[pl.BlockSpec((N_SLABS, n_seq * tq, LANES), qmap)] * 2,
        out_shape=[jax.ShapeDtypeStruct((N_SLABS, t, LANES), BF16)] * 2,
        scratch_shapes=[pltpu.VMEM((n_seq * N_SCORE_BUFS, sum(p[4] for p in parts), tq), F32),
                        pltpu.VMEM((n_seq * (N_SLABS + MLA_HEADS), LANES, tq), BF16)],
        compiler_params=pltpu.CompilerParams(
            dimension_semantics=("arbitrary", "arbitrary"), vmem_limit_bytes=VMEM_LIMIT),
        name="attn_dec" if len(parts) > 1 else "attn_ctx",
    )(*args)


def _s5_kernel(n_batch, n_chunks, x_ref, kmat_ref, bst_ref, cst_ref, lamt_ref, h0_ref, y_ref, fin_ref,
               s_scr, hin_scr):
    x = x_ref[...]
    y = _dot(x, kmat_ref[...])
    for k in range(2):
        s_scr[k] = _dot(x, bst_ref[k])

    a_re, a_im = lamt_ref[0], lamt_ref[1]
    fwd_lanes = lax.broadcasted_iota(jnp.int32, (n_batch, LANES), 1) < S5_STATE

    def chunk_rows(j):
        return pl.ds(j, n_batch, stride=n_chunks)

    def step(j, carry):
        h_re, h_im = carry
        rf, rb = chunk_rows(j), chunk_rows(n_chunks - 1 - j)
        hin_scr[0, rf, :] = h_re
        hin_scr[1, rf, :] = h_im
        hin_scr[2, rb, :] = h_re
        hin_scr[3, rb, :] = h_im
        s_re = jnp.where(fwd_lanes, s_scr[0, rf, :], s_scr[0, rb, :])
        s_im = jnp.where(fwd_lanes, s_scr[1, rf, :], s_scr[1, rb, :])
        return a_re * h_re - a_im * h_im + s_re, a_re * h_im + a_im * h_re + s_im

    fin = lax.fori_loop(0, n_chunks, step, (h0_ref[0], h0_ref[1]))
    fwd_rows = lax.broadcasted_iota(jnp.int32, hin_scr.shape[1:], 1) < S5_STATE
    for k in range(2):
        fin_ref[k] = fin[k]
        hin = jnp.where(fwd_rows, hin_scr[k], hin_scr[2 + k])
        y += _dot(hin.astype(BF16), cst_ref[k])
    y_ref[...] = y


def _s5_call(xg, kmat, bst, cst, lamt, h0, n_batch, layer, h0_layer):
    rows = xg.shape[1]
    n_chunks = rows // n_batch
    wide = S5_CHUNK * S5_CH
    grp = lambda g: (g, 0, 0)
    grp4 = lambda g: (g, 0, 0, 0)
    return pl.pallas_call(
        functools.partial(_s5_kernel, n_batch, n_chunks),
        grid=(S5_GROUPS,),
        in_specs=[
            pl.BlockSpec((None, rows, wide), grp),
            pl.BlockSpec((None, None, wide, wide), lambda g: (layer, g, 0, 0)),
            pl.BlockSpec((None, None, 2, wide, LANES), lambda g: (layer, g, 0, 0, 0)),
            pl.BlockSpec((None, None, 2, LANES, wide), lambda g: (layer, g, 0, 0, 0)),
            pl.BlockSpec((None, None, 2, 1, LANES), lambda g: (layer, g, 0, 0, 0)),
            pl.BlockSpec((None, None, 2, n_batch, LANES), lambda g: (h0_layer, g, 0, 0, 0)),
        ],
        out_specs=[
            pl.BlockSpec((None, rows, wide), grp),
            pl.BlockSpec((None, 2, n_batch, LANES), grp4),
        ],
        out_shape=[
            jax.ShapeDtypeStruct((S5_GROUPS, rows, wide), F32),
            jax.ShapeDtypeStruct((S5_GROUPS, 2, n_batch, LANES), F32),
        ],
        scratch_shapes=[pltpu.VMEM((2, rows, LANES), F32), pltpu.VMEM((4, rows, LANES), F32)],
        compiler_params=pltpu.CompilerParams(dimension_semantics=("arbitrary",)),
        name="s5",
    )(xg, kmat, bst, cst, lamt, h0)


def _s5_operators(a_re, a_im, log_step, b_re, b_im, c_re, c_im, d):
    hi = lax.Precision.HIGHEST
    n = S5_CHUNK
    wide = n * S5_CH
    ar, ai = a_re.astype(F32), a_im.astype(F32)
    step = jnp.exp(log_step.astype(F32))[..., None]
    zr, zi = ar * step, ai * step
    k = jnp.arange(n + 1, dtype=F32)[:, None, None, None]
    mag = jnp.exp(k * zr[None])
    pr, pi = mag * jnp.cos(k * zi[None]), mag * jnp.sin(k * zi[None])

    xr, xi = pr[1] - 1.0, pi[1]
    den = ar * ar + ai * ai
    fr, fi = ((xr * ar + xi * ai) / den)[:, :, None, :], ((xi * ar - xr * ai) / den)[:, :, None, :]
    br, bi = (jnp.swapaxes(b.astype(F32), 2, 3) for b in (b_re, b_im))
    bbr, bbi = fr * br - fi * bi, fr * bi + fi * br
    cr, ci = c_re.astype(F32), c_im.astype(F32)

    tt = np.arange(n)

    def state_readout(powers, dirn):
        qr = pr[powers, dirn].transpose(1, 2, 0)[:, :, :, None]
        qi = pi[powers, dirn].transpose(1, 2, 0)[:, :, :, None]
        ctr = jnp.swapaxes(cr[dirn], 1, 2)[:, :, None, :]
        cti = jnp.swapaxes(ci[dirn], 1, 2)[:, :, None, :]
        rr = ctr * qr - cti * qi
        ri = -(ctr * qi + cti * qr)
        return rr.reshape(S5_GROUPS, S5_STATE, wide), ri.reshape(S5_GROUPS, S5_STATE, wide)

    def lag_response(powers, dirn):
        rr, ri = state_readout(powers, dirn)
        return (jnp.einsum('gkp,gpn->gkn', bbr[dirn], rr, precision=hi)
                + jnp.einsum('gkp,gpn->gkn', bbi[dirn], ri, precision=hi))

    fwd, bwd = lag_response(tt, 0), lag_response(n - 1 - tt, 1)
    skip = d.astype(F32)[:, None, :] * jnp.eye(S5_CH, dtype=F32)[None]
    strip = jnp.concatenate([bwd[:, :, :wide - S5_CH], bwd[:, :, wide - S5_CH:] + fwd[:, :, :S5_CH] + skip,
                             fwd[:, :, S5_CH:]], axis=-1)
    kmat = jnp.stack([strip[:, :, (n - 1 - s) * S5_CH:(n - 1 - s) * S5_CH + wide] for s in range(n)], axis=1)
    kmat = kmat.reshape(S5_GROUPS, wide, wide)

    def edge_states(powers, dirn):
        qr = jnp.swapaxes(pr[powers, dirn], 0, 1)[:, :, None, :]
        qi = jnp.swapaxes(pi[powers, dirn], 0, 1)[:, :, None, :]
        er = qr * bbr[dirn][:, None] - qi * bbi[dirn][:, None]
        ei = qr * bbi[dirn][:, None] + qi * bbr[dirn][:, None]
        return er.reshape(S5_GROUPS, wide, S5_STATE), ei.reshape(S5_GROUPS, wide, S5_STATE)

    (bf_r, bf_i), (bb_r, bb_i) = edge_states(n - 1 - tt, 0), edge_states(tt, 1)
    bst = jnp.stack([jnp.concatenate([bf_r, bb_r], axis=2), jnp.concatenate([bf_i, bb_i], axis=2)], axis=1)
    (cf_r, cf_i), (cb_r, cb_i) = state_readout(tt + 1, 0), state_readout(n - tt, 1)
    cst = jnp.stack([jnp.concatenate([cf_r, cb_r], axis=1), jnp.concatenate([cf_i, cb_i], axis=1)], axis=1)
    lamt = jnp.stack([jnp.concatenate([pr[n, 0], pr[n, 1]], axis=1),
                      jnp.concatenate([pi[n, 0], pi[n, 1]], axis=1)], axis=1)
    return kmat.astype(BF16), bst.astype(BF16), cst.astype(BF16), lamt[:, :, None, :]


def _rope_tables(n_pos):
    n = DA_QK // 4
    inv = ROPE_THETA ** (-jnp.arange(n, dtype=F32) / n)
    pos = jnp.arange(n_pos, dtype=jnp.int32)
    ang_r = (pos // GRID_W).astype(F32)[:, None] * inv[None, :]
    ang_c = (pos % GRID_W).astype(F32)[:, None] * inv[None, :]
    cos32 = jnp.concatenate([jnp.cos(ang_r)] * 2 + [jnp.cos(ang_c)] * 2, axis=1)
    sin32 = jnp.concatenate([-jnp.sin(ang_r), jnp.sin(ang_r), -jnp.sin(ang_c), jnp.sin(ang_c)], axis=1)
    pad = LANES - MLA_ROPE
    cosm = jnp.concatenate([cos32, jnp.ones((n_pos, pad), F32)], axis=1)
    sinm = jnp.concatenate([sin32, jnp.zeros((n_pos, pad), F32)], axis=1)
    return jnp.tile(cos32, (1, 4)), jnp.tile(sin32, (1, 4)), cosm, sinm


def _prep_inproj(w):
    o = np.cumsum([0, DA_WIDTH, DA_WIDTH, DA_WIDTH, MLA_Q_RANK, MLA_KV_RANK, MLA_ROPE, S5_WIDTH])
    wq, wk, wv, wcq, wckv, wkpe, wu = (w[..., o[i]:o[i + 1]] for i in range(7))
    wkpe = jnp.pad(wkpe, ((0, 0), (0, 0), (0, LANES - MLA_ROPE)))
    return jnp.concatenate([wq, wk, wv, wcq, wckv, wu, wkpe], axis=-1).astype(BF16)


def _prep_mla(q_norm_w, w_q_up, kv_norm_w, w_kv_up):
    pad = LANES - MLA_ROPE - MLA_NOPE
    wq = w_q_up.reshape(DEPTH, MLA_Q_RANK, MLA_HEADS, MLA_NOPE + MLA_ROPE)
    wqup = jnp.pad(jnp.concatenate([wq[..., MLA_NOPE:], wq[..., :MLA_NOPE]], axis=-1),
                   ((0, 0), (0, 0), (0, 0), (0, pad))).reshape(DEPTH, MLA_Q_RANK, MLA_HEADS * LANES)
    wkv = w_kv_up.reshape(DEPTH, MLA_KV_RANK, MLA_HEADS, MLA_NOPE + MLA_V)
    wk = jnp.pad(wkv[..., :MLA_NOPE], ((0, 0), (0, 0), (0, 0), (MLA_ROPE, pad)))
    wk = wk.reshape(DEPTH, MLA_KV_RANK, MLA_HEADS * LANES)
    wv = wkv[..., MLA_NOPE:].reshape(DEPTH, MLA_KV_RANK, MLA_WIDTH)
    return (q_norm_w.reshape(DEPTH, 1, MLA_Q_RANK), kv_norm_w.reshape(DEPTH, 1, MLA_KV_RANK),
            wqup.astype(BF16), wk.astype(BF16), wv.astype(BF16))


def _cached_slabs(cache, transposed):
    b, _, n = cache.shape[:3]
    a = cache.reshape(b, DEPTH, n, N_SLABS, LANES).astype(BF16)
    a = a.transpose(1, 3, 4, 0, 2) if transposed else a.transpose(1, 3, 0, 2, 4)
    return a.reshape(DEPTH, N_SLABS, LANES, b * n) if transposed else a.reshape(DEPTH, N_SLABS, b * n, LANES)


def kernel(x_prompt, x_sample, cache_diff_k, cache_diff_v, cache_mla_ckv, cache_mla_kpe, state_s5_re, state_s5_im, c, c_ctx, w_ada, b_ada, norm_w, ffn_w_in, ffn_w_out, w_in, diff_lambda, diff_subln_w, mla_q_norm_w, mla_w_q_up, mla_kv_norm_w, mla_w_kv_up, s5_a_re, s5_a_im, s5_log_step, s5_b_re, s5_b_im, s5_c_re, s5_c_im, s5_d, s5_w_glu, s5_b_glu, w_out, final_norm_w):
    n_ctx, seq, _ = x_prompt.shape
    n_dec, dec_seq, _ = x_sample.shape
    past = cache_diff_k.shape[2]

    cvec = jnp.concatenate(
        [c_ctx[None, :], c, jnp.zeros((MOD_ROWS - 1 - n_dec, D_MODEL), F32)], axis=0)
    mod = _adaln(cvec, w_ada, b_ada)
    rope_tabs = _rope_tables(dec_seq)

    groups = {
        'ctx': dict(x=x_prompt.reshape(n_ctx * seq, D_MODEL), seg=n_ctx * seq, row=0, nb=n_ctx),
        'dec': dict(x=x_sample.reshape(n_dec * dec_seq, D_MODEL), seg=dec_seq, row=1, nb=n_dec),
    }
    ffn_w = (ffn_w_in, ffn_w_out)
    norm_w4 = norm_w.reshape(DEPTH, 3, 1, D_MODEL)
    w_proj = _prep_inproj(w_in)
    mla_w = _prep_mla(mla_q_norm_w, mla_w_q_up, mla_kv_norm_w, mla_w_kv_up)
    mix_w = (s5_w_glu.astype(BF16), s5_b_glu.reshape(DEPTH, 1, S5_WIDTH), w_out.astype(BF16))
    s5_ops = jax.vmap(_s5_operators)(s5_a_re, s5_a_im, s5_log_step, s5_b_re, s5_b_im, s5_c_re, s5_c_im, s5_d)
    subw = jnp.tile(diff_subln_w, (1, 2)).reshape(DEPTH, 1, LANES)
    ctx_k, ctx_vt = _cached_slabs(cache_diff_k, False), _cached_slabs(cache_diff_v, True)
    ctx_kpe = jnp.pad(cache_mla_kpe, ((0, 0), (0, 0), (0, 0), (0, LANES - MLA_ROPE)))
    pack_dirs = lambda s: jnp.concatenate([s[:, :, 0], s[:, :, 1]], axis=-1)
    h0_dec = jnp.stack([pack_dirs(state_s5_re), pack_dirs(state_s5_im)], 0).transpose(2, 3, 0, 1, 4)
    h0_ctx = jnp.zeros((1, S5_GROUPS, 2, n_ctx, LANES), F32)
    caches = None
    s5_states = []
    for l in range(DEPTH):
        lam_init = 0.8 - 0.6 * math.exp(-0.3 * l)
        final_w = final_norm_w if l == DEPTH - 1 else None
        for name, g in groups.items():
            dec = name == 'dec'
            x = _ffn_call(g['x'], mod, g['seg'], g['row'], norm_w4, l, 0, ffn_w)
            proj = _inproj_call(x, mod, g['seg'], g['row'], norm_w4, l, w_proj, mla_w,
                                rope_tabs if dec else None, cache_out=None if dec else (seq, caches))
            q, k, vt, qm, km, vmt, xg = proj[:7]
            parts = [(k, vt, km, vmt, g['x'].shape[0] // g['nb'])]
            if dec:
                ckm, cvmt = _mla_expand_call(cache_mla_ckv, ctx_kpe, mla_w[3], mla_w[4], l)
                parts.append((ctx_k, ctx_vt, ckm, cvmt, past))
            o_a, o_b = _attn_call(q, qm, diff_lambda, subw, parts, g['nb'], l, lam_init)
            y_s5, fin = _s5_call(xg, *s5_ops, h0_dec if dec else h0_ctx, g['nb'], l, l if dec else 0)
            g['x'] = _ffn_call(x, mod, g['seg'], g['row'], norm_w4, l, 1, ffn_w,
                               mix_args=(o_a, o_b, y_s5), mix_w=mix_w, final_w=final_w)
            if not dec:
                caches = proj[7:]
                fin = fin.reshape(S5_GROUPS, 2, n_ctx, 2, S5_STATE).transpose(1, 2, 3, 0, 4)
                s5_states.append((fin[0], fin[1]))

    y_prompt = groups['ctx']['x'].reshape(n_ctx, seq, D_MODEL)
    y_sample = groups['dec']['x'].reshape(n_dec, dec_seq, D_MODEL)
    new_k, new_v, new_ckv, new_kpe = caches
    new_s5_re = jnp.stack([t[0] for t in s5_states], axis=1)
    new_s5_im = jnp.stack([t[1] for t in s5_states], axis=1)
    return (y_prompt, y_sample,
            new_k.reshape(n_ctx, DEPTH, seq, DA_HEADS, 2 * DA_QK),
            new_v.reshape(n_ctx, DEPTH, seq, DA_HEADS, DA_V),
            new_ckv, new_kpe, new_s5_re, new_s5_im)
```

```python
import functools
import math

import jax
import jax.numpy as jnp
import numpy as np
from jax import lax
from jax.experimental import pallas as pl
from jax.experimental.pallas import tpu as pltpu

F32 = jnp.float32
BF16 = jnp.bfloat16

D_MODEL = 1024
DEPTH = 2
GRID_W = 64
ROPE_THETA = 10000.0
EPS = 1e-6
N_MOD = 9
D_FF = 2816
DA_HEADS = 6
DA_QK = 32
DA_V = 64
DA_WIDTH = DA_HEADS * DA_V
MLA_HEADS = 6
MLA_Q_RANK = 256
MLA_KV_RANK = 128
MLA_NOPE = 64
MLA_ROPE = 32
MLA_V = 64
MLA_WIDTH = MLA_HEADS * MLA_V
S5_WIDTH = D_MODEL - DA_WIDTH - MLA_WIDTH
S5_CH = 16
S5_GROUPS = S5_WIDTH // S5_CH
S5_STATE = 64

LANES = 128
FF_CHUNK = 256
N_FF_CHUNKS = D_FF // FF_CHUNK
TOKEN_TILE = 512
Q_TILE = 512
S5_CHUNK = 16
N_SLABS = DA_WIDTH // LANES
MOD_ROWS = 8
VMEM_LIMIT = 56 * 1024 * 1024


def _rms(x, w):
    return x * lax.rsqrt(jnp.mean(x * x, axis=-1, keepdims=True) + EPS) * w


def _silu(x):
    return x * jax.nn.sigmoid(x)


def _gelu_tanh(x):
    c = math.sqrt(2.0 / math.pi)
    return x * (0.5 * (1.0 + jnp.tanh(c * (x + 0.044715 * (x * x * x)))))


def _dot(a, b):
    return jnp.dot(a, b, preferred_element_type=F32)


def _resident(shape):
    nd = len(shape)
    return pl.BlockSpec(shape, lambda *_: (0,) * nd, pipeline_mode=pl.Buffered(1))


def _resident_at(shape, *lead):
    nd = len(shape)
    return pl.BlockSpec((None,) * len(lead) + tuple(shape), lambda *_: tuple(lead) + (0,) * nd,
                        pipeline_mode=pl.Buffered(1))


def _adaln_kernel(c_ref, w_ref, b_ref, o_ref):
    s = _silu(c_ref[...])
    o_ref[...] = jnp.dot(s, w_ref[...], preferred_element_type=F32,
                         precision=lax.Precision.HIGHEST) + b_ref[...]


def _adaln(cvec, w_ada, b_ada):
    n = N_MOD * D_MODEL
    tn = 1152
    out = pl.pallas_call(
        _adaln_kernel,
        grid=(DEPTH, n // tn),
        in_specs=[
            pl.BlockSpec((MOD_ROWS, D_MODEL), lambda l, j: (0, 0)),
            pl.BlockSpec((None, D_MODEL, tn), lambda l, j: (l, 0, j)),
            pl.BlockSpec((None, 1, tn), lambda l, j: (l, 0, j)),
        ],
        out_specs=pl.BlockSpec((None, MOD_ROWS, tn), lambda l, j: (l, 0, j)),
        out_shape=jax.ShapeDtypeStruct((DEPTH, MOD_ROWS, n), F32),
        compiler_params=pltpu.CompilerParams(
            dimension_semantics=("arbitrary", "arbitrary"), vmem_limit_bytes=VMEM_LIMIT),
        name="adaln",
    )(cvec, w_ada, b_ada.reshape(DEPTH, 1, n))
    return out.reshape(DEPTH, MOD_ROWS, N_MOD, D_MODEL)


_ROWS_PER_TILE = TOKEN_TILE // S5_CHUNK
_SLOTS = LANES // S5_CH


def _to_group_major(u_scr, xg_ref, row0, n_rows):
    c0, n_chunks = row0 // S5_CHUNK, n_rows // S5_CHUNK
    slot = lax.broadcasted_iota(jnp.int32, (n_chunks, LANES), 1) // S5_CH
    for g in range(S5_GROUPS):
        for tt in range(S5_CHUNK // _SLOTS):
            acc = None
            for j in range(_SLOTS):
                t = tt * _SLOTS + j
                src = u_scr[g // _SLOTS, pl.ds(row0 + t, n_chunks, stride=S5_CHUNK), :]
                shift = ((j - g % _SLOTS) % _SLOTS) * S5_CH
                piece = pltpu.roll(src, shift, 1) if shift else src
                acc = piece if acc is None else jnp.where(slot == j, piece, acc)
            xg_ref[g, c0:c0 + n_chunks, tt * LANES:(tt + 1) * LANES] = acc.astype(BF16)
        yield


def _from_group_major(yg_ref, y_scr, row0, n_rows):
    c0, n_chunks = row0 // S5_CHUNK, n_rows // S5_CHUNK
    slot = lax.broadcasted_iota(jnp.int32, (n_chunks, LANES), 1) // S5_CH
    for t in range(S5_CHUNK):
        for gc in range(S5_GROUPS // _SLOTS):
            acc = None
            for m in range(_SLOTS):
                src = yg_ref[gc * _SLOTS + m, c0:c0 + n_chunks, (t // _SLOTS) * LANES:(t // _SLOTS + 1) * LANES]
                shift = ((m - t % _SLOTS) % _SLOTS) * S5_CH
                piece = pltpu.roll(src, shift, 1) if shift else src
                acc = piece if acc is None else jnp.where(slot == m, piece, acc)
            y_scr[gc, pl.ds(row0 + t, n_chunks, stride=S5_CHUNK), :] = acc
        yield


def _interleave(stages):
    stages = list(stages)
    while stages:
        for g in list(stages):
            try:
                next(g)
            except StopIteration:
                stages.remove(g)


def _delayed(gen, steps):
    for _ in range(steps):
        yield
    yield from gen


_MIX_PARTS = 2
_MIX_STAGGER = 8


def _ffn_kernel(mix, final, mod_base, *refs):
    it = iter(refs)
    x_ref, mod_ref, nw_ref, wi_ref, wo_ref = (next(it) for _ in range(5))
    if mix:
        oa_ref, ob_ref, ys_ref, wglu_ref, bglu_ref, wout_ref = (next(it) for _ in range(6))
    if final:
        fnw_ref = next(it)
    out_ref, h_scr, acc_scr = next(it), next(it), next(it)

    if mix:
        y_scr = next(it)
    shift = mod_ref[0, mod_base:mod_base + 1, :]
    scale = mod_ref[0, mod_base + 1:mod_base + 2, :]
    gate = mod_ref[0, mod_base + 2:mod_base + 3, :]
    tm = x_ref.shape[0]
    n_part = _MIX_PARTS if mix else 1
    rows_part = tm // n_part

    def chunk_cols(j):
        col = j * FF_CHUNK
        return slice(col, col + FF_CHUNK), slice(D_FF + col, D_FF + col + FF_CHUNK)

    def prologue(p):
        r0 = p * rows_part
        rows = slice(r0, r0 + rows_part)
        x = x_ref[rows, :]
        if mix:
            yield from _from_group_major(ys_ref, y_scr, r0, rows_part)
            yc = _gelu_tanh(jnp.concatenate([y_scr[gc, rows, :] for gc in range(S5_WIDTH // LANES)], axis=1))
            oc = yc * jax.nn.sigmoid(_dot(yc.astype(BF16), wglu_ref[...]) + bglu_ref[...])
            yield
            mixed = jnp.concatenate([oa_ref[s, rows, :] for s in range(N_SLABS)]
                                    + [ob_ref[s, rows, :] for s in range(N_SLABS)]
                                    + [oc.astype(BF16)], axis=1)
            x = x + mod_ref[0, mod_base - 1:mod_base, :] * _dot(mixed, wout_ref[...])
            out_ref[rows, :] = x
            yield
        h = _rms(x, nw_ref[...]) * (1.0 + scale) + shift
        h_scr[rows, :] = h.astype(BF16)
        yield

    _interleave(_delayed(prologue(p), p * _MIX_STAGGER) for p in range(n_part))
    acc_scr[...] = jnp.zeros_like(acc_scr)

    for j in range(N_FF_CHUNKS):
        hb = h_scr[...]
        ca, cg = chunk_cols(j)
        a = _dot(hb, wi_ref[:, ca])
        g = _dot(hb, wi_ref[:, cg])
        acc_scr[...] += _dot((_silu(g) * a).astype(BF16), wo_ref[ca, :])
    y = (out_ref[...] if mix else x_ref[...]) + 0.5 * gate * acc_scr[...]
    if final:
        y = _rms(y, fnw_ref[...])
    out_ref[...] = y


def _mod_spec(layer, seg_len, row_base):
    return pl.BlockSpec((None, 1, N_MOD, D_MODEL),
                        lambda i: (layer, row_base + (i * TOKEN_TILE) // seg_len, 0, 0))


def _ffn_call(x, mod, seg_len, row_base, norm_w4, layer, which, ffn_w, mix_args=None, mix_w=None, final_w=None):
    t = x.shape[0]
    tm = TOKEN_TILE
    mix = mix_args is not None
    final = final_w is not None
    w_in_all, w_out_all = ffn_w
    tok = lambda i: (i, 0)
    in_specs = [
        pl.BlockSpec((tm, D_MODEL), tok),
        _mod_spec(layer, seg_len, row_base),
        _resident_at((1, D_MODEL), layer, 2 * which),
        _resident_at((D_MODEL, 2 * D_FF), layer, which),
        _resident_at((D_FF, D_MODEL), layer, which),
    ]
    args = [x, mod, norm_w4, w_in_all, w_out_all]
    if mix:
        in_specs += [
            pl.BlockSpec((N_SLABS, tm, LANES), lambda i: (0, i, 0)),
            pl.BlockSpec((N_SLABS, tm, LANES), lambda i: (0, i, 0)),
            pl.BlockSpec((S5_GROUPS, _ROWS_PER_TILE, S5_CHUNK * S5_CH), lambda i: (0, i, 0)),
            _resident_at((S5_WIDTH, S5_WIDTH), layer),
            _resident_at((1, S5_WIDTH), layer),
            _resident_at((D_MODEL, D_MODEL), layer),
        ]
        args += [*mix_args, *mix_w]
    if final:
        in_specs.append(_resident((1, D_MODEL)))
        args.append(final_w.reshape(1, D_MODEL))
    return pl.pallas_call(
        functools.partial(_ffn_kernel, mix, final, 6 * which),
        grid=(t // tm,),
        in_specs=in_specs,
        out_specs=pl.BlockSpec((tm, D_MODEL), tok),
        out_shape=jax.ShapeDtypeStruct((t, D_MODEL), F32),
        scratch_shapes=[pltpu.VMEM((tm, D_MODEL), BF16), pltpu.VMEM((tm, D_MODEL), F32)]
        + ([pltpu.VMEM((S5_WIDTH // LANES, tm, LANES), F32)] if mix else []),
        compiler_params=pltpu.CompilerParams(
            dimension_semantics=("arbitrary",), vmem_limit_bytes=VMEM_LIMIT),
        name="mix_ffn" if mix else "ffn",
    )(*args)


_C_Q, _C_K, _C_V = 0, DA_WIDTH, 2 * DA_WIDTH
_C_CQ = 3 * DA_WIDTH
_C_CKV = _C_CQ + MLA_Q_RANK
_C_U = _C_CKV + MLA_KV_RANK
_C_KPE = _C_U + S5_WIDTH
_NCOL = _C_KPE + LANES

_QSCALE_DIFF = DA_QK ** -0.5 * math.log2(math.e)
_QSCALE_MLA = (MLA_NOPE + MLA_ROPE) ** -0.5 * math.log2(math.e)


_INPROJ_PARTS = 2
_INPROJ_STAGGER = 6


def _rotate_half(x):
    half = DA_QK // 4
    lane = lax.broadcasted_iota(jnp.int32, x.shape, 1)
    return jnp.where(lane % (2 * half) < half, pltpu.roll(x, LANES - half, 1), pltpu.roll(x, half, 1))


def _inproj_kernel(rope, n_aliased, *refs):
    it = iter(refs)
    x_ref, mod_ref, nw_ref, w_ref, qnw_ref, kvnw_ref, wqup_ref, wk_ref, wv_ref = (next(it) for _ in range(9))
    if rope:
        cos_ref, sin_ref, cosm_ref, sinm_ref = (next(it) for _ in range(4))
    for _ in range(n_aliased):
        next(it)
    q_ref, k_ref, vt_ref, qm_ref, km_ref, vmt_ref, xg_ref = (next(it) for _ in range(7))
    if not rope:
        ka_ref, va_ref, ckv_ref, kpe_ref = (next(it) for _ in range(4))
    u_scr = next(it)

    tm = x_ref.shape[0]
    n_part = _INPROJ_PARTS
    rows_part = tm // n_part

    def part(p):
        r0 = p * rows_part
        rows = slice(r0, r0 + rows_part)
        h = _rms(x_ref[rows, :], nw_ref[...]) * (1.0 + mod_ref[0, 4:5, :]) + mod_ref[0, 3:4, :]
        h = h.astype(BF16)
        yield
        qkv = _dot(h, w_ref[:, :_C_CQ])
        yield
        rest = _dot(h, w_ref[:, _C_CQ:])
        yield

        def cols(start, width):
            src, off = (qkv, 0) if start < _C_CQ else (rest, _C_CQ)
            return src[:, start - off:start - off + width]

        if rope:
            cos, sin, cosm, sinm = (r[rows, :] for r in (cos_ref, sin_ref, cosm_ref, sinm_ref))
        for s in range(N_SLABS):
            qs = cols(_C_Q + s * LANES, LANES)
            ks = cols(_C_K + s * LANES, LANES)
            if rope:
                qs = qs * cos + _rotate_half(qs) * sin
                ks = ks * cos + _rotate_half(ks) * sin
            q_ref[s, rows, :] = (qs * _QSCALE_DIFF).astype(BF16)
            k_ref[s, rows, :] = ks.astype(BF16)
            vt_ref[s, :, rows] = cols(_C_V + s * LANES, LANES).T.astype(BF16)
            yield
        for gc in range(S5_WIDTH // LANES):
            u_scr[gc, rows, :] = cols(_C_U + gc * LANES, LANES)
        yield from _to_group_major(u_scr, xg_ref, r0, rows_part)

        cqn = _rms(cols(_C_CQ, MLA_Q_RANK), qnw_ref[...]).astype(BF16)
        qm = _dot(cqn, wqup_ref[...])
        ckvn = _rms(cols(_C_CKV, MLA_KV_RANK), kvnw_ref[...])
        ckvb = ckvn.astype(BF16)
        kn = _dot(ckvb, wk_ref[...])
        vm = _dot(ckvb, wv_ref[...])
        yield
        kpe = cols(_C_KPE, LANES)
        if rope:
            kpe = kpe * cosm + _rotate_half(kpe) * sinm
        for hd in range(MLA_HEADS):
            qh = qm[:, hd * LANES:(hd + 1) * LANES]
            if rope:
                qh = qh * cosm + _rotate_half(qh) * sinm
            qm_ref[hd, rows, :] = (qh * _QSCALE_MLA).astype(BF16)
            km_ref[hd, rows, :] = (kn[:, hd * LANES:(hd + 1) * LANES] + kpe).astype(BF16)
            yield
        for s in range(N_SLABS):
            vmt_ref[s, :, rows] = vm[:, s * LANES:(s + 1) * LANES].T.astype(BF16)

        if not rope:
            seq = ka_ref.shape[-2]
            new = (cols(_C_K, DA_WIDTH), cols(_C_V, DA_WIDTH), ckvn, cols(_C_KPE, MLA_ROPE))
            for b in range(r0 // seq, (r0 + rows_part) // seq):
                mine = slice(b * seq - r0, (b + 1) * seq - r0)
                for ref, val in zip((ka_ref, va_ref, ckv_ref, kpe_ref), new):
                    if len(ref.shape) == 3:
                        ref[b] = val[mine]
                    else:
                        ref[b, 0] = val[mine]
                        for later in range(1, ref.shape[1]):
                            ref[b, later] = jnp.zeros_like(val[mine])

    _interleave(_delayed(part(p), p * _INPROJ_STAGGER) for p in range(n_part))


def _inproj_call(x, mod, seg_len, row_base, norm_w4, layer, w, mla_w, rope_tabs, cache_out=None):
    t = x.shape[0]
    tm = TOKEN_TILE
    rope = rope_tabs is not None
    qnw, kvnw, wqup, wk, wv = mla_w
    tok = lambda i: (i, 0)
    slab = lambda i: (0, i, 0)
    slab_t = lambda i: (0, 0, i)
    in_specs = [
        pl.BlockSpec((tm, D_MODEL), tok),
        _mod_spec(layer, seg_len, row_base),
        _resident_at((1, D_MODEL), layer, 1),
        _resident_at((D_MODEL, _NCOL), layer),
        _resident_at((1, MLA_Q_RANK), layer),
        _resident_at((1, MLA_KV_RANK), layer),
        _resident_at((MLA_Q_RANK, MLA_HEADS * LANES), layer),
        _resident_at((MLA_KV_RANK, MLA_HEADS * LANES), layer),
        _resident_at((MLA_KV_RANK, MLA_WIDTH), layer),
    ]
    args = [x, mod, norm_w4, w, qnw, kvnw, wqup, wk, wv]
    if rope:
        n_pos = rope_tabs[0].shape[0]
        pos = lambda i: (i % (n_pos // tm), 0)
        in_specs += [pl.BlockSpec((tm, LANES), pos)] * 4
        args += list(rope_tabs)
    out_shape = [
        jax.ShapeDtypeStruct((N_SLABS, t, LANES), BF16),
        jax.ShapeDtypeStruct((N_SLABS, t, LANES), BF16),
        jax.ShapeDtypeStruct((N_SLABS, LANES, t), BF16),
        jax.ShapeDtypeStruct((MLA_HEADS, t, LANES), BF16),
        jax.ShapeDtypeStruct((MLA_HEADS, t, LANES), BF16),
        jax.ShapeDtypeStruct((N_SLABS, LANES, t), BF16),
        jax.ShapeDtypeStruct((S5_GROUPS, t // S5_CHUNK, S5_CHUNK * S5_CH), BF16),
    ]
    out_specs = [
        pl.BlockSpec((N_SLABS, tm, LANES), slab),
        pl.BlockSpec((N_SLABS, tm, LANES), slab),
        pl.BlockSpec((N_SLABS, LANES, tm), slab_t),
        pl.BlockSpec((MLA_HEADS, tm, LANES), slab),
        pl.BlockSpec((MLA_HEADS, tm, LANES), slab),
        pl.BlockSpec((N_SLABS, LANES, tm), slab_t),
        pl.BlockSpec((S5_GROUPS, _ROWS_PER_TILE, S5_CHUNK * S5_CH), slab),
    ]
    aliases = {}
    if not rope:
        seq, stacked = cache_out
        n_first = len(out_shape)
        for k, width in enumerate((DA_WIDTH, DA_WIDTH, MLA_KV_RANK, MLA_ROPE)):
            out_shape.append(jax.ShapeDtypeStruct((t // seq, DEPTH, seq, width), F32))
            if stacked is None:
                out_specs.append(pl.BlockSpec((tm // seq, DEPTH, seq, width), lambda i: (i, 0, 0, 0)))
            else:
                out_specs.append(pl.BlockSpec((tm // seq, None, seq, width), lambda i: (i, layer, 0, 0)))
                aliases[len(args)] = n_first + k
                in_specs.append(pl.BlockSpec(memory_space=pl.ANY))
                args.append(stacked[k])
    return pl.pallas_call(
        functools.partial(_inproj_kernel, rope, len(aliases)),
        grid=(t // tm,),
        in_specs=in_specs,
        out_specs=out_specs,
        out_shape=out_shape,
        input_output_aliases=aliases,
        scratch_shapes=[pltpu.VMEM((S5_WIDTH // LANES, tm, LANES), F32)],
        compiler_params=pltpu.CompilerParams(
            dimension_semantics=("arbitrary",), vmem_limit_bytes=VMEM_LIMIT),
        name="inproj_dec" if rope else "inproj_ctx",
    )(*args)


def _mla_expand_kernel(ckv_ref, kpe_ref, wk_ref, wv_ref, km_ref, vmt_ref):
    ckvb = ckv_ref[...].astype(BF16)
    kn = _dot(ckvb, wk_ref[...])
    vm = _dot(ckvb, wv_ref[...])
    kpe = kpe_ref[...]
    for hd in range(MLA_HEADS):
        km_ref[hd] = (kn[:, hd * LANES:(hd + 1) * LANES] + kpe).astype(BF16)
    for s in range(N_SLABS):
        vmt_ref[s] = vm[:, s * LANES:(s + 1) * LANES].T.astype(BF16)


def _mla_expand_call(ckv, kpe128, wk, wv, layer):
    n_batch, _, rows, _ = ckv.shape
    t = n_batch * rows
    return pl.pallas_call(
        _mla_expand_kernel,
        grid=(n_batch,),
        in_specs=[
            pl.BlockSpec((None, None, rows, MLA_KV_RANK), lambda i: (i, layer, 0, 0)),
            pl.BlockSpec((None, None, rows, LANES), lambda i: (i, layer, 0, 0)),
            _resident_at((MLA_KV_RANK, MLA_HEADS * LANES), layer),
            _resident_at((MLA_KV_RANK, MLA_WIDTH), layer),
        ],
        out_specs=[
            pl.BlockSpec((MLA_HEADS, rows, LANES), lambda i: (0, i, 0)),
            pl.BlockSpec((N_SLABS, LANES, rows), lambda i: (0, 0, i)),
        ],
        out_shape=[
            jax.ShapeDtypeStruct((MLA_HEADS, t, LANES), BF16),
            jax.ShapeDtypeStruct((N_SLABS, LANES, t), BF16),
        ],
        compiler_params=pltpu.CompilerParams(dimension_semantics=("arbitrary",)),
        name="mla_expand",
    )(ckv, kpe128, wk, wv)


KEY_TILE = 256
SHORT_SEQS_PER_STEP = 2
N_SCORE_BUFS = 4


def _attn_kernel(n_parts, lam_init, n_seq, *refs):
    q_ref, qm_ref, lamp_ref, subw_ref = refs[:4]
    parts = [refs[4 + 4 * i:8 + 4 * i] for i in range(n_parts)]
    oa_ref, ob_ref, sbuf, qt_scr = refs[4 + 4 * n_parts:]

    lp = lamp_ref[...]
    lam = (jnp.exp(jnp.sum(lp[0:1] * lp[1:2], keepdims=True))
           - jnp.exp(jnp.sum(lp[2:3] * lp[3:4], keepdims=True)) + lam_init)
    tq = q_ref.shape[1] // n_seq
    low_half = lax.broadcasted_iota(jnp.int32, (tq, LANES), 1) < DA_V
    channel = lax.broadcasted_iota(jnp.int32, (LANES, tq), 0)
    subw = subw_ref[...] * (1.0 - lam_init)
    sublanes = 8
    n_qt = N_SLABS + MLA_HEADS

    tiles, row = [], 0
    for p, (k, _, _, _) in enumerate(parts):
        per_seq = k.shape[1] // n_seq
        for k0 in range(0, per_seq, KEY_TILE):
            n = min(KEY_TILE, per_seq - k0)
            tiles.append((p, k0, n, slice(row, row + n)))
            row += n

    def key_slice(unit, p, k0, n):
        first = unit['seq'] * (parts[p][0].shape[1] // n_seq) + k0
        return slice(first, first + n)

    def row_fold(op, acc, x):
        for j in range(x.shape[0] // sublanes):
            piece = x[j * sublanes:(j + 1) * sublanes, :]
            acc = piece if acc is None else op(acc, piece)
        return acc

    for j in range(n_seq):
        mine = slice(j * tq, (j + 1) * tq)
        for s in range(N_SLABS):
            qt_scr[j * n_qt + s] = q_ref[s, mine, :].astype(F32).T.astype(BF16)
        for hd in range(MLA_HEADS):
            qt_scr[j * n_qt + N_SLABS + hd] = qm_ref[hd, mine, :].astype(F32).T.astype(BF16)

    def scores(unit):
        buf = unit['buf']
        if unit['mla']:
            rhs, kref, kidx = qt_scr[unit['seq'] * n_qt + N_SLABS + unit['head']], 2, unit['head']
        else:
            qt = qt_scr[unit['seq'] * n_qt + unit['slab']]
            rhs, kref, kidx = jnp.where(channel // DA_QK == unit['sub'], qt, jnp.zeros_like(qt)), 0, unit['slab']
        mx = None
        for p, k0, n, rows in tiles:
            s = _dot(parts[p][kref][kidx, key_slice(unit, p, k0, n), :], rhs)
            sbuf[buf, rows, :] = s
            mx = row_fold(jnp.maximum, mx, s)
            yield
        unit['max'] = jnp.max(mx, axis=0, keepdims=True)

    def expo(unit):
        buf, tot = unit['buf'], None
        for _, _, _, rows in tiles:
            e = jnp.exp2(sbuf[buf, rows, :] - unit['max'])
            sbuf[buf, rows, :] = e
            tot = row_fold(jnp.add, tot, e)
            yield
        unit['sum'] = jnp.sum(tot, axis=0, keepdims=True)

    def pv(head):
        units, slab = head['units'], head['slab']
        vref = 3 if units[0]['mla'] else 1
        l0 = units[0]['sum']
        if len(units) == 2:
            r = lam * l0 / units[1]['sum']
        acc = None
        for p, k0, n, rows in tiles:
            w = sbuf[units[0]['buf'], rows, :]
            if len(units) == 2:
                w = w - r * sbuf[units[1]['buf'], rows, :]
            o = _dot(parts[p][vref][slab, :, key_slice(units[0], p, k0, n)], w.astype(BF16))
            acc = o if acc is None else acc + o
            yield
        head['out'] = (acc * (1.0 / l0)).T

    def finish(lo, hi):
        slab, mine = hi['slab'], slice(hi['seq'] * tq, (hi['seq'] + 1) * tq)
        o = jnp.where(low_half, lo['out'], hi['out'])
        if hi['mla']:
            ob_ref[slab, mine, :] = o.astype(BF16)
            return
        o2 = o * o
        ss_lo = jnp.sum(jnp.where(low_half, o2, 0.0), axis=1, keepdims=True)
        ss_hi = jnp.sum(jnp.where(low_half, 0.0, o2), axis=1, keepdims=True)
        ms = jnp.where(low_half, ss_lo, ss_hi) * (1.0 / DA_V)
        oa_ref[slab, mine, :] = (o * lax.rsqrt(ms + EPS) * subw).astype(BF16)

    def stream(seq):
        units, heads = [], []
        for slab in range(N_SLABS):
            for mla in (False, True):
                for half in range(2):
                    mine = [dict(seq=seq, mla=mla, slab=slab, head=2 * slab + half, sub=2 * half + c)
                            for c in range(1 if mla else 2)]
                    heads.append(dict(seq=seq, units=mine, slab=slab, mla=mla, half=half,
                                      last=len(units) + len(mine) - 1))
                    units += mine
        for k, unit in enumerate(units):
            unit['buf'] = seq * N_SCORE_BUFS + k % N_SCORE_BUFS
        return units, heads

    streams = [stream(j) for j in range(n_seq)]
    n_units = len(streams[0][0])
    pending = {}
    for k in range(n_units + 2):
        stages, ready = [], []
        for units, heads in streams:
            if k < n_units:
                stages.append(scores(units[k]))
            if 1 <= k <= n_units:
                stages.append(expo(units[k - 1]))
            for h in heads:
                if h['last'] == k - 2:
                    stages.append(pv(h))
                    ready.append(h)
        _interleave(stages)
        for h in ready:
            key = (h['seq'], h['slab'], h['mla'])
            if h['half'] == 1:
                finish(pending.pop(key), h)
            else:
                pending[key] = h


def _attn_call(q, qm, lamp, subw, parts, n_batch, layer, lam_init):
    t = q.shape[1]
    seq = t // n_batch
    tq = min(Q_TILE, seq)
    nq = seq // tq
    n_seq = SHORT_SEQS_PER_STEP if nq == 1 and n_batch % SHORT_SEQS_PER_STEP == 0 else 1
    qmap = lambda b, i: (0, b * nq + i, 0)
    in_specs = [
        pl.BlockSpec((N_SLABS, n_seq * tq, LANES), qmap),
        pl.BlockSpec((MLA_HEADS, n_seq * tq, LANES), qmap),
        pl.BlockSpec((None, 4, DA_QK), lambda b, i: (layer, 0, 0)),
        pl.BlockSpec((None, 1, LANES), lambda b, i: (layer, 0, 0)),
    ]
    args = [q, qm, lamp, subw]

    def keys_spec(arr, n_keys, transposed):
        n_keys = n_seq * n_keys
        block = (arr.shape[-3], LANES, n_keys) if transposed else (arr.shape[-3], n_keys, LANES)
        at = (lambda b: (0, 0, b)) if transposed else (lambda b: (0, b, 0))
        if arr.ndim == 4:
            return pl.BlockSpec((None,) + block, lambda b, i: (layer,) + at(b))
        return pl.BlockSpec(block, lambda b, i: at(b))

    for k, vt, km, vmt, n_keys in parts:
        in_specs += [keys_spec(k, n_keys, False), keys_spec(vt, n_keys, True),
                     keys_spec(km, n_keys, False), keys_spec(vmt, n_keys, True)]
        args += [k, vt, km, vmt]
    return pl.pallas_call(
        functools.partial(_attn_kernel, len(parts), lam_init, n_seq),
        grid=(n_batch // n_seq, nq),
        in_specs=in_specs,
        out_specs=[pl.BlockSpec((N_SLABS, n_seq * tq, LANES), qmap)] * 2,
        out_shape=[jax.ShapeDtypeStruct((N_SLABS, t, LANES), BF16)] * 2,
        scratch_shapes=[pltpu.VMEM((n_seq * N_SCORE_BUFS, sum(p[4] for p in parts), tq), F32),
                        pltpu.VMEM((n_seq * (N_SLABS + MLA_HEADS), LANES, tq), BF16)],
        compiler_params=pltpu.CompilerParams(
            dimension_semantics=("arbitrary", "arbitrary"), vmem_limit_bytes=VMEM_LIMIT),
        name="attn_dec" if len(parts) > 1 else "attn_ctx",
    )(*args)


def _s5_kernel(n_batch, n_chunks, x_ref, kmat_ref, bst_ref, cst_ref, lamt_ref, h0_ref, y_ref, fin_ref,
               s_scr, hin_scr):
    x = x_ref[...]
    y = _dot(x, kmat_ref[...])
    for k in range(2):
        s_scr[k] = _dot(x, bst_ref[k])

    a_re, a_im = lamt_ref[0], lamt_ref[1]
    fwd_lanes = lax.broadcasted_iota(jnp.int32, (n_batch, LANES), 1) < S5_STATE

    def chunk_rows(j):
        return pl.ds(j, n_batch, stride=n_chunks)

    def step(j, carry):
        h_re, h_im = carry
        rf, rb = chunk_rows(j), chunk_rows(n_chunks - 1 - j)
        hin_scr[0, rf, :] = h_re
        hin_scr[1, rf, :] = h_im
        hin_scr[2, rb, :] = h_re
        hin_scr[3, rb, :] = h_im
        s_re = jnp.where(fwd_lanes, s_scr[0, rf, :], s_scr[0, rb, :])
        s_im = jnp.where(fwd_lanes, s_scr[1, rf, :], s_scr[1, rb, :])
        return a_re * h_re - a_im * h_im + s_re, a_re * h_im + a_im * h_re + s_im

    fin = lax.fori_loop(0, n_chunks, step, (h0_ref[0], h0_ref[1]))
    fwd_rows = lax.broadcasted_iota(jnp.int32, hin_scr.shape[1:], 1) < S5_STATE
    for k in range(2):
        fin_ref[k] = fin[k]
        hin = jnp.where(fwd_rows, hin_scr[k], hin_scr[2 + k])
        y += _dot(hin.astype(BF16), cst_ref[k])
    y_ref[...] = y


def _s5_call(xg, kmat, bst, cst, lamt, h0, n_batch, layer, h0_layer):
    rows = xg.shape[1]
    n_chunks = rows // n_batch
    wide = S5_CHUNK * S5_CH
    grp = lambda g: (g, 0, 0)
    grp4 = lambda g: (g, 0, 0, 0)
    return pl.pallas_call(
        functools.partial(_s5_kernel, n_batch, n_chunks),
        grid=(S5_GROUPS,),
        in_specs=[
            pl.BlockSpec((None, rows, wide), grp),
            pl.BlockSpec((None, None, wide, wide), lambda g: (layer, g, 0, 0)),
            pl.BlockSpec((None, None, 2, wide, LANES), lambda g: (layer, g, 0, 0, 0)),
            pl.BlockSpec((None, None, 2, LANES, wide), lambda g: (layer, g, 0, 0, 0)),
            pl.BlockSpec((None, None, 2, 1, LANES), lambda g: (layer, g, 0, 0, 0)),
            pl.BlockSpec((None, None, 2, n_batch, LANES), lambda g: (h0_layer, g, 0, 0, 0)),
        ],
        out_specs=[
            pl.BlockSpec((None, rows, wide), grp),
            pl.BlockSpec((None, 2, n_batch, LANES), grp4),
        ],
        out_shape=[
            jax.ShapeDtypeStruct((S5_GROUPS, rows, wide), F32),
            jax.ShapeDtypeStruct((S5_GROUPS, 2, n_batch, LANES), F32),
        ],
        scratch_shapes=[pltpu.VMEM((2, rows, LANES), F32), pltpu.VMEM((4, rows, LANES), F32)],
        compiler_params=pltpu.CompilerParams(dimension_semantics=("arbitrary",)),
        name="s5",
    )(xg, kmat, bst, cst, lamt, h0)


def _s5_operators(a_re, a_im, log_step, b_re, b_im, c_re, c_im, d):
    hi = lax.Precision.HIGHEST
    n = S5_CHUNK
    wide = n * S5_CH
    ar, ai = a_re.astype(F32), a_im.astype(F32)
    step = jnp.exp(log_step.astype(F32))[..., None]
    zr, zi = ar * step, ai * step
    k = jnp.arange(n + 1, dtype=F32)[:, None, None, None]
    mag = jnp.exp(k * zr[None])
    pr, pi = mag * jnp.cos(k * zi[None]), mag * jnp.sin(k * zi[None])

    xr, xi = pr[1] - 1.0, pi[1]
    den = ar * ar + ai * ai
    fr, fi = ((xr * ar + xi * ai) / den)[:, :, None, :], ((xi * ar - xr * ai) / den)[:, :, None, :]
    br, bi = (jnp.swapaxes(b.astype(F32), 2, 3) for b in (b_re, b_im))
    bbr, bbi = fr * br - fi * bi, fr * bi + fi * br
    cr, ci = c_re.astype(F32), c_im.astype(F32)

    tt = np.arange(n)

    def state_readout(powers, dirn):
        qr = pr[powers, dirn].transpose(1, 2, 0)[:, :, :, None]
        qi = pi[powers, dirn].transpose(1, 2, 0)[:, :, :, None]
        ctr = jnp.swapaxes(cr[dirn], 1, 2)[:, :, None, :]
        cti = jnp.swapaxes(ci[dirn], 1, 2)[:, :, None, :]
        rr = ctr * qr - cti * qi
        ri = -(ctr * qi + cti * qr)
        return rr.reshape(S5_GROUPS, S5_STATE, wide), ri.reshape(S5_GROUPS, S5_STATE, wide)

    def lag_response(powers, dirn):
        rr, ri = state_readout(powers, dirn)
        return (jnp.einsum('gkp,gpn->gkn', bbr[dirn], rr, precision=hi)
                + jnp.einsum('gkp,gpn->gkn', bbi[dirn], ri, precision=hi))

    fwd, bwd = lag_response(tt, 0), lag_response(n - 1 - tt, 1)
    skip = d.astype(F32)[:, None, :] * jnp.eye(S5_CH, dtype=F32)[None]
    strip = jnp.concatenate([bwd[:, :, :wide - S5_CH], bwd[:, :, wide - S5_CH:] + fwd[:, :, :S5_CH] + skip,
                             fwd[:, :, S5_CH:]], axis=-1)
    kmat = jnp.stack([strip[:, :, (n - 1 - s) * S5_CH:(n - 1 - s) * S5_CH + wide] for s in range(n)], axis=1)
    kmat = kmat.reshape(S5_GROUPS, wide, wide)

    def edge_states(powers, dirn):
        qr = jnp.swapaxes(pr[powers, dirn], 0, 1)[:, :, None, :]
        qi = jnp.swapaxes(pi[powers, dirn], 0, 1)[:, :, None, :]
        er = qr * bbr[dirn][:, None] - qi * bbi[dirn][:, None]
        ei = qr * bbi[dirn][:, None] + qi * bbr[dirn][:, None]
        return er.reshape(S5_GROUPS, wide, S5_STATE), ei.reshape(S5_GROUPS, wide, S5_STATE)

    (bf_r, bf_i), (bb_r, bb_i) = edge_states(n - 1 - tt, 0), edge_states(tt, 1)
    bst = jnp.stack([jnp.concatenate([bf_r, bb_r], axis=2), jnp.concatenate([bf_i, bb_i], axis=2)], axis=1)
    (cf_r, cf_i), (cb_r, cb_i) = state_readout(tt + 1, 0), state_readout(n - tt, 1)
    cst = jnp.stack([jnp.concatenate([cf_r, cb_r], axis=1), jnp.concatenate([cf_i, cb_i], axis=1)], axis=1)
    lamt = jnp.stack([jnp.concatenate([pr[n, 0], pr[n, 1]], axis=1),
                      jnp.concatenate([pi[n, 0], pi[n, 1]], axis=1)], axis=1)
    return kmat.astype(BF16), bst.astype(BF16), cst.astype(BF16), lamt[:, :, None, :]


def _rope_tables(n_pos):
    n = DA_QK // 4
    inv = ROPE_THETA ** (-jnp.arange(n, dtype=F32) / n)
    pos = jnp.arange(n_pos, dtype=jnp.int32)
    ang_r = (pos // GRID_W).astype(F32)[:, None] * inv[None, :]
    ang_c = (pos % GRID_W).astype(F32)[:, None] * inv[None, :]
    cos32 = jnp.concatenate([jnp.cos(ang_r)] * 2 + [jnp.cos(ang_c)] * 2, axis=1)
    sin32 = jnp.concatenate([-jnp.sin(ang_r), jnp.sin(ang_r), -jnp.sin(ang_c), jnp.sin(ang_c)], axis=1)
    pad = LANES - MLA_ROPE
    cosm = jnp.concatenate([cos32, jnp.ones((n_pos, pad), F32)], axis=1)
    sinm = jnp.concatenate([sin32, jnp.zeros((n_pos, pad), F32)], axis=1)
    return jnp.tile(cos32, (1, 4)), jnp.tile(sin32, (1, 4)), cosm, sinm


def _prep_inproj(w):
    o = np.cumsum([0, DA_WIDTH, DA_WIDTH, DA_WIDTH, MLA_Q_RANK, MLA_KV_RANK, MLA_ROPE, S5_WIDTH])
    wq, wk, wv, wcq, wckv, wkpe, wu = (w[..., o[i]:o[i + 1]] for i in range(7))
    wkpe = jnp.pad(wkpe, ((0, 0), (0, 0), (0, LANES - MLA_ROPE)))
    return jnp.concatenate([wq, wk, wv, wcq, wckv, wu, wkpe], axis=-1).astype(BF16)


def _prep_mla(q_norm_w, w_q_up, kv_norm_w, w_kv_up):
    pad = LANES - MLA_ROPE - MLA_NOPE
    wq = w_q_up.reshape(DEPTH, MLA_Q_RANK, MLA_HEADS, MLA_NOPE + MLA_ROPE)
    wqup = jnp.pad(jnp.concatenate([wq[..., MLA_NOPE:], wq[..., :MLA_NOPE]], axis=-1),
                   ((0, 0), (0, 0), (0, 0), (0, pad))).reshape(DEPTH, MLA_Q_RANK, MLA_HEADS * LANES)
    wkv = w_kv_up.reshape(DEPTH, MLA_KV_RANK, MLA_HEADS, MLA_NOPE + MLA_V)
    wk = jnp.pad(wkv[..., :MLA_NOPE], ((0, 0), (0, 0), (0, 0), (MLA_ROPE, pad)))
    wk = wk.reshape(DEPTH, MLA_KV_RANK, MLA_HEADS * LANES)
    wv = wkv[..., MLA_NOPE:].reshape(DEPTH, MLA_KV_RANK, MLA_WIDTH)
    return (q_norm_w.reshape(DEPTH, 1, MLA_Q_RANK), kv_norm_w.reshape(DEPTH, 1, MLA_KV_RANK),
            wqup.astype(BF16), wk.astype(BF16), wv.astype(BF16))


def _cached_slabs(cache, transposed):
    b, _, n = cache.shape[:3]
    a = cache.reshape(b, DEPTH, n, N_SLABS, LANES).astype(BF16)
    a = a.transpose(1, 3, 4, 0, 2) if transposed else a.transpose(1, 3, 0, 2, 4)
    return a.reshape(DEPTH, N_SLABS, LANES, b * n) if transposed else a.reshape(DEPTH, N_SLABS, b * n, LANES)


def kernel(x_prompt, x_sample, cache_diff_k, cache_diff_v, cache_mla_ckv, cache_mla_kpe, state_s5_re, state_s5_im, c, c_ctx, w_ada, b_ada, norm_w, ffn_w_in, ffn_w_out, w_in, diff_lambda, diff_subln_w, mla_q_norm_w, mla_w_q_up, mla_kv_norm_w, mla_w_kv_up, s5_a_re, s5_a_im, s5_log_step, s5_b_re, s5_b_im, s5_c_re, s5_c_im, s5_d, s5_w_glu, s5_b_glu, w_out, final_norm_w):
    n_ctx, seq, _ = x_prompt.shape
    n_dec, dec_seq, _ = x_sample.shape
    past = cache_diff_k.shape[2]

    cvec = jnp.concatenate(
        [c_ctx[None, :], c, jnp.zeros((MOD_ROWS - 1 - n_dec, D_MODEL), F32)], axis=0)
    mod = _adaln(cvec, w_ada, b_ada)
    rope_tabs = _rope_tables(dec_seq)

    groups = {
        'ctx': dict(x=x_prompt.reshape(n_ctx * seq, D_MODEL), seg=n_ctx * seq, row=0, nb=n_ctx),
        'dec': dict(x=x_sample.reshape(n_dec * dec_seq, D_MODEL), seg=dec_seq, row=1, nb=n_dec),
    }
    ffn_w = (ffn_w_in.astype(BF16), ffn_w_out.astype(BF16))
    norm_w4 = norm_w.reshape(DEPTH, 3, 1, D_MODEL)
    w_proj = _prep_inproj(w_in)
    mla_w = _prep_mla(mla_q_norm_w, mla_w_q_up, mla_kv_norm_w, mla_w_kv_up)
    mix_w = (s5_w_glu.astype(BF16), s5_b_glu.reshape(DEPTH, 1, S5_WIDTH), w_out.astype(BF16))
    s5_ops = jax.vmap(_s5_operators)(s5_a_re, s5_a_im, s5_log_step, s5_b_re, s5_b_im, s5_c_re, s5_c_im, s5_d)
    subw = jnp.tile(diff_subln_w, (1, 2)).reshape(DEPTH, 1, LANES)
    ctx_k, ctx_vt = _cached_slabs(cache_diff_k, False), _cached_slabs(cache_diff_v, True)
    ctx_kpe = jnp.pad(cache_mla_kpe, ((0, 0), (0, 0), (0, 0), (0, LANES - MLA_ROPE)))
    pack_dirs = lambda s: jnp.concatenate([s[:, :, 0], s[:, :, 1]], axis=-1)
    h0_dec = jnp.stack([pack_dirs(state_s5_re), pack_dirs(state_s5_im)], 0).transpose(2, 3, 0, 1, 4)
    h0_ctx = jnp.zeros((1, S5_GROUPS, 2, n_ctx, LANES), F32)
    caches = None
    s5_states = []
    for l in range(DEPTH):
        lam_init = 0.8 - 0.6 * math.exp(-0.3 * l)
        final_w = final_norm_w if l == DEPTH - 1 else None
        for name, g in groups.items():
            dec = name == 'dec'
            x = _ffn_call(g['x'], mod, g['seg'], g['row'], norm_w4, l, 0, ffn_w)
            proj = _inproj_call(x, mod, g['seg'], g['row'], norm_w4, l, w_proj, mla_w,
                                rope_tabs if dec else None, cache_out=None if dec else (seq, caches))
            q, k, vt, qm, km, vmt, xg = proj[:7]
            parts = [(k, vt, km, vmt, g['x'].shape[0] // g['nb'])]
            if dec:
                ckm, cvmt = _mla_expand_call(cache_mla_ckv, ctx_kpe, mla_w[3], mla_w[4], l)
                parts.append((ctx_k, ctx_vt, ckm, cvmt, past))
            o_a, o_b = _attn_call(q, qm, diff_lambda, subw, parts, g['nb'], l, lam_init)
            y_s5, fin = _s5_call(xg, *s5_ops, h0_dec if dec else h0_ctx, g['nb'], l, l if dec else 0)
            g['x'] = _ffn_call(x, mod, g['seg'], g['row'], norm_w4, l, 1, ffn_w,
                               mix_args=(o_a, o_b, y_s5), mix_w=mix_w, final_w=final_w)
            if not dec:
                caches = proj[7:]
                fin = fin.reshape(S5_GROUPS, 2, n_ctx, 2, S5_STATE).transpose(1, 2, 3, 0, 4)
                s5_states.append((fin[0], fin[1]))

    y_prompt = groups['ctx']['x'].reshape(n_ctx, seq, D_MODEL)
    y_sample = groups['dec']['x'].reshape(n_dec, dec_seq, D_MODEL)
    new_k, new_v, new_ckv, new_kpe = caches
    new_s5_re = jnp.stack([t[0] for t in s5_states], axis=1)
    new_s5_im = jnp.stack([t[1] for t in s5_states], axis=1)
    return (y_prompt, y_sample,
            new_k.reshape(n_ctx, DEPTH, seq, DA_HEADS, 2 * DA_QK),
            new_v.reshape(n_ctx, DEPTH, seq, DA_HEADS, DA_V),
            new_ckv, new_kpe, new_s5_re, new_s5_im)
```

```python
import functools
import math

import jax
import jax.numpy as jnp
import numpy as np
from jax import lax
from jax.experimental import pallas as pl
from jax.experimental.pallas import tpu as pltpu

F32 = jnp.float32
BF16 = jnp.bfloat16

D_MODEL = 1024
DEPTH = 2
GRID_W = 64
ROPE_THETA = 10000.0
EPS = 1e-6
N_MOD = 9
D_FF = 2816
DA_HEADS = 6
DA_QK = 32
DA_V = 64
DA_WIDTH = DA_HEADS * DA_V
MLA_HEADS = 6
MLA_Q_RANK = 256
MLA_KV_RANK = 128
MLA_NOPE = 64
MLA_ROPE = 32
MLA_V = 64
MLA_WIDTH = MLA_HEADS * MLA_V
S5_WIDTH = D_MODEL - DA_WIDTH - MLA_WIDTH
S5_CH = 16
S5_GROUPS = S5_WIDTH // S5_CH
S5_STATE = 64

LANES = 128
FF_CHUNK = 256
N_FF_CHUNKS = D_FF // FF_CHUNK
TOKEN_TILE = 512
Q_TILE = 512
S5_CHUNK = 16
N_SLABS = DA_WIDTH // LANES
MOD_ROWS = 8
VMEM_LIMIT = 56 * 1024 * 1024


def _rms(x, w):
    return x * lax.rsqrt(jnp.mean(x * x, axis=-1, keepdims=True) + EPS) * w


def _silu(x):
    return x * jax.nn.sigmoid(x)


def _gelu_tanh(x):
    c = math.sqrt(2.0 / math.pi)
    return x * (0.5 * (1.0 + jnp.tanh(c * (x + 0.044715 * (x * x * x)))))


def _dot(a, b):
    return jnp.dot(a, b, preferred_element_type=F32)


def _resident(shape):
    nd = len(shape)
    return pl.BlockSpec(shape, lambda *_: (0,) * nd, pipeline_mode=pl.Buffered(1))


def _resident_at(shape, *lead):
    nd = len(shape)
    return pl.BlockSpec((None,) * len(lead) + tuple(shape), lambda *_: tuple(lead) + (0,) * nd,
                        pipeline_mode=pl.Buffered(1))


def _adaln_kernel(c_ref, w_ref, b_ref, o_ref):
    s = _silu(c_ref[...])
    o_ref[...] = jnp.dot(s, w_ref[...], preferred_element_type=F32,
                         precision=lax.Precision.HIGHEST) + b_ref[...]


def _adaln(cvec, w_ada, b_ada):
    n = N_MOD * D_MODEL
    tn = 1152
    out = pl.pallas_call(
        _adaln_kernel,
        grid=(DEPTH, n // tn),
        in_specs=[
            pl.BlockSpec((MOD_ROWS, D_MODEL), lambda l, j: (0, 0)),
            pl.BlockSpec((None, D_MODEL, tn), lambda l, j: (l, 0, j)),
            pl.BlockSpec((None, 1, tn), lambda l, j: (l, 0, j)),
        ],
        out_specs=pl.BlockSpec((None, MOD_ROWS, tn), lambda l, j: (l, 0, j)),
        out_shape=jax.ShapeDtypeStruct((DEPTH, MOD_ROWS, n), F32),
        compiler_params=pltpu.CompilerParams(
            dimension_semantics=("arbitrary", "arbitrary"), vmem_limit_bytes=VMEM_LIMIT),
        name="adaln",
    )(cvec, w_ada, b_ada.reshape(DEPTH, 1, n))
    return out.reshape(DEPTH, MOD_ROWS, N_MOD, D_MODEL)


_ROWS_PER_TILE = TOKEN_TILE // S5_CHUNK
_SLOTS = LANES // S5_CH


def _to_group_major(u_scr, xg_ref, row0, n_rows):
    c0, n_chunks = row0 // S5_CHUNK, n_rows // S5_CHUNK
    slot = lax.broadcasted_iota(jnp.int32, (n_chunks, LANES), 1) // S5_CH
    for g in range(S5_GROUPS):
        for tt in range(S5_CHUNK // _SLOTS):
            acc = None
            for j in range(_SLOTS):
                t = tt * _SLOTS + j
                src = u_scr[g // _SLOTS, pl.ds(row0 + t, n_chunks, stride=S5_CHUNK), :]
                shift = ((j - g % _SLOTS) % _SLOTS) * S5_CH
                piece = pltpu.roll(src, shift, 1) if shift else src
                acc = piece if acc is None else jnp.where(slot == j, piece, acc)
            xg_ref[g, c0:c0 + n_chunks, tt * LANES:(tt + 1) * LANES] = acc.astype(BF16)
        yield


def _from_group_major(yg_ref, y_scr, row0, n_rows):
    c0, n_chunks = row0 // S5_CHUNK, n_rows // S5_CHUNK
    slot = lax.broadcasted_iota(jnp.int32, (n_chunks, LANES), 1) // S5_CH
    for t in range(S5_CHUNK):
        for gc in range(S5_GROUPS // _SLOTS):
            acc = None
            for m in range(_SLOTS):
                src = yg_ref[gc * _SLOTS + m, c0:c0 + n_chunks, (t // _SLOTS) * LANES:(t // _SLOTS + 1) * LANES]
                shift = ((m - t % _SLOTS) % _SLOTS) * S5_CH
                piece = pltpu.roll(src, shift, 1) if shift else src
                acc = piece if acc is None else jnp.where(slot == m, piece, acc)
            y_scr[gc, pl.ds(row0 + t, n_chunks, stride=S5_CHUNK), :] = acc
        yield


def _interleave(stages):
    stages = list(stages)
    while stages:
        for g in list(stages):
            try:
                next(g)
            except StopIteration:
                stages.remove(g)


def _delayed(gen, steps):
    for _ in range(steps):
        yield
    yield from gen


_MIX_PARTS = 2
_MIX_STAGGER = 8


def _ffn_kernel(mix, final, mod_base, *refs):
    it = iter(refs)
    x_ref, mod_ref, nw_ref, wi_ref, wo_ref = (next(it) for _ in range(5))
    if mix:
        oa_ref, ob_ref, ys_ref, wglu_ref, bglu_ref, wout_ref = (next(it) for _ in range(6))
    if final:
        fnw_ref = next(it)
    out_ref, h_scr, acc_scr = next(it), next(it), next(it)

    if mix:
        y_scr = next(it)
    shift = mod_ref[0, mod_base:mod_base + 1, :]
    scale = mod_ref[0, mod_base + 1:mod_base + 2, :]
    gate = mod_ref[0, mod_base + 2:mod_base + 3, :]
    tm = x_ref.shape[0]
    n_part = _MIX_PARTS if mix else 1
    rows_part = tm // n_part

    def chunk_cols(j):
        col = j * FF_CHUNK
        return slice(col, col + FF_CHUNK), slice(D_FF + col, D_FF + col + FF_CHUNK)

    def prologue(p):
        r0 = p * rows_part
        rows = slice(r0, r0 + rows_part)
        x = x_ref[rows, :]
        if mix:
            yield from _from_group_major(ys_ref, y_scr, r0, rows_part)
            yc = _gelu_tanh(jnp.concatenate([y_scr[gc, rows, :] for gc in range(S5_WIDTH // LANES)], axis=1))
            oc = yc * jax.nn.sigmoid(_dot(yc.astype(BF16), wglu_ref[...]) + bglu_ref[...])
            yield
            mixed = jnp.concatenate([oa_ref[s, rows, :] for s in range(N_SLABS)]
                                    + [ob_ref[s, rows, :] for s in range(N_SLABS)]
                                    + [oc.astype(BF16)], axis=1)
            x = x + mod_ref[0, mod_base - 1:mod_base, :] * _dot(mixed, wout_ref[...])
            out_ref[rows, :] = x
            yield
        h = _rms(x, nw_ref[...]) * (1.0 + scale) + shift
        h_scr[rows, :] = h.astype(BF16)
        yield

    _interleave(_delayed(prologue(p), p * _MIX_STAGGER) for p in range(n_part))
    acc_scr[...] = jnp.zeros_like(acc_scr)

    for j in range(N_FF_CHUNKS):
        hb = h_scr[...]
        ca, cg = chunk_cols(j)
        a = _dot(hb, wi_ref[:, ca])
        g = _dot(hb, wi_ref[:, cg])
        acc_scr[...] += _dot((_silu(g) * a).astype(BF16), wo_ref[ca, :])
    y = (out_ref[...] if mix else x_ref[...]) + 0.5 * gate * acc_scr[...]
    if final:
        y = _rms(y, fnw_ref[...])
    out_ref[...] = y


def _mod_spec(layer, seg_len, row_base):
    return pl.BlockSpec((None, 1, N_MOD, D_MODEL),
                        lambda i: (layer, row_base + (i * TOKEN_TILE) // seg_len, 0, 0))


def _ffn_call(x, mod, seg_len, row_base, norm_w4, layer, which, ffn_w, mix_args=None, mix_w=None, final_w=None):
    t = x.shape[0]
    tm = TOKEN_TILE
    mix = mix_args is not None
    final = final_w is not None
    w_in_all, w_out_all = ffn_w
    tok = lambda i: (i, 0)
    in_specs = [
        pl.BlockSpec((tm, D_MODEL), tok),
        _mod_spec(layer, seg_len, row_base),
        _resident_at((1, D_MODEL), layer, 2 * which),
        _resident_at((D_MODEL, 2 * D_FF), layer, which),
        _resident_at((D_FF, D_MODEL), layer, which),
    ]
    args = [x, mod, norm_w4, w_in_all, w_out_all]
    if mix:
        in_specs += [
            pl.BlockSpec((N_SLABS, tm, LANES), lambda i: (0, i, 0)),
            pl.BlockSpec((N_SLABS, tm, LANES), lambda i: (0, i, 0)),
            pl.BlockSpec((S5_GROUPS, _ROWS_PER_TILE, S5_CHUNK * S5_CH), lambda i: (0, i, 0)),
            _resident_at((S5_WIDTH, S5_WIDTH), layer),
            _resident_at((1, S5_WIDTH), layer),
            _resident_at((D_MODEL, D_MODEL), layer),
        ]
        args += [*mix_args, *mix_w]
    if final:
        in_specs.append(_resident((1, D_MODEL)))
        args.append(final_w.reshape(1, D_MODEL))
    return pl.pallas_call(
        functools.partial(_ffn_kernel, mix, final, 6 * which),
        grid=(t // tm,),
        in_specs=in_specs,
        out_specs=pl.BlockSpec((tm, D_MODEL), tok),
        out_shape=jax.ShapeDtypeStruct((t, D_MODEL), F32),
        scratch_shapes=[pltpu.VMEM((tm, D_MODEL), BF16), pltpu.VMEM((tm, D_MODEL), F32)]
        + ([pltpu.VMEM((S5_WIDTH // LANES, tm, LANES), F32)] if mix else []),
        compiler_params=pltpu.CompilerParams(
            dimension_semantics=("arbitrary",), vmem_limit_bytes=VMEM_LIMIT),
        name="mix_ffn" if mix else "ffn",
    )(*args)


_C_Q, _C_K, _C_V = 0, DA_WIDTH, 2 * DA_WIDTH
_C_CQ = 3 * DA_WIDTH
_C_CKV = _C_CQ + MLA_Q_RANK
_C_U = _C_CKV + MLA_KV_RANK
_C_KPE = _C_U + S5_WIDTH
_NCOL = _C_KPE + LANES

_QSCALE_DIFF = DA_QK ** -0.5 * math.log2(math.e)
_QSCALE_MLA = (MLA_NOPE + MLA_ROPE) ** -0.5 * math.log2(math.e)


_INPROJ_PARTS = 2
_INPROJ_STAGGER = 6


def _rotate_half(x):
    half = DA_QK // 4
    lane = lax.broadcasted_iota(jnp.int32, x.shape, 1)
    return jnp.where(lane % (2 * half) < half, pltpu.roll(x, LANES - half, 1), pltpu.roll(x, half, 1))


def _inproj_kernel(rope, n_aliased, *refs):
    it = iter(refs)
    x_ref, mod_ref, nw_ref, w_ref, qnw_ref, kvnw_ref, wqup_ref, wk_ref, wv_ref = (next(it) for _ in range(9))
    if rope:
        cos_ref, sin_ref, cosm_ref, sinm_ref = (next(it) for _ in range(4))
    for _ in range(n_aliased):
        next(it)
    q_ref, k_ref, vt_ref, qm_ref, km_ref, vmt_ref, xg_ref = (next(it) for _ in range(7))
    if not rope:
        ka_ref, va_ref, ckv_ref, kpe_ref = (next(it) for _ in range(4))
    u_scr = next(it)

    tm = x_ref.shape[0]
    n_part = _INPROJ_PARTS
    rows_part = tm // n_part

    def part(p):
        r0 = p * rows_part
        rows = slice(r0, r0 + rows_part)
        h = _rms(x_ref[rows, :], nw_ref[...]) * (1.0 + mod_ref[0, 4:5, :]) + mod_ref[0, 3:4, :]
        h = h.astype(BF16)
        yield
        qkv = _dot(h, w_ref[:, :_C_CQ])
        yield
        rest = _dot(h, w_ref[:, _C_CQ:])
        yield

        def cols(start, width):
            src, off = (qkv, 0) if start < _C_CQ else (rest, _C_CQ)
            return src[:, start - off:start - off + width]

        if rope:
            cos, sin, cosm, sinm = (r[rows, :] for r in (cos_ref, sin_ref, cosm_ref, sinm_ref))
        for s in range(N_SLABS):
            qs = cols(_C_Q + s * LANES, LANES)
            ks = cols(_C_K + s * LANES, LANES)
            if rope:
                qs = qs * cos + _rotate_half(qs) * sin
                ks = ks * cos + _rotate_half(ks) * sin
            q_ref[s, rows, :] = (qs * _QSCALE_DIFF).astype(BF16)
            k_ref[s, rows, :] = ks.astype(BF16)
            vt_ref[s, :, rows] = cols(_C_V + s * LANES, LANES).T.astype(BF16)
            yield
        for gc in range(S5_WIDTH // LANES):
            u_scr[gc, rows, :] = cols(_C_U + gc * LANES, LANES)
        yield from _to_group_major(u_scr, xg_ref, r0, rows_part)

        cqn = _rms(cols(_C_CQ, MLA_Q_RANK), qnw_ref[...]).astype(BF16)
        qm = _dot(cqn, wqup_ref[...])
        ckvn = _rms(cols(_C_CKV, MLA_KV_RANK), kvnw_ref[...])
        ckvb = ckvn.astype(BF16)
        kn = _dot(ckvb, wk_ref[...])
        vm = _dot(ckvb, wv_ref[...])
        yield
        kpe = cols(_C_KPE, LANES)
        if rope:
            kpe = kpe * cosm + _rotate_half(kpe) * sinm
        for hd in range(MLA_HEADS):
            qh = qm[:, hd * LANES:(hd + 1) * LANES]
            if rope:
                qh = qh * cosm + _rotate_half(qh) * sinm
            qm_ref[hd, rows, :] = (qh * _QSCALE_MLA).astype(BF16)
            km_ref[hd, rows, :] = (kn[:, hd * LANES:(hd + 1) * LANES] + kpe).astype(BF16)
            yield
        for s in range(N_SLABS):
            vmt_ref[s, :, rows] = vm[:, s * LANES:(s + 1) * LANES].T.astype(BF16)

        if not rope:
            seq = ka_ref.shape[-2]
            new = (cols(_C_K, DA_WIDTH), cols(_C_V, DA_WIDTH), ckvn, cols(_C_KPE, MLA_ROPE))
            for b in range(r0 // seq, (r0 + rows_part) // seq):
                mine = slice(b * seq - r0, (b + 1) * seq - r0)
                for ref, val in zip((ka_ref, va_ref, ckv_ref, kpe_ref), new):
                    if len(ref.shape) == 3:
                        ref[b] = val[mine]
                    else:
                        ref[b, 0] = val[mine]
                        for later in range(1, ref.shape[1]):
                            ref[b, later] = jnp.zeros_like(val[mine])

    _interleave(_delayed(part(p), p * _INPROJ_STAGGER) for p in range(n_part))


def _inproj_call(x, mod, seg_len, row_base, norm_w4, layer, w, mla_w, rope_tabs, cache_out=None):
    t = x.shape[0]
    tm = TOKEN_TILE
    rope = rope_tabs is not None
    qnw, kvnw, wqup, wk, wv = mla_w
    tok = lambda i: (i, 0)
    slab = lambda i: (0, i, 0)
    slab_t = lambda i: (0, 0, i)
    in_specs = [
        pl.BlockSpec((tm, D_MODEL), tok),
        _mod_spec(layer, seg_len, row_base),
        _resident_at((1, D_MODEL), layer, 1),
        _resident_at((D_MODEL, _NCOL), layer),
        _resident_at((1, MLA_Q_RANK), layer),
        _resident_at((1, MLA_KV_RANK), layer),
        _resident_at((MLA_Q_RANK, MLA_HEADS * LANES), layer),
        _resident_at((MLA_KV_RANK, MLA_HEADS * LANES), layer),
        _resident_at((MLA_KV_RANK, MLA_WIDTH), layer),
    ]
    args = [x, mod, norm_w4, w, qnw, kvnw, wqup, wk, wv]
    if rope:
        n_pos = rope_tabs[0].shape[0]
        pos = lambda i: (i % (n_pos // tm), 0)
        in_specs += [pl.BlockSpec((tm, LANES), pos)] * 4
        args += list(rope_tabs)
    out_shape = [
        jax.ShapeDtypeStruct((N_SLABS, t, LANES), BF16),
        jax.ShapeDtypeStruct((N_SLABS, t, LANES), BF16),
        jax.ShapeDtypeStruct((N_SLABS, LANES, t), BF16),
        jax.ShapeDtypeStruct((MLA_HEADS, t, LANES), BF16),
        jax.ShapeDtypeStruct((MLA_HEADS, t, LANES), BF16),
        jax.ShapeDtypeStruct((N_SLABS, LANES, t), BF16),
        jax.ShapeDtypeStruct((S5_GROUPS, t // S5_CHUNK, S5_CHUNK * S5_CH), BF16),
    ]
    out_specs = [
        pl.BlockSpec((N_SLABS, tm, LANES), slab),
        pl.BlockSpec((N_SLABS, tm, LANES), slab),
        pl.BlockSpec((N_SLABS, LANES, tm), slab_t),
        pl.BlockSpec((MLA_HEADS, tm, LANES), slab),
        pl.BlockSpec((MLA_HEADS, tm, LANES), slab),
        pl.BlockSpec((N_SLABS, LANES, tm), slab_t),
        pl.BlockSpec((S5_GROUPS, _ROWS_PER_TILE, S5_CHUNK * S5_CH), slab),
    ]
    aliases = {}
    if not rope:
        seq, stacked = cache_out
        n_first = len(out_shape)
        for k, width in enumerate((DA_WIDTH, DA_WIDTH, MLA_KV_RANK, MLA_ROPE)):
            out_shape.append(jax.ShapeDtypeStruct((t // seq, DEPTH, seq, width), F32))
            if stacked is None:
                out_specs.append(pl.BlockSpec((tm // seq, DEPTH, seq, width), lambda i: (i, 0, 0, 0)))
            else:
                out_specs.append(pl.BlockSpec((tm // seq, None, seq, width), lambda i: (i, layer, 0, 0)))
                aliases[len(args)] = n_first + k
                in_specs.append(pl.BlockSpec(memory_space=pl.ANY))
                args.append(stacked[k])
    return pl.pallas_call(
        functools.partial(_inproj_kernel, rope, len(aliases)),
        grid=(t // tm,),
        in_specs=in_specs,
        out_specs=out_specs,
        out_shape=out_shape,
        input_output_aliases=aliases,
        scratch_shapes=[pltpu.VMEM((S5_WIDTH // LANES, tm, LANES), F32)],
        compiler_params=pltpu.CompilerParams(
            dimension_semantics=("arbitrary",), vmem_limit_bytes=VMEM_LIMIT),
        name="inproj_dec" if rope else "inproj_ctx",
    )(*args)


def _mla_expand_kernel(ckv_ref, kpe_ref, wk_ref, wv_ref, km_ref, vmt_ref):
    ckvb = ckv_ref[...].astype(BF16)
    kn = _dot(ckvb, wk_ref[...])
    vm = _dot(ckvb, wv_ref[...])
    kpe = kpe_ref[...]
    for hd in range(MLA_HEADS):
        km_ref[hd] = (kn[:, hd * LANES:(hd + 1) * LANES] + kpe).astype(BF16)
    for s in range(N_SLABS):
        vmt_ref[s] = vm[:, s * LANES:(s + 1) * LANES].T.astype(BF16)


def _mla_expand_call(ckv, kpe128, wk, wv, layer):
    n_batch, _, rows, _ = ckv.shape
    t = n_batch * rows
    return pl.pallas_call(
        _mla_expand_kernel,
        grid=(n_batch,),
        in_specs=[
            pl.BlockSpec((None, None, rows, MLA_KV_RANK), lambda i: (i, layer, 0, 0)),
            pl.BlockSpec((None, None, rows, LANES), lambda i: (i, layer, 0, 0)),
            _resident_at((MLA_KV_RANK, MLA_HEADS * LANES), layer),
            _resident_at((MLA_KV_RANK, MLA_WIDTH), layer),
        ],
        out_specs=[
            pl.BlockSpec((MLA_HEADS, rows, LANES), lambda i: (0, i, 0)),
            pl.BlockSpec((N_SLABS, LANES, rows), lambda i: (0, 0, i)),
        ],
        out_shape=[
            jax.ShapeDtypeStruct((MLA_HEADS, t, LANES), BF16),
            jax.ShapeDtypeStruct((N_SLABS, LANES, t), BF16),
        ],
        compiler_params=pltpu.CompilerParams(dimension_semantics=("arbitrary",)),
        name="mla_expand",
    )(ckv, kpe128, wk, wv)


KEY_TILE = 256
SHORT_SEQS_PER_STEP = 2
N_SCORE_BUFS = 4


def _attn_kernel(n_parts, lam_init, n_seq, *refs):
    q_ref, qm_ref, lamp_ref, subw_ref = refs[:4]
    parts = [refs[4 + 4 * i:8 + 4 * i] for i in range(n_parts)]
    oa_ref, ob_ref, sbuf, qt_scr = refs[4 + 4 * n_parts:]

    lp = lamp_ref[...]
    lam = (jnp.exp(jnp.sum(lp[0:1] * lp[1:2], keepdims=True))
           - jnp.exp(jnp.sum(lp[2:3] * lp[3:4], keepdims=True)) + lam_init)
    tq = q_ref.shape[1] // n_seq
    low_half = lax.broadcasted_iota(jnp.int32, (tq, LANES), 1) < DA_V
    channel = lax.broadcasted_iota(jnp.int32, (LANES, tq), 0)
    subw = subw_ref[...] * (1.0 - lam_init)
    sublanes = 8
    n_qt = N_SLABS + MLA_HEADS

    tiles, row = [], 0
    for p, (k, _, _, _) in enumerate(parts):
        per_seq = k.shape[1] // n_seq
        for k0 in range(0, per_seq, KEY_TILE):
            n = min(KEY_TILE, per_seq - k0)
            tiles.append((p, k0, n, slice(row, row + n)))
            row += n

    def key_slice(unit, p, k0, n):
        first = unit['seq'] * (parts[p][0].shape[1] // n_seq) + k0
        return slice(first, first + n)

    def row_fold(op, acc, x):
        for j in range(x.shape[0] // sublanes):
            piece = x[j * sublanes:(j + 1) * sublanes, :]
            acc = piece if acc is None else op(acc, piece)
        return acc

    for j in range(n_seq):
        mine = slice(j * tq, (j + 1) * tq)
        for s in range(N_SLABS):
            qt_scr[j * n_qt + s] = q_ref[s, mine, :].astype(F32).T.astype(BF16)
        for hd in range(MLA_HEADS):
            qt_scr[j * n_qt + N_SLABS + hd] = qm_ref[hd, mine, :].astype(F32).T.astype(BF16)

    def scores(unit):
        buf = unit['buf']
        if unit['mla']:
            rhs, kref, kidx = qt_scr[unit['seq'] * n_qt + N_SLABS + unit['head']], 2, unit['head']
        else:
            qt = qt_scr[unit['seq'] * n_qt + unit['slab']]
            rhs, kref, kidx = jnp.where(channel // DA_QK == unit['sub'], qt, jnp.zeros_like(qt)), 0, unit['slab']
        mx = None
        for p, k0, n, rows in tiles:
            s = _dot(parts[p][kref][kidx, key_slice(unit, p, k0, n), :], rhs)
            sbuf[buf, rows, :] = s
            mx = row_fold(jnp.maximum, mx, s)
            yield
        unit['max'] = jnp.max(mx, axis=0, keepdims=True)

    def expo(unit):
        buf, tot = unit['buf'], None
        for _, _, _, rows in tiles:
            e = jnp.exp2(sbuf[buf, rows, :] - unit['max'])
            sbuf[buf, rows, :] = e
            tot = row_fold(jnp.add, tot, e)
            yield
        unit['sum'] = jnp.sum(tot, axis=0, keepdims=True)

    def pv(head):
        units, slab = head['units'], head['slab']
        vref = 3 if units[0]['mla'] else 1
        l0 = units[0]['sum']
        if len(units) == 2:
            r = lam * l0 / units[1]['sum']
        acc = None
        for p, k0, n, rows in tiles:
            w = sbuf[units[0]['buf'], rows, :]
            if len(units) == 2:
                w = w - r * sbuf[units[1]['buf'], rows, :]
            o = _dot(parts[p][vref][slab, :, key_slice(units[0], p, k0, n)], w.astype(BF16))
            acc = o if acc is None else acc + o
            yield
        head['out'] = (acc * (1.0 / l0)).T

    def finish(lo, hi):
        slab, mine = hi['slab'], slice(hi['seq'] * tq, (hi['seq'] + 1) * tq)
        o = jnp.where(low_half, lo['out'], hi['out'])
        if hi['mla']:
            ob_ref[slab, mine, :] = o.astype(BF16)
            return
        o2 = o * o
        ss_lo = jnp.sum(jnp.where(low_half, o2, 0.0), axis=1, keepdims=True)
        ss_hi = jnp.sum(jnp.where(low_half, 0.0, o2), axis=1, keepdims=True)
        ms = jnp.where(low_half, ss_lo, ss_hi) * (1.0 / DA_V)
        oa_ref[slab, mine, :] = (o * lax.rsqrt(ms + EPS) * subw).astype(BF16)

    def stream(seq):
        units, heads = [], []
        for slab in range(N_SLABS):
            for mla in (False, True):
                for half in range(2):
                    mine = [dict(seq=seq, mla=mla, slab=slab, head=2 * slab + half, sub=2 * half + c)
                            for c in range(1 if mla else 2)]
                    heads.append(dict(seq=seq, units=mine, slab=slab, mla=mla, half=half,
                                      last=len(units) + len(mine) - 1))
                    units += mine
        for k, unit in enumerate(units):
            unit['buf'] = seq * N_SCORE_BUFS + k % N_SCORE_BUFS
        return units, heads

    streams = [stream(j) for j in range(n_seq)]
    n_units = len(streams[0][0])
    pending = {}
    for k in range(n_units + 2):
        stages, ready = [], []
        for units, heads in streams:
            if k < n_units:
                stages.append(scores(units[k]))
            if 1 <= k <= n_units:
                stages.append(expo(units[k - 1]))
            for h in heads:
                if h['last'] == k - 2:
                    stages.append(pv(h))
                    ready.append(h)
        _interleave(stages)
        for h in ready:
            key = (h['seq'], h['slab'], h['mla'])
            if h['half'] == 1:
                finish(pending.pop(key), h)
            else:
                pending[key] = h


def _attn_call(q, qm, lamp, subw, parts, n_batch, layer, lam_init):
    t = q.shape[1]
    seq = t // n_batch
    tq = min(Q_TILE, seq)
    nq = seq // tq
    n_seq = SHORT_SEQS_PER_STEP if nq == 1 and n_batch % SHORT_SEQS_PER_STEP == 0 else 1
    qmap = lambda b, i: (0, b * nq + i, 0)
    in_specs = [
        pl.BlockSpec((N_SLABS, n_seq * tq, LANES), qmap),
        pl.BlockSpec((MLA_HEADS, n_seq * tq, LANES), qmap),
        pl.BlockSpec((None, 4, DA_QK), lambda b, i: (layer, 0, 0)),
        pl.BlockSpec((None, 1, LANES), lambda b, i: (layer, 0, 0)),
    ]
    args = [q, qm, lamp, subw]

    def keys_spec(arr, n_keys, transposed):
        n_keys = n_seq * n_keys
        block = (arr.shape[-3], LANES, n_keys) if transposed else (arr.shape[-3], n_keys, LANES)
        at = (lambda b: (0, 0, b)) if transposed else (lambda b: (0, b, 0))
        if arr.ndim == 4:
            return pl.BlockSpec((None,) + block, lambda b, i: (layer,) + at(b))
        return pl.BlockSpec(block, lambda b, i: at(b))

    for k, vt, km, vmt, n_keys in parts:
        in_specs += [keys_spec(k, n_keys, False), keys_spec(vt, n_keys, True),
                     keys_spec(km, n_keys, False), keys_spec(vmt, n_keys, True)]
        args += [k, vt, km, vmt]
    return pl.pallas_call(
        functools.partial(_attn_kernel, len(parts), lam_init, n_seq),
        grid=(n_batch // n_seq, nq),
        in_specs=in_specs,
        out_specs=[pl.BlockSpec((N_SLABS, n_seq * tq, LANES), qmap)] * 2,
        out_shape=[jax.ShapeDtypeStruct((N_SLABS, t, LANES), BF16)] * 2,
        scratch_shapes=[pltpu.VMEM((n_seq * N_SCORE_BUFS, sum(p[4] for p in parts), tq), F32),
                        pltpu.VMEM((n_seq * (N_SLABS + MLA_HEADS), LANES, tq), BF16)],
        compiler_params=pltpu.CompilerParams(
            dimension_semantics=("arbitrary", "arbitrary"), vmem_limit_bytes=VMEM_LIMIT),
        name="attn_dec" if len(parts) > 1 else "attn_ctx",
    )(*args)


S5_GROUPS_PER_STEP = 4


def _s5_kernel(n_batch, n_chunks, x_ref, kmat_ref, bst_ref, cst_ref, lamt_ref, h0_ref, y_ref, fin_ref,
               s_scr, hin_scr):
    n_grp = x_ref.shape[0]
    ys = []
    for gi in range(n_grp):
        x = x_ref[gi]
        ys.append(_dot(x, kmat_ref[gi]))
        for k in range(2):
            s_scr[gi, k] = _dot(x, bst_ref[gi, k])

    lam_t = [(lamt_ref[gi, 0], lamt_ref[gi, 1]) for gi in range(n_grp)]
    fwd_lanes = lax.broadcasted_iota(jnp.int32, (n_batch, LANES), 1) < S5_STATE

    def chunk_rows(j):
        return pl.ds(j, n_batch, stride=n_chunks)

    def step(j, carry):
        rf, rb = chunk_rows(j), chunk_rows(n_chunks - 1 - j)
        new = []
        for gi in range(n_grp):
            h_re, h_im = carry[2 * gi], carry[2 * gi + 1]
            a_re, a_im = lam_t[gi]
            hin_scr[gi, 0, rf, :] = h_re
            hin_scr[gi, 1, rf, :] = h_im
            hin_scr[gi, 2, rb, :] = h_re
            hin_scr[gi, 3, rb, :] = h_im
            s_re = jnp.where(fwd_lanes, s_scr[gi, 0, rf, :], s_scr[gi, 0, rb, :])
            s_im = jnp.where(fwd_lanes, s_scr[gi, 1, rf, :], s_scr[gi, 1, rb, :])
            new += [a_re * h_re - a_im * h_im + s_re, a_re * h_im + a_im * h_re + s_im]
        return tuple(new)

    fin = lax.fori_loop(0, n_chunks, step,
                        tuple(h0_ref[gi, k] for gi in range(n_grp) for k in range(2)))
    fwd_rows = lax.broadcasted_iota(jnp.int32, hin_scr.shape[2:], 1) < S5_STATE
    for gi in range(n_grp):
        y = ys[gi]
        for k in range(2):
            fin_ref[gi, k] = fin[2 * gi + k]
            hin = jnp.where(fwd_rows, hin_scr[gi, k], hin_scr[gi, 2 + k])
            y += _dot(hin.astype(BF16), cst_ref[gi, k])
        y_ref[gi] = y


def _s5_call(xg, kmat, bst, cst, lamt, h0, n_batch, layer, h0_layer):
    rows = xg.shape[1]
    n_chunks = rows // n_batch
    wide = S5_CHUNK * S5_CH
    gp = S5_GROUPS_PER_STEP
    grp = lambda g: (g, 0, 0)
    grp4 = lambda g: (g, 0, 0, 0)
    return pl.pallas_call(
        functools.partial(_s5_kernel, n_batch, n_chunks),
        grid=(S5_GROUPS // gp,),
        in_specs=[
            pl.BlockSpec((gp, rows, wide), grp),
            pl.BlockSpec((None, gp, wide, wide), lambda g: (layer, g, 0, 0)),
            pl.BlockSpec((None, gp, 2, wide, LANES), lambda g: (layer, g, 0, 0, 0)),
            pl.BlockSpec((None, gp, 2, LANES, wide), lambda g: (layer, g, 0, 0, 0)),
            pl.BlockSpec((None, gp, 2, 1, LANES), lambda g: (layer, g, 0, 0, 0)),
            pl.BlockSpec((None, gp, 2, n_batch, LANES), lambda g: (h0_layer, g, 0, 0, 0)),
        ],
        out_specs=[
            pl.BlockSpec((gp, rows, wide), grp),
            pl.BlockSpec((gp, 2, n_batch, LANES), grp4),
        ],
        out_shape=[
            jax.ShapeDtypeStruct((S5_GROUPS, rows, wide), F32),
            jax.ShapeDtypeStruct((S5_GROUPS, 2, n_batch, LANES), F32),
        ],
        scratch_shapes=[pltpu.VMEM((gp, 2, rows, LANES), F32), pltpu.VMEM((gp, 4, rows, LANES), F32)],
        compiler_params=pltpu.CompilerParams(dimension_semantics=("arbitrary",)),
        name="s5",
    )(xg, kmat, bst, cst, lamt, h0)


def _s5_operators(a_re, a_im, log_step, b_re, b_im, c_re, c_im, d):
    hi = lax.Precision.HIGHEST
    n = S5_CHUNK
    wide = n * S5_CH
    ar, ai = a_re.astype(F32), a_im.astype(F32)
    step = jnp.exp(log_step.astype(F32))[..., None]
    zr, zi = ar * step, ai * step
    k = jnp.arange(n + 1, dtype=F32)[:, None, None, None]
    mag = jnp.exp(k * zr[None])
    pr, pi = mag * jnp.cos(k * zi[None]), mag * jnp.sin(k * zi[None])

    xr, xi = pr[1] - 1.0, pi[1]
    den = ar * ar + ai * ai
    fr, fi = ((xr * ar + xi * ai) / den)[:, :, None, :], ((xi * ar - xr * ai) / den)[:, :, None, :]
    br, bi = (jnp.swapaxes(b.astype(F32), 2, 3) for b in (b_re, b_im))
    bbr, bbi = fr * br - fi * bi, fr * bi + fi * br
    cr, ci = c_re.astype(F32), c_im.astype(F32)

    tt = np.arange(n)

    def state_readout(powers, dirn):
        qr = pr[powers, dirn].transpose(1, 2, 0)[:, :, :, None]
        qi = pi[powers, dirn].transpose(1, 2, 0)[:, :, :, None]
        ctr = jnp.swapaxes(cr[dirn], 1, 2)[:, :, None, :]
        cti = jnp.swapaxes(ci[dirn], 1, 2)[:, :, None, :]
        rr = ctr * qr - cti * qi
        ri = -(ctr * qi + cti * qr)
        return rr.reshape(S5_GROUPS, S5_STATE, wide), ri.reshape(S5_GROUPS, S5_STATE, wide)

    def lag_response(powers, dirn):
        rr, ri = state_readout(powers, dirn)
        return (jnp.einsum('gkp,gpn->gkn', bbr[dirn], rr, precision=hi)
                + jnp.einsum('gkp,gpn->gkn', bbi[dirn], ri, precision=hi))

    fwd, bwd = lag_response(tt, 0), lag_response(n - 1 - tt, 1)
    skip = d.astype(F32)[:, None, :] * jnp.eye(S5_CH, dtype=F32)[None]
    strip = jnp.concatenate([bwd[:, :, :wide - S5_CH], bwd[:, :, wide - S5_CH:] + fwd[:, :, :S5_CH] + skip,
                             fwd[:, :, S5_CH:]], axis=-1)
    kmat = jnp.stack([strip[:, :, (n - 1 - s) * S5_CH:(n - 1 - s) * S5_CH + wide] for s in range(n)], axis=1)
    kmat = kmat.reshape(S5_GROUPS, wide, wide)

    def edge_states(powers, dirn):
        qr = jnp.swapaxes(pr[powers, dirn], 0, 1)[:, :, None, :]
        qi = jnp.swapaxes(pi[powers, dirn], 0, 1)[:, :, None, :]
        er = qr * bbr[dirn][:, None] - qi * bbi[dirn][:, None]
        ei = qr * bbi[dirn][:, None] + qi * bbr[dirn][:, None]
        return er.reshape(S5_GROUPS, wide, S5_STATE), ei.reshape(S5_GROUPS, wide, S5_STATE)

    (bf_r, bf_i), (bb_r, bb_i) = edge_states(n - 1 - tt, 0), edge_states(tt, 1)
    bst = jnp.stack([jnp.concatenate([bf_r, bb_r], axis=2), jnp.concatenate([bf_i, bb_i], axis=2)], axis=1)
    (cf_r, cf_i), (cb_r, cb_i) = state_readout(tt + 1, 0), state_readout(n - tt, 1)
    cst = jnp.stack([jnp.concatenate([cf_r, cb_r], axis=1), jnp.concatenate([cf_i, cb_i], axis=1)], axis=1)
    lamt = jnp.stack([jnp.concatenate([pr[n, 0], pr[n, 1]], axis=1),
                      jnp.concatenate([pi[n, 0], pi[n, 1]], axis=1)], axis=1)
    return kmat.astype(BF16), bst.astype(BF16), cst.astype(BF16), lamt[:, :, None, :]


def _rope_tables(n_pos):
    n = DA_QK // 4
    inv = ROPE_THETA ** (-jnp.arange(n, dtype=F32) / n)
    pos = jnp.arange(n_pos, dtype=jnp.int32)
    ang_r = (pos // GRID_W).astype(F32)[:, None] * inv[None, :]
    ang_c = (pos % GRID_W).astype(F32)[:, None] * inv[None, :]
    cos32 = jnp.concatenate([jnp.cos(ang_r)] * 2 + [jnp.cos(ang_c)] * 2, axis=1)
    sin32 = jnp.concatenate([-jnp.sin(ang_r), jnp.sin(ang_r), -jnp.sin(ang_c), jnp.sin(ang_c)], axis=1)
    pad = LANES - MLA_ROPE
    cosm = jnp.concatenate([cos32, jnp.ones((n_pos, pad), F32)], axis=1)
    sinm = jnp.concatenate([sin32, jnp.zeros((n_pos, pad), F32)], axis=1)
    return jnp.tile(cos32, (1, 4)), jnp.tile(sin32, (1, 4)), cosm, sinm


def _prep_inproj(w):
    o = np.cumsum([0, DA_WIDTH, DA_WIDTH, DA_WIDTH, MLA_Q_RANK, MLA_KV_RANK, MLA_ROPE, S5_WIDTH])
    wq, wk, wv, wcq, wckv, wkpe, wu = (w[..., o[i]:o[i + 1]] for i in range(7))
    wkpe = jnp.pad(wkpe, ((0, 0), (0, 0), (0, LANES - MLA_ROPE)))
    return jnp.concatenate([wq, wk, wv, wcq, wckv, wu, wkpe], axis=-1).astype(BF16)


def _prep_mla(q_norm_w, w_q_up, kv_norm_w, w_kv_up):
    pad = LANES - MLA_ROPE - MLA_NOPE
    wq = w_q_up.reshape(DEPTH, MLA_Q_RANK, MLA_HEADS, MLA_NOPE + MLA_ROPE)
    wqup = jnp.pad(jnp.concatenate([wq[..., MLA_NOPE:], wq[..., :MLA_NOPE]], axis=-1),
                   ((0, 0), (0, 0), (0, 0), (0, pad))).reshape(DEPTH, MLA_Q_RANK, MLA_HEADS * LANES)
    wkv = w_kv_up.reshape(DEPTH, MLA_KV_RANK, MLA_HEADS, MLA_NOPE + MLA_V)
    wk = jnp.pad(wkv[..., :MLA_NOPE], ((0, 0), (0, 0), (0, 0), (MLA_ROPE, pad)))
    wk = wk.reshape(DEPTH, MLA_KV_RANK, MLA_HEADS * LANES)
    wv = wkv[..., MLA_NOPE:].reshape(DEPTH, MLA_KV_RANK, MLA_WIDTH)
    return (q_norm_w.reshape(DEPTH, 1, MLA_Q_RANK), kv_norm_w.reshape(DEPTH, 1, MLA_KV_RANK),
            wqup.astype(BF16), wk.astype(BF16), wv.astype(BF16))


def _cached_slabs(cache, transposed):
    b, _, n = cache.shape[:3]
    a = cache.reshape(b, DEPTH, n, N_SLABS, LANES).astype(BF16)
    a = a.transpose(1, 3, 4, 0, 2) if transposed else a.transpose(1, 3, 0, 2, 4)
    return a.reshape(DEPTH, N_SLABS, LANES, b * n) if transposed else a.reshape(DEPTH, N_SLABS, b * n, LANES)


def kernel(x_prompt, x_sample, cache_diff_k, cache_diff_v, cache_mla_ckv, cache_mla_kpe, state_s5_re, state_s5_im, c, c_ctx, w_ada, b_ada, norm_w, ffn_w_in, ffn_w_out, w_in, diff_lambda, diff_subln_w, mla_q_norm_w, mla_w_q_up, mla_kv_norm_w, mla_w_kv_up, s5_a_re, s5_a_im, s5_log_step, s5_b_re, s5_b_im, s5_c_re, s5_c_im, s5_d, s5_w_glu, s5_b_glu, w_out, final_norm_w):
    n_ctx, seq, _ = x_prompt.shape
    n_dec, dec_seq, _ = x_sample.shape
    past = cache_diff_k.shape[2]

    cvec = jnp.concatenate(
        [c_ctx[None, :], c, jnp.zeros((MOD_ROWS - 1 - n_dec, D_MODEL), F32)], axis=0)
    mod = _adaln(cvec, w_ada, b_ada)
    rope_tabs = _rope_tables(dec_seq)

    groups = {
        'ctx': dict(x=x_prompt.reshape(n_ctx * seq, D_MODEL), seg=n_ctx * seq, row=0, nb=n_ctx),
        'dec': dict(x=x_sample.reshape(n_dec * dec_seq, D_MODEL), seg=dec_seq, row=1, nb=n_dec),
    }
    ffn_w = (ffn_w_in.astype(BF16), ffn_w_out.astype(BF16))
    norm_w4 = norm_w.reshape(DEPTH, 3, 1, D_MODEL)
    w_proj = _prep_inproj(w_in)
    mla_w = _prep_mla(mla_q_norm_w, mla_w_q_up, mla_kv_norm_w, mla_w_kv_up)
    mix_w = (s5_w_glu.astype(BF16), s5_b_glu.reshape(DEPTH, 1, S5_WIDTH), w_out.astype(BF16))
    s5_ops = jax.vmap(_s5_operators)(s5_a_re, s5_a_im, s5_log_step, s5_b_re, s5_b_im, s5_c_re, s5_c_im, s5_d)
    subw = jnp.tile(diff_subln_w, (1, 2)).reshape(DEPTH, 1, LANES)
    ctx_k, ctx_vt = _cached_slabs(cache_diff_k, False), _cached_slabs(cache_diff_v, True)
    ctx_kpe = jnp.pad(cache_mla_kpe, ((0, 0), (0, 0), (0, 0), (0, LANES - MLA_ROPE)))
    pack_dirs = lambda s: jnp.concatenate([s[:, :, 0], s[:, :, 1]], axis=-1)
    h0_dec = jnp.stack([pack_dirs(state_s5_re), pack_dirs(state_s5_im)], 0).transpose(2, 3, 0, 1, 4)
    h0_ctx = jnp.zeros((1, S5_GROUPS, 2, n_ctx, LANES), F32)
    caches = None
    s5_states = []
    for l in range(DEPTH):
        lam_init = 0.8 - 0.6 * math.exp(-0.3 * l)
        final_w = final_norm_w if l == DEPTH - 1 else None
        for name, g in groups.items():
            dec = name == 'dec'
            x = _ffn_call(g['x'], mod, g['seg'], g['row'], norm_w4, l, 0, ffn_w)
            proj = _inproj_call(x, mod, g['seg'], g['row'], norm_w4, l, w_proj, mla_w,
                                rope_tabs if dec else None, cache_out=None if dec else (seq, caches))
            q, k, vt, qm, km, vmt, xg = proj[:7]
            parts = [(k, vt, km, vmt, g['x'].shape[0] // g['nb'])]
            if dec:
                ckm, cvmt = _mla_expand_call(cache_mla_ckv, ctx_kpe, mla_w[3], mla_w[4], l)
                parts.append((ctx_k, ctx_vt, ckm, cvmt, past))
            o_a, o_b = _attn_call(q, qm, diff_lambda, subw, parts, g['nb'], l, lam_init)
            y_s5, fin = _s5_call(xg, *s5_ops, h0_dec if dec else h0_ctx, g['nb'], l, l if dec else 0)
            g['x'] = _ffn_call(x, mod, g['seg'], g['row'], norm_w4, l, 1, ffn_w,
                               mix_args=(o_a, o_b, y_s5), mix_w=mix_w, final_w=final_w)
            if not dec:
                caches = proj[7:]
                fin = fin.reshape(S5_GROUPS, 2, n_ctx, 2, S5_STATE).transpose(1, 2, 3, 0, 4)
                s5_states.append((fin[0], fin[1]))

    y_prompt = groups['ctx']['x'].reshape(n_ctx, seq, D_MODEL)
    y_sample = groups['dec']['x'].reshape(n_dec, dec_seq, D_MODEL)
    new_k, new_v, new_ckv, new_kpe = caches
    new_s5_re = jnp.stack([t[0] for t in s5_states], axis=1)
    new_s5_im = jnp.stack([t[1] for t in s5_states], axis=1)
    return (y_prompt, y_sample,
            new_k.reshape(n_ctx, DEPTH, seq, DA_HEADS, 2 * DA_QK),
            new_v.reshape(n_ctx, DEPTH, seq, DA_HEADS, DA_V),
            new_ckv, new_kpe, new_s5_re, new_s5_im)
```
